```python
import math
import jax, jax.numpy as jnp
from jax import lax
import numpy as np

D_MODEL = 1024
BATCH = 8
SEQ = 2048
DEPTH = 2

HEAD_DIM = 64
CONV_WIDTH = 256
CONV_K = 3
GDN_HEADS = 4
GDN_WIDTH = GDN_HEADS * HEAD_DIM
GDN_CONV_K = 4
GDN_CHUNK = 64
NSA_Q_HEADS = 8
NSA_KV_HEADS = 2
NSA_GROUP = NSA_Q_HEADS // NSA_KV_HEADS
NSA_WIDTH = NSA_Q_HEADS * HEAD_DIM
NSA_KV_WIDTH = 6 * NSA_KV_HEADS * HEAD_DIM
CMP_LEN = 32
CMP_STRIDE = 16
CMP_HIDDEN = 2 * HEAD_DIM
SEL_BLOCK = 64
SEL_TOPN = 8
WINDOW = 512
Q_BLOCK = 128
FORCE_BONUS = 1e3
MIX_WIDTH = CONV_WIDTH + GDN_WIDTH + NSA_WIDTH
IN_SIZES = (CONV_WIDTH, CONV_WIDTH, CONV_WIDTH,
            3 * GDN_WIDTH, GDN_WIDTH, GDN_HEADS, GDN_HEADS,
            NSA_WIDTH, NSA_KV_WIDTH, 3 * NSA_Q_HEADS)
IN_WIDTH = 3 * CONV_WIDTH + 4 * GDN_WIDTH + 2 * GDN_HEADS + NSA_WIDTH + NSA_KV_WIDTH + 3 * NSA_Q_HEADS
D_FF = 2816
DEEPNORM_ALPHA = (2 * DEPTH) ** 0.25
DEEPNORM_BETA = (8 * DEPTH) ** -0.25
LN_EPS = 1e-5
NORM_EPS = 1e-6
NEG = -1e30

kernel_name = "hymba_style_conv_gdn_nsa_macaron_deepnorm"


def layer_norm(x, g, b):
    xf = x.astype(jnp.float32)
    mu = jnp.mean(xf, -1, keepdims=True)
    var = jnp.mean(jnp.square(xf - mu), -1, keepdims=True)
    return ((xf - mu) * lax.rsqrt(var + LN_EPS) * g.astype(jnp.float32) + b.astype(jnp.float32)).astype(x.dtype)


def swiglu(x, w_gate, w_up, w_down):
    return (jax.nn.silu(x @ w_gate) * (x @ w_up)) @ w_down


def causal_dwconv(u, w):
    k = w.shape[-1]
    s = u.shape[1]
    u_pad = jnp.pad(u, ((0, 0), (k - 1, 0), (0, 0)))
    y = u_pad[:, 0:s] * w[:, 0]
    for j in range(1, k):
        y = y + u_pad[:, j:j + s] * w[:, j]
    return y


def l2norm(t):
    return t * lax.rsqrt(jnp.sum(t * t, -1, keepdims=True) + NORM_EPS)


def split_columns(h, sizes):
    out, start = [], 0
    for n in sizes:
        out.append(h[..., start:start + n])
        start += n
    return out


def short_conv_mixer(b_gate, c_gate, h, conv_w):
    return b_gate * causal_dwconv(c_gate * h, conv_w)


def gated_deltanet(qkv, z, a, b, conv_w, a_log, dt_bias, norm_w):
    f32 = jnp.float32
    bsz, s, _ = qkv.shape
    dtype = qkv.dtype
    c = GDN_CHUNK
    nc = s // c
    qkv = jax.nn.silu(causal_dwconv(qkv, conv_w)).astype(f32)
    q, k, v = jnp.split(qkv, 3, axis=-1)
    to_heads = lambda t: t.reshape(bsz, s, GDN_HEADS, HEAD_DIM).transpose(0, 2, 1, 3)
    q = l2norm(to_heads(q)) * HEAD_DIM ** -0.5
    k = l2norm(to_heads(k))
    v = to_heads(v)
    beta = jax.nn.sigmoid(b.astype(f32)).transpose(0, 2, 1)
    g = (-jnp.exp(a_log.astype(f32)) * jax.nn.softplus(a.astype(f32) + dt_bias.astype(f32))).transpose(0, 2, 1)
    chunk = lambda t: t.reshape(bsz, GDN_HEADS, nc, c, *t.shape[3:])
    q, k, v, beta, g = chunk(q), chunk(k), chunk(v), chunk(beta), chunk(g)
    gc = jnp.cumsum(g, axis=-1)
    tril = jnp.tril(jnp.ones((c, c), bool))
    strict = jnp.tril(jnp.ones((c, c), bool), -1)
    diff = gc[..., :, None] - gc[..., None, :]
    decay = jnp.where(tril, jnp.exp(jnp.where(tril, diff, 0.0)), 0.0)
    kb = k * beta[..., None]
    vb = v * beta[..., None]
    lmat = jnp.where(strict, jnp.einsum('bhnid,bhnjd->bhnij', kb, k) * decay, 0.0)
    eye = jnp.eye(c, dtype=f32)
    tinv = lax.linalg.triangular_solve(lmat + eye, jnp.broadcast_to(eye, lmat.shape),
                                       left_side=True, lower=True, unit_diagonal=True)
    u = jnp.einsum('bhnij,bhnjd->bhnid', tinv, vb)
    w = jnp.einsum('bhnij,bhnjd->bhnid', tinv, kb * jnp.exp(gc)[..., None])
    a_qk = jnp.where(tril, jnp.einsum('bhnid,bhnjd->bhnij', q, k) * decay, 0.0)
    g_last = gc[..., -1]
    q_dec = q * jnp.exp(gc)[..., None]
    k_dec = k * jnp.exp(g_last[..., None] - gc)[..., None]

    def step(state, xs):
        q_i, k_i, u_i, w_i, a_i, gl_i = xs
        v_new = u_i - jnp.einsum('bhck,bhkv->bhcv', w_i, state)
        o = jnp.einsum('bhck,bhkv->bhcv', q_i, state) + jnp.einsum('bhij,bhjv->bhiv', a_i, v_new)
        state = state * jnp.exp(gl_i)[..., None, None] + jnp.einsum('bhck,bhcv->bhkv', k_i, v_new)
        return state, o

    xs = tuple(jnp.moveaxis(t, 2, 0) for t in (q_dec, k_dec, u, w, a_qk, g_last))
    state0 = jnp.zeros((bsz, GDN_HEADS, HEAD_DIM, HEAD_DIM), f32)
    _, o = lax.scan(step, state0, xs)
    o = jnp.moveaxis(o, 0, 2).reshape(bsz, GDN_HEADS, s, HEAD_DIM).transpose(0, 2, 1, 3)
    o = o * lax.rsqrt(jnp.mean(o * o, -1, keepdims=True) + NORM_EPS) * norm_w.astype(f32)
    o = o * jax.nn.silu(z.astype(f32).reshape(bsz, s, GDN_HEADS, HEAD_DIM))
    return o.reshape(bsz, s, GDN_WIDTH).astype(dtype)


def nsa_attention(q, kv, gates, pe_k, pe_v, ck_w1, ck_w2, cv_w1, cv_w2):
    f32 = jnp.float32
    bsz, s, _ = q.shape
    dtype = q.dtype
    hk, grp, d = NSA_KV_HEADS, NSA_GROUP, HEAD_DIM
    scale = HEAD_DIM ** -0.5
    q = q.reshape(bsz, s, hk, grp, d).transpose(0, 2, 3, 1, 4)
    kv = kv.reshape(bsz, s, 6, hk, d)
    k_cmp, v_cmp, k_slc, v_slc, k_win, v_win = (kv[:, :, i] for i in range(6))
    t_pos = jnp.arange(s)

    n_cmp = (s - CMP_LEN) // CMP_STRIDE + 1
    starts = jnp.arange(n_cmp) * CMP_STRIDE
    blk_idx = starts[:, None] + jnp.arange(CMP_LEN)[None, :]

    def compress(t, pe, w1, w2):
        blocks = t[:, blk_idx] + pe[None, None, :, None, :]
        flat = blocks.transpose(0, 1, 3, 2, 4).reshape(bsz, n_cmp, hk, CMP_LEN * d)
        return jax.nn.silu(flat @ w1) @ w2

    kc = compress(k_cmp, pe_k, ck_w1, ck_w2)
    vc = compress(v_cmp, pe_v, cv_w1, cv_w2)
    s_cmp = jnp.einsum('bhgtd,bnhd->bhgtn', q, kc).astype(f32) * scale
    cmp_ok = (starts + CMP_LEN - 1)[None, :] <= t_pos[:, None]
    p_cmp = jax.nn.softmax(jnp.where(cmp_ok, s_cmp, NEG), axis=-1)
    p_cmp = jnp.where(cmp_ok.any(-1)[:, None], p_cmp, 0.0)
    o_cmp = jnp.einsum('bhgtn,bnhd->bhgtd', p_cmp.astype(dtype), vc)

    n_sel = s // SEL_BLOCK
    n_top = min(SEL_TOPN, n_sel)
    cmp_tok = (t_pos[None, :] >= starts[:, None]) & (t_pos[None, :] < starts[:, None] + CMP_LEN)
    sel_tok = (t_pos[None, :] // SEL_BLOCK) == jnp.arange(n_sel)[:, None]
    overlap = (cmp_tok.astype(f32) @ sel_tok.astype(f32).T) / CMP_LEN
    imp = jnp.einsum('bhgtn,nj->bhtj', p_cmp, overlap)
    q_blk = t_pos // SEL_BLOCK
    j = jnp.arange(n_sel)
    forced = (j[None, :] == 0) | (j[None, :] == q_blk[:, None]) | (j[None, :] == q_blk[:, None] - 1)
    imp = jnp.where(forced, imp + FORCE_BONUS, imp)
    imp = jnp.where(j[None, :] <= q_blk[:, None], imp, NEG)
    _, sel_idx = lax.top_k(imp, n_top)

    kb_slc = k_slc.reshape(bsz, n_sel, SEL_BLOCK, hk, d).transpose(0, 3, 1, 2, 4)
    vb_slc = v_slc.reshape(bsz, n_sel, SEL_BLOCK, hk, d).transpose(0, 3, 1, 2, 4)
    kw = jnp.pad(k_win.transpose(0, 2, 1, 3), ((0, 0), (0, 0), (WINDOW, 0), (0, 0)))
    vw = jnp.pad(v_win.transpose(0, 2, 1, 3), ((0, 0), (0, 0), (WINDOW, 0), (0, 0)))
    n_qb = s // Q_BLOCK
    b_ix = jnp.arange(bsz)[:, None, None, None]
    h_ix = jnp.arange(hk)[None, :, None, None]

    def block_attn(args):
        qb, idx, i = args
        tq = i * Q_BLOCK + jnp.arange(Q_BLOCK)
        kg = kb_slc[b_ix, h_ix, idx].reshape(bsz, hk, Q_BLOCK, n_top * SEL_BLOCK, d)
        vg = vb_slc[b_ix, h_ix, idx].reshape(bsz, hk, Q_BLOCK, n_top * SEL_BLOCK, d)
        kpos = (idx[..., None] * SEL_BLOCK + jnp.arange(SEL_BLOCK)).reshape(bsz, hk, Q_BLOCK, n_top * SEL_BLOCK)
        ok = kpos <= tq[None, None, :, None]
        sc = jnp.einsum('bhgqd,bhqkd->bhgqk', qb, kg).astype(f32) * scale
        ps = jax.nn.softmax(jnp.where(ok[:, :, None], sc, NEG), axis=-1)
        o_s = jnp.einsum('bhgqk,bhqkd->bhgqd', ps.astype(dtype), vg)
        kwin = lax.dynamic_slice_in_dim(kw, i * Q_BLOCK, WINDOW + Q_BLOCK, axis=2)
        vwin = lax.dynamic_slice_in_dim(vw, i * Q_BLOCK, WINDOW + Q_BLOCK, axis=2)
        wpos = i * Q_BLOCK - WINDOW + jnp.arange(WINDOW + Q_BLOCK)
        dist = tq[:, None] - wpos[None, :]
        wok = (dist >= 0) & (dist < WINDOW) & (wpos[None, :] >= 0)
        sw = jnp.einsum('bhgqd,bhkd->bhgqk', qb, kwin).astype(f32) * scale
        pw = jax.nn.softmax(jnp.where(wok, sw, NEG), axis=-1)
        o_w = jnp.einsum('bhgqk,bhkd->bhgqd', pw.astype(dtype), vwin)
        return o_s, o_w

    q_blocks = q.reshape(bsz, hk, grp, n_qb, Q_BLOCK, d).transpose(3, 0, 1, 2, 4, 5)
    idx_blocks = sel_idx.reshape(bsz, hk, n_qb, Q_BLOCK, n_top).transpose(2, 0, 1, 3, 4)
    o_slc, o_win = lax.map(block_attn, (q_blocks, idx_blocks, jnp.arange(n_qb)))
    unblock = lambda o: o.transpose(1, 2, 3, 0, 4, 5).reshape(bsz, hk, grp, s, d)
    to_tokens = lambda o: o.transpose(0, 3, 1, 2, 4).reshape(bsz, s, NSA_Q_HEADS, d)
    gt = jax.nn.sigmoid(gates.astype(f32)).reshape(bsz, s, NSA_Q_HEADS, 3).astype(dtype)
    o = (gt[..., 0:1] * to_tokens(o_cmp) + gt[..., 1:2] * to_tokens(unblock(o_slc))
         + gt[..., 2:3] * to_tokens(unblock(o_win)))
    return o.reshape(bsz, s, NSA_WIDTH)


def hybrid_mixer(x, w_in, conv_w, gdn_conv_w, gdn_a_log, gdn_dt_bias, gdn_norm_w,
                 cmp_pe_k, cmp_pe_v, cmp_k_w1, cmp_k_w2, cmp_v_w1, cmp_v_w2, w_out):
    h = x @ w_in
    cb, cc, ch, gqkv, gz, ga, gb, nq, nkv, ngate = split_columns(h, IN_SIZES)
    y_a = short_conv_mixer(cb, cc, ch, conv_w)
    y_b = gated_deltanet(gqkv, gz, ga, gb, gdn_conv_w, gdn_a_log, gdn_dt_bias, gdn_norm_w)
    y_c = nsa_attention(nq, nkv, ngate, cmp_pe_k, cmp_pe_v, cmp_k_w1, cmp_k_w2, cmp_v_w1, cmp_v_w2)
    return jnp.concatenate([y_a, y_b, y_c], axis=-1) @ w_out


def setup_inputs(seed: int = 0) -> dict:
    key = jax.random.key(seed)
    keys = jax.random.split(key, 26)
    f32 = jnp.float32
    L = DEPTH

    def nrm(i, shape, scale):
        return jax.random.normal(keys[i], shape, f32) * scale

    dt = jnp.exp(jax.random.uniform(keys[10], (L, GDN_HEADS), f32, math.log(1e-3), math.log(1e-1)))
    return {
        "x": nrm(0, (BATCH, SEQ, D_MODEL), 1.0),
        "ffn1_w_gate": nrm(1, (L, D_MODEL, D_FF), D_MODEL ** -0.5),
        "ffn1_w_up": nrm(2, (L, D_MODEL, D_FF), D_MODEL ** -0.5),
        "ffn1_w_down": nrm(3, (L, D_FF, D_MODEL), D_FF ** -0.5 * DEEPNORM_BETA),
        "ln1_g": 1.0 + nrm(4, (L, D_MODEL), 0.02),
        "ln1_b": nrm(5, (L, D_MODEL), 0.02),
        "w_in": nrm(6, (L, D_MODEL, IN_WIDTH), D_MODEL ** -0.5),
        "conv_w": nrm(7, (L, CONV_WIDTH, CONV_K), CONV_K ** -0.5),
        "gdn_conv_w": nrm(8, (L, 3 * GDN_WIDTH, GDN_CONV_K), GDN_CONV_K ** -0.5),
        "gdn_a_log": jnp.log(jax.random.uniform(keys[9], (L, GDN_HEADS), f32, 1.0, 16.0)),
        "gdn_dt_bias": dt + jnp.log(-jnp.expm1(-dt)),
        "gdn_norm_w": 1.0 + nrm(11, (L, HEAD_DIM), 0.02),
        "cmp_pe_k": nrm(12, (L, CMP_LEN, HEAD_DIM), 0.1),
        "cmp_pe_v": nrm(13, (L, CMP_LEN, HEAD_DIM), 0.1),
        "cmp_k_w1": nrm(14, (L, CMP_LEN * HEAD_DIM, CMP_HIDDEN), (CMP_LEN * HEAD_DIM) ** -0.5),
        "cmp_k_w2": nrm(15, (L, CMP_HIDDEN, HEAD_DIM), CMP_HIDDEN ** -0.5),
        "cmp_v_w1": nrm(16, (L, CMP_LEN * HEAD_DIM, CMP_HIDDEN), (CMP_LEN * HEAD_DIM) ** -0.5),
        "cmp_v_w2": nrm(17, (L, CMP_HIDDEN, HEAD_DIM), CMP_HIDDEN ** -0.5),
        "w_out": nrm(18, (L, MIX_WIDTH, D_MODEL), MIX_WIDTH ** -0.5 * DEEPNORM_BETA),
        "ln2_g": 1.0 + nrm(19, (L, D_MODEL), 0.02),
        "ln2_b": nrm(20, (L, D_MODEL), 0.02),
        "ffn2_w_gate": nrm(21, (L, D_MODEL, D_FF), D_MODEL ** -0.5),
        "ffn2_w_up": nrm(22, (L, D_MODEL, D_FF), D_MODEL ** -0.5),
        "ffn2_w_down": nrm(23, (L, D_FF, D_MODEL), D_FF ** -0.5 * DEEPNORM_BETA),
        "ln3_g": 1.0 + nrm(24, (L, D_MODEL), 0.02),
        "ln3_b": nrm(25, (L, D_MODEL), 0.02),
    }


def reference(x, ffn1_w_gate, ffn1_w_up, ffn1_w_down, ln1_g, ln1_b, w_in, conv_w, gdn_conv_w,
              gdn_a_log, gdn_dt_bias, gdn_norm_w, cmp_pe_k, cmp_pe_v, cmp_k_w1, cmp_k_w2,
              cmp_v_w1, cmp_v_w2, w_out, ln2_g, ln2_b, ffn2_w_gate, ffn2_w_up, ffn2_w_down,
              ln3_g, ln3_b):
    for l in range(DEPTH):
        x = layer_norm(DEEPNORM_ALPHA * x + 0.5 * swiglu(x, ffn1_w_gate[l], ffn1_w_up[l], ffn1_w_down[l]),
                       ln1_g[l], ln1_b[l])
        x = layer_norm(DEEPNORM_ALPHA * x + hybrid_mixer(x, w_in[l], conv_w[l], gdn_conv_w[l], gdn_a_log[l],
                                                         gdn_dt_bias[l], gdn_norm_w[l], cmp_pe_k[l], cmp_pe_v[l],
                                                         cmp_k_w1[l], cmp_k_w2[l], cmp_v_w1[l], cmp_v_w2[l],
                                                         w_out[l]),
                       ln2_g[l], ln2_b[l])
        x = layer_norm(DEEPNORM_ALPHA * x + 0.5 * swiglu(x, ffn2_w_gate[l], ffn2_w_up[l], ffn2_w_down[l]),
                       ln3_g[l], ln3_b[l])
    return x
```

```python
import functools

import jax
import jax.numpy as jnp
from jax import lax
from jax.experimental import pallas as pl
from jax.experimental.pallas import tpu as pltpu

F32 = jnp.float32
BF16 = jnp.bfloat16

HEAD_DIM = 64
CONV_WIDTH = 256
CONV_K = 3
GDN_HEADS = 4
GDN_WIDTH = GDN_HEADS * HEAD_DIM
GDN_CONV_K = 4
GDN_CHUNK = 64
NSA_Q_HEADS = 8
NSA_KV_HEADS = 2
NSA_GROUP = NSA_Q_HEADS // NSA_KV_HEADS
NSA_WIDTH = NSA_Q_HEADS * HEAD_DIM
CMP_LEN = 32
CMP_STRIDE = 16
SEL_BLOCK = 64
SEL_TOPN = 8
WINDOW = 512
FORCE_BONUS = 1e3
LN_EPS = 1e-5
NORM_EPS = 1e-6
NEG = -1e30

V7X_VMEM_LIMIT_BYTES = 56 * 1024 * 1024
LANES = 128
HEAD_SHIFT = HEAD_DIM.bit_length() - 1
SEL_SHIFT = SEL_BLOCK.bit_length() - 1

GDN_COLS = 4 * GDN_WIDTH + LANES
NSA_HEAD_COLS = NSA_GROUP * HEAD_DIM + 6 * HEAD_DIM + LANES


def _layer_norm(r, g, b):
    mu = jnp.mean(r, axis=-1, keepdims=True)
    c = r - mu
    var = jnp.mean(c * c, axis=-1, keepdims=True)
    return c * lax.rsqrt(var + LN_EPS) * g + b


def _const_spec(shape):
    return pl.BlockSpec(shape, lambda *_: (0,) * len(shape), pipeline_mode=pl.Buffered(1))


def _ffn_ln_body(x_ref, wg_ref, wu_ref, wd_ref, g_ref, b_ref, o_ref, *, alpha):
    x = x_ref[...]
    xb = x.astype(BF16)
    hg = jnp.dot(xb, wg_ref[...], preferred_element_type=F32)
    hu = jnp.dot(xb, wu_ref[...], preferred_element_type=F32)
    a = (hg * jax.nn.sigmoid(hg) * hu).astype(BF16)
    y = jnp.dot(a, wd_ref[...], preferred_element_type=F32)
    o_ref[...] = _layer_norm(alpha * x + 0.5 * y, g_ref[...], b_ref[...])


def _ffn_ln(x, wg, wu, wd, g, b, *, alpha, tm=512):
    m, d = x.shape
    f = wg.shape[1]
    return pl.pallas_call(
        functools.partial(_ffn_ln_body, alpha=alpha),
        grid=(m // tm,),
        in_specs=[
            pl.BlockSpec((tm, d), lambda i: (i, 0)),
            _const_spec((d, f)), _const_spec((d, f)), _const_spec((f, d)),
            _const_spec((1, d)), _const_spec((1, d)),
        ],
        out_specs=pl.BlockSpec((tm, d), lambda i: (i, 0)),
        out_shape=jax.ShapeDtypeStruct((m, d), F32),
        compiler_params=pltpu.CompilerParams(
            dimension_semantics=("parallel",), vmem_limit_bytes=V7X_VMEM_LIMIT_BYTES),
        name="ffn_ln",
    )(x, wg, wu, wd, g, b)


def _in_proj_body(x_ref, wc_ref, wg_ref, wn_ref, hc_ref, hg_ref, hn_ref):
    xb = x_ref[...].astype(BF16)
    hc_ref[...] = jnp.dot(xb, wc_ref[...], preferred_element_type=F32)
    hg_ref[...] = jnp.dot(xb, wg_ref[...], preferred_element_type=F32)
    hn_ref[...] = jnp.dot(xb, wn_ref[...], preferred_element_type=F32)


def _in_proj(x, wc, wg, wn, *, tm=512):
    m, d = x.shape
    widths = (wc.shape[1], wg.shape[1], wn.shape[1])
    return pl.pallas_call(
        _in_proj_body,
        grid=(m // tm,),
        in_specs=[pl.BlockSpec((tm, d), lambda i: (i, 0))] + [_const_spec((d, w)) for w in widths],
        out_specs=[pl.BlockSpec((tm, w), lambda i: (i, 0)) for w in widths],
        out_shape=[jax.ShapeDtypeStruct((m, w), F32) for w in widths],
        compiler_params=pltpu.CompilerParams(
            dimension_semantics=("parallel",), vmem_limit_bytes=V7X_VMEM_LIMIT_BYTES),
        name="in_proj",
    )(x, wc, wg, wn)


def _out_proj_ln_body(x_ref, ya_ref, yb_ref, yc_ref, wa_ref, wb_ref, wc_ref, g_ref, b_ref, o_ref, *, alpha):
    y = jnp.dot(ya_ref[...].astype(BF16), wa_ref[...], preferred_element_type=F32)
    y += jnp.dot(yb_ref[...].astype(BF16), wb_ref[...], preferred_element_type=F32)
    y += jnp.dot(yc_ref[...].astype(BF16), wc_ref[...], preferred_element_type=F32)
    o_ref[...] = _layer_norm(alpha * x_ref[...] + y, g_ref[...], b_ref[...])


def _out_proj_ln(x, ya, yb, yc, wa, wb, wc, g, b, *, alpha, tm=512):
    m, d = x.shape
    row = lambda w: pl.BlockSpec((tm, w), lambda i: (i, 0))
    return pl.pallas_call(
        functools.partial(_out_proj_ln_body, alpha=alpha),
        grid=(m // tm,),
        in_specs=[row(d), row(ya.shape[1]), row(yb.shape[1]), row(yc.shape[1]),
                  _const_spec(wa.shape), _const_spec(wb.shape), _const_spec(wc.shape),
                  _const_spec((1, d)), _const_spec((1, d))],
        out_specs=row(d),
        out_shape=jax.ShapeDtypeStruct((m, d), F32),
        compiler_params=pltpu.CompilerParams(
            dimension_semantics=("parallel",), vmem_limit_bytes=V7X_VMEM_LIMIT_BYTES),
        name="out_proj_ln",
    )(x, ya, yb, yc, wa, wb, wc, g, b)


def _shift_rows(u, s):
    rows = lax.broadcasted_iota(jnp.int32, u.shape, 0)
    return jnp.where(rows >= s, pltpu.roll(u, s, 0), 0.0)


def _causal_dwconv(u, w):
    k = w.shape[0]
    y = u * w[k - 1:k, :]
    for j in range(k - 1):
        y = y + _shift_rows(u, k - 1 - j) * w[j:j + 1, :]
    return y


def _short_conv_body(h_ref, w_ref, o_ref):
    h = h_ref[0]
    c = CONV_WIDTH
    b_gate, c_gate, hh = h[:, 0:c], h[:, c:2 * c], h[:, 2 * c:3 * c]
    o_ref[0] = b_gate * _causal_dwconv(c_gate * hh, w_ref[...])


def _short_conv(hc, w_taps):
    bsz, s, c3 = hc.shape
    return pl.pallas_call(
        _short_conv_body,
        grid=(bsz,),
        in_specs=[pl.BlockSpec((1, s, c3), lambda b: (b, 0, 0)), _const_spec(w_taps.shape)],
        out_specs=pl.BlockSpec((1, s, CONV_WIDTH), lambda b: (b, 0, 0)),
        out_shape=jax.ShapeDtypeStruct((bsz, s, CONV_WIDTH), F32),
        compiler_params=pltpu.CompilerParams(
            dimension_semantics=("parallel",), vmem_limit_bytes=V7X_VMEM_LIMIT_BYTES),
        name="short_conv",
    )(hc, w_taps)


def _dot(a, b):
    return jnp.dot(a, b, preferred_element_type=F32)


def _dot_nt(a, b):
    return lax.dot_general(a, b, (((1,), (1,)), ((), ())), preferred_element_type=F32)


def _dot_tn(a, b):
    return lax.dot_general(a, b, (((0,), (0,)), ((), ())), preferred_element_type=F32)


def _split_bf16(a, terms):
    parts, rest = [], a
    for _ in range(terms):
        p = rest.astype(BF16)
        parts.append(p)
        rest = rest - p.astype(F32)
    return parts


def _dot_exact_lhs(m_bf16, a, terms):
    return sum(_dot(m_bf16, p) for p in _split_bf16(a, terms))


def _dot_exact_rhs(a, m_bf16, terms):
    return sum(_dot(p, m_bf16) for p in _split_bf16(a, terms))


def _dot_hl(a, b):
    a_hi, a_lo = _split_bf16(a, 2)
    b_hi, b_lo = _split_bf16(b, 2)
    return _dot(a_hi, b_hi) + (_dot(a_hi, b_lo) + _dot(a_lo, b_hi))


def _iota2(shape, dim):
    return lax.broadcasted_iota(jnp.int32, shape, dim)


def _unit_lower_inverse(l_strict, eye):
    p = eye - l_strict
    m = _dot_hl(l_strict, l_strict)
    steps = GDN_CHUNK.bit_length() - 2
    for _ in range(steps - 1):
        p = p + _dot_hl(p, m)
        m = _dot_hl(m, m)
    return p + _dot_hl(p, m)


def _softplus(x):
    return jnp.maximum(x, 0.0) + jnp.log1p(jnp.exp(-jnp.abs(x)))


def _gdn_body(q_ref, k_ref, v_ref, z_ref, ab_ref, wq_ref, wk_ref, wv_ref, alog_ref, dtb_ref, nw_ref,
              o_ref,
              qn_s, kn_s, kb_s, vb_s, g_s, u_s, w_s, a_s, qd_s, kd_s, egl_s, o_s):
    hp = pl.program_id(1)
    s = q_ref.shape[1]
    c = GDN_CHUNK
    d = HEAD_DIM
    nc = s // c

    lane = _iota2((1, LANES), 1)
    head_ones = (_iota2((LANES, LANES), 0) >> HEAD_SHIFT == _iota2((LANES, LANES), 1) >> HEAD_SHIFT).astype(BF16)

    def head_sum(t):
        return _dot_exact_rhs(t, head_ones, 2)

    def conv_silu(ref, w_ref):
        y = _causal_dwconv(ref[0], w_ref[...])
        return y * jax.nn.sigmoid(y)

    def l2norm(t):
        return t * lax.rsqrt(head_sum(t * t) + NORM_EPS)

    qn = l2norm(conv_silu(q_ref, wq_ref)) * (d ** -0.5)
    kn = l2norm(conv_silu(k_ref, wk_ref))
    xv = conv_silu(v_ref, wv_ref)

    ab = ab_ref[0]
    lo = lane < d

    def per_head(col):
        c_lo = jnp.where(hp == 0, ab[:, col:col + 1], ab[:, col + 2:col + 3])
        c_hi = jnp.where(hp == 0, ab[:, col + 1:col + 2], ab[:, col + 3:col + 4])
        return jnp.where(lo, c_lo, c_hi)

    a_rep = per_head(0)
    b_rep = per_head(GDN_HEADS)
    beta = jax.nn.sigmoid(b_rep)
    g_s[...] = -jnp.exp(alog_ref[...]) * _softplus(a_rep + dtb_ref[...])
    qn_s[...] = qn
    kn_s[...] = kn
    kb_s[...] = kn * beta
    vb_s[...] = xv * beta

    ri = _iota2((c, c), 0)
    ci = _iota2((c, c), 1)
    tril = ci <= ri
    strict = ci < ri
    eye = (ci == ri).astype(F32)
    tril_b = tril.astype(BF16)
    ones_b = jnp.ones((c, c), BF16)
    ri2 = _iota2((c, LANES), 0)
    ci2 = _iota2((c, LANES), 1) & (d - 1)
    tril2 = ci2 <= ri2
    upper2 = (ri2 <= ci2).astype(F32)

    def chunk_body(n, carry):
        rows = pl.ds(pl.multiple_of(n * c, c), c)
        g = g_s[rows, :]
        gc = _dot_exact_lhs(tril_b, g, 3)
        gct = _dot_exact_lhs(ones_b, g * upper2, 3)
        decay = jnp.where(tril2, jnp.exp(jnp.where(tril2, gc - gct, 0.0)), 0.0)
        egc = jnp.exp(gc)
        glast = gc[c - 1:c, :]
        kscale = jnp.exp(glast - gc)
        egl_s[n] = jnp.exp(glast)
        qn_c, kn_c, kb_c, vb_c = qn_s[rows, :], kn_s[rows, :], kb_s[rows, :], vb_s[rows, :]
        for h in range(2):
            sl = slice(h * d, (h + 1) * d)
            q_h, k_h, kb_h, vb_h = qn_c[:, sl], kn_c[:, sl], kb_c[:, sl], vb_c[:, sl]
            dec_h, egc_h = decay[:, sl], egc[:, sl]
            k_b16 = k_h.astype(BF16)
            lmat = jnp.where(strict, _dot_nt(kb_h.astype(BF16), k_b16) * dec_h, 0.0)
            tinv = _unit_lower_inverse(lmat, eye).astype(BF16)
            u_s[h, rows, :] = _dot(tinv, vb_h.astype(BF16))
            w_s[h, rows, :] = _dot(tinv, (kb_h * egc_h).astype(BF16)).astype(BF16)
            a_s[h, rows, :] = jnp.where(tril, _dot_nt(q_h.astype(BF16), k_b16) * dec_h, 0.0).astype(BF16)
            qd_s[h, rows, :] = (q_h * egc_h).astype(BF16)
            kd_s[h, rows, :] = (k_h * kscale[:, sl]).astype(BF16)
        return carry

    lax.fori_loop(0, nc, chunk_body, 0)

    def scan_body(n, states):
        rows = pl.ds(pl.multiple_of(n * c, c), c)
        egl = egl_s[n]
        new_states = []
        for h in range(2):
            sl = slice(h * d, (h + 1) * d)
            st = states[h]
            st_b = st.astype(BF16)
            v_new = u_s[h, rows, :] - _dot(w_s[h, rows, :], st_b)
            v_b = v_new.astype(BF16)
            o_s[rows, sl] = _dot(qd_s[h, rows, :], st_b) + _dot(a_s[h, rows, :], v_b)
            new_states.append(st * egl[:, sl] + _dot_tn(kd_s[h, rows, :], v_b))
        return tuple(new_states)

    zero = jnp.zeros((d, d), F32)
    lax.fori_loop(0, nc, scan_body, (zero, zero))

    o = o_s[...]
    o = o * lax.rsqrt(head_sum(o * o) * (1.0 / d) + NORM_EPS) * nw_ref[...]
    z = z_ref[0]
    o_ref[0] = o * (z * jax.nn.sigmoid(z))


def _gated_deltanet(hg, conv_taps, alog_rep, dtb_rep, nw_rep):
    bsz, s, _ = hg.shape
    slab = lambda off: pl.BlockSpec((1, s, LANES), lambda b, hp: (b, 0, off + hp))
    taps = lambda off: pl.BlockSpec((GDN_CONV_K, LANES), lambda b, hp: (0, off + hp))
    rep = pl.BlockSpec((1, LANES), lambda b, hp: (0, hp))
    nlane = GDN_WIDTH // LANES
    return pl.pallas_call(
        _gdn_body,
        grid=(bsz, nlane),
        in_specs=[slab(0), slab(nlane), slab(2 * nlane), slab(3 * nlane),
                  pl.BlockSpec((1, s, LANES), lambda b, hp: (b, 0, 4 * nlane)),
                  taps(0), taps(nlane), taps(2 * nlane),
                  rep, rep, _const_spec((1, LANES))],
        out_specs=pl.BlockSpec((1, s, LANES), lambda b, hp: (b, 0, hp)),
        out_shape=jax.ShapeDtypeStruct((bsz, s, GDN_WIDTH), F32),
        scratch_shapes=[pltpu.VMEM((s, LANES), F32)] * 5 + [
            pltpu.VMEM((2, s, HEAD_DIM), F32),
            pltpu.VMEM((2, s, HEAD_DIM), BF16), pltpu.VMEM((2, s, HEAD_DIM), BF16),
            pltpu.VMEM((2, s, HEAD_DIM), BF16), pltpu.VMEM((2, s, HEAD_DIM), BF16),
            pltpu.VMEM((s // GDN_CHUNK, 1, LANES), F32),
            pltpu.VMEM((s, LANES), F32)],
        compiler_params=pltpu.CompilerParams(
            dimension_semantics=("parallel", "parallel"), vmem_limit_bytes=V7X_VMEM_LIMIT_BYTES),
        name="gated_deltanet",
    )(hg, hg, hg, hg, hg, conv_taps, conv_taps, conv_taps, alog_rep, dtb_rep, nw_rep)


NSA_TQ = 128
NSA_TK = 128
NSA_NCMP_PAD = 128


def _softmax_rows(s):
    m = jnp.max(s, axis=-1, keepdims=True)
    e = jnp.exp(s - m)
    return e, jnp.sum(e, axis=-1, keepdims=True)


def _nsa_body(q_ref, gate_ref, cmp_ref, kv_ref, pe_ref, w1_ref, w2_ref, ovl_ref, o_ref,
              kcv_s, ks_s, vs_s, kw_s, vw_s):
    qi = pl.program_id(2)
    tq, tk, d, grp = NSA_TQ, NSA_TK, HEAD_DIM, NSA_GROUP
    s_len = kv_ref.shape[1]
    n_sel = s_len // SEL_BLOCK
    seg = CMP_LEN // CMP_STRIDE
    band = WINDOW + tq

    @pl.when(qi == 0)
    def _():
        kv = kv_ref[0]
        ks_s[...] = kv[:, 0:d].astype(BF16)
        vs_s[...] = kv[:, d:2 * d].astype(BF16)
        kw_s[...] = kv[:, 2 * d:3 * d].astype(BF16)
        vw_s[...] = kv[:, 3 * d:4 * d].astype(BF16)
        pre = [jnp.zeros((NSA_NCMP_PAD, 2 * LANES), F32) for _ in range(seg)]
        for l in range(CMP_STRIDE):
            t_l = cmp_ref[0, pl.ds(l, NSA_NCMP_PAD, stride=CMP_STRIDE), :]
            for h in range(seg):
                lh = l + h * CMP_STRIDE
                pre[h] = pre[h] + _dot((t_l + pe_ref[lh:lh + 1, :]).astype(BF16), w1_ref[lh])
        hid = pre[0] + pltpu.roll(pre[1], NSA_NCMP_PAD - 1, 0)
        hid = hid * jax.nn.sigmoid(hid)
        kcv_s[...] = _dot(hid.astype(BF16), w2_ref[...]).astype(BF16)

    ts = qi * tq
    q = q_ref[0] * (d ** -0.5)
    qs = jnp.concatenate([q[:, g * d:(g + 1) * d] for g in range(grp)], axis=0).astype(BF16)
    t_col = ts + _iota2((tq, 1), 0)

    def mask_rows(ok, sc):
        n = sc.shape[-1]
        return jnp.where(ok[None], sc.reshape(grp, tq, n), NEG).reshape(grp * tq, n)

    kcv = kcv_s[...]
    kc, vc = kcv[:, 0:d], kcv[:, d:2 * d]
    n_idx = _iota2((tq, NSA_NCMP_PAD), 1)
    cmp_ok = (n_idx * CMP_STRIDE + (CMP_LEN - 1) <= t_col) & (n_idx < NSA_NCMP_PAD - 1)
    e, den = _softmax_rows(mask_rows(cmp_ok, _dot_nt(qs, kc)))
    p_cmp = jnp.where((t_col >= CMP_LEN - 1)[None], (e / den).reshape(grp, tq, NSA_NCMP_PAD), 0.0)
    p_cmp = p_cmp.reshape(grp * tq, NSA_NCMP_PAD)
    o_cmp = _dot(p_cmp.astype(BF16), vc)

    p_sum = p_cmp[0:tq]
    for g in range(1, grp):
        p_sum = p_sum + p_cmp[g * tq:(g + 1) * tq]
    imp = _dot_exact_rhs(p_sum, ovl_ref[...], 3)
    j_idx = _iota2((tq, n_sel), 1)
    q_blk = t_col >> SEL_SHIFT
    forced = (j_idx == 0) | (j_idx == q_blk) | (j_idx == q_blk - 1)
    imp = jnp.where(forced, imp + FORCE_BONUS, imp)
    causal_blk = j_idx <= q_blk
    imp = jnp.where(causal_blk, imp, NEG)
    rank = jnp.zeros((tq, n_sel), F32)
    for jp in range(n_sel):
        col = imp[:, jp:jp + 1]
        beats = (col > imp) | ((col == imp) & (j_idx > jp))
        rank = rank + beats.astype(F32)
    sel = ((rank < SEL_TOPN) & causal_blk).astype(BF16)

    def slc_step(kt, carry):
        m_run, l_run, acc = carry
        k0 = pl.multiple_of(kt * tk, tk)
        kpos = k0 + _iota2((1, tk), 1)
        expand = (_iota2((n_sel, tk), 0) == (k0 + _iota2((n_sel, tk), 1)) >> SEL_SHIFT).astype(BF16)
        ok = (_dot(sel, expand) > 0.5) & (kpos <= t_col)
        sc = mask_rows(ok, _dot_nt(qs, ks_s[pl.ds(k0, tk), :]))
        m_new = jnp.maximum(m_run, jnp.max(sc, axis=-1, keepdims=True))
        alpha = jnp.exp(m_run - m_new)
        p = jnp.exp(sc - m_new)
        l_new = alpha * l_run + jnp.sum(p, axis=-1, keepdims=True)
        acc = alpha * acc + _dot(p.astype(BF16), vs_s[pl.ds(k0, tk), :])
        return m_new, l_new, acc

    init = (jnp.full((grp * tq, 1), NEG, F32), jnp.zeros((grp * tq, 1), F32), jnp.zeros((grp * tq, d), F32))
    _, l_slc, acc_slc = lax.fori_loop(0, (ts + tq + tk - 1) // tk, slc_step, init)
    o_slc = acc_slc / l_slc

    w0 = pl.multiple_of(jnp.maximum(ts - WINDOW, 0), tq)
    dist = t_col - (w0 + _iota2((1, band), 1))
    win_ok = (dist >= 0) & (dist < WINDOW)
    e, den = _softmax_rows(mask_rows(win_ok, _dot_nt(qs, kw_s[pl.ds(w0, band), :])))
    o_win = _dot(e.astype(BF16), vw_s[pl.ds(w0, band), :]) / den

    gt = jax.nn.sigmoid(gate_ref[0])
    outs = []
    for g in range(grp):
        rows = slice(g * tq, (g + 1) * tq)
        outs.append(gt[:, g:g + 1] * o_cmp[rows] + gt[:, grp + g:grp + g + 1] * o_slc[rows]
                    + gt[:, 2 * grp + g:2 * grp + g + 1] * o_win[rows])
    o_ref[0] = jnp.concatenate(outs, axis=-1)


def _nsa_attention(hn, pe, w1, w2, overlap):
    bsz, s, _ = hn.shape
    tq, d = NSA_TQ, HEAD_DIM
    qw = NSA_GROUP * d
    kvw = 4 * d
    assert NSA_HEAD_COLS == qw + 2 * LANES + kvw and qw == kvw and s % tq == 0 and s >= WINDOW + tq
    per_w, per_l = NSA_HEAD_COLS // qw, NSA_HEAD_COLS // LANES
    return pl.pallas_call(
        _nsa_body,
        grid=(bsz, NSA_KV_HEADS, s // tq),
        in_specs=[
            pl.BlockSpec((1, tq, qw), lambda b, h, i: (b, i, per_w * h)),
            pl.BlockSpec((1, tq, LANES), lambda b, h, i: (b, i, per_l * h + qw // LANES)),
            pl.BlockSpec((1, s, LANES), lambda b, h, i: (b, 0, per_l * h + qw // LANES + 1)),
            pl.BlockSpec((1, s, kvw), lambda b, h, i: (b, 0, per_w * h + per_w - 1)),
            _const_spec(pe.shape), _const_spec(w1.shape), _const_spec(w2.shape), _const_spec(overlap.shape),
        ],
        out_specs=pl.BlockSpec((1, tq, qw), lambda b, h, i: (b, i, h)),
        out_shape=jax.ShapeDtypeStruct((bsz, s, NSA_WIDTH), F32),
        scratch_shapes=[pltpu.VMEM((NSA_NCMP_PAD, LANES), BF16)] + [pltpu.VMEM((s, d), BF16)] * 4,
        compiler_params=pltpu.CompilerParams(
            dimension_semantics=("parallel", "parallel", "arbitrary"),
            vmem_limit_bytes=V7X_VMEM_LIMIT_BYTES),
        name="nsa_attention",
    )(hn, hn, hn, hn, pe, w1, w2, overlap)


def _prep_w_in(w):
    dm = w.shape[0]
    sizes = (3 * CONV_WIDTH, 3 * GDN_WIDTH, GDN_WIDTH, GDN_HEADS, GDN_HEADS,
             NSA_WIDTH, 6 * NSA_KV_HEADS * HEAD_DIM, 3 * NSA_Q_HEADS)
    parts, start = [], 0
    for n in sizes:
        parts.append(w[:, start:start + n])
        start += n
    wconv, gqkv, gz, ga, gb, nq, nkv, ngate = parts
    pad = lambda n: jnp.zeros((dm, n), w.dtype)
    wgdn = jnp.concatenate([gqkv, gz, ga, gb, pad(LANES - 2 * GDN_HEADS)], axis=1)
    nkv = nkv.reshape(dm, 6, NSA_KV_HEADS, HEAD_DIM)
    ngate = ngate.reshape(dm, NSA_KV_HEADS, NSA_GROUP, 3)
    per_head = []
    for h in range(NSA_KV_HEADS):
        qw = NSA_GROUP * HEAD_DIM
        gates = ngate[:, h].transpose(0, 2, 1).reshape(dm, 3 * NSA_GROUP)
        per_head += [nq[:, h * qw:(h + 1) * qw], gates, pad(LANES - 3 * NSA_GROUP),
                     nkv[:, :, h, :].reshape(dm, 6 * HEAD_DIM)]
    wnsa = jnp.concatenate(per_head, axis=1)
    return wconv.astype(BF16), wgdn.astype(BF16), wnsa.astype(BF16)


def _prep_cmp_weights(pe_k, pe_v, k_w1, k_w2, v_w1, v_w2):
    d, hid = HEAD_DIM, k_w1.shape[1]
    pe = jnp.concatenate([pe_k, pe_v], axis=1)
    k1 = k_w1.reshape(CMP_LEN, d, hid)
    v1 = v_w1.reshape(CMP_LEN, d, hid)
    z1 = jnp.zeros_like(k1)
    w1 = jnp.concatenate([jnp.concatenate([k1, z1], axis=2),
                          jnp.concatenate([z1, v1], axis=2)], axis=1)
    z2 = jnp.zeros_like(k_w2)
    w2 = jnp.concatenate([jnp.concatenate([k_w2, z2], axis=1),
                          jnp.concatenate([z2, v_w2], axis=1)], axis=0)
    return pe, w1.astype(BF16), w2.astype(BF16)


def _overlap_matrix(s):
    n_cmp = (s - CMP_LEN) // CMP_STRIDE + 1
    t = jnp.arange(s)
    starts = jnp.arange(NSA_NCMP_PAD) * CMP_STRIDE
    cmp_tok = (t[None, :] >= starts[:, None]) & (t[None, :] < starts[:, None] + CMP_LEN)
    cmp_tok = cmp_tok & (jnp.arange(NSA_NCMP_PAD) < n_cmp)[:, None]
    sel_tok = (t[None, :] // SEL_BLOCK) == jnp.arange(s // SEL_BLOCK)[:, None]
    return ((cmp_tok.astype(F32) @ sel_tok.astype(F32).T) / CMP_LEN).astype(BF16)


def kernel(x, ffn1_w_gate, ffn1_w_up, ffn1_w_down, ln1_g, ln1_b, w_in, conv_w, gdn_conv_w, gdn_a_log, gdn_dt_bias, gdn_norm_w, cmp_pe_k, cmp_pe_v, cmp_k_w1, cmp_k_w2, cmp_v_w1, cmp_v_w2, w_out, ln2_g, ln2_b, ffn2_w_gate, ffn2_w_up, ffn2_w_down, ln3_g, ln3_b):
    bsz, s, dm = x.shape
    depth = w_in.shape[0]
    alpha = (2 * depth) ** 0.25
    m = bsz * s
    overlap = _overlap_matrix(s)
    row = lambda v: v.reshape(1, -1)
    h = x.reshape(m, dm)
    for l in range(depth):
        h = _ffn_ln(h, ffn1_w_gate[l].astype(BF16), ffn1_w_up[l].astype(BF16), ffn1_w_down[l].astype(BF16),
                    row(ln1_g[l]), row(ln1_b[l]), alpha=alpha)

        wconv, wgdn, wnsa = _prep_w_in(w_in[l])
        hc, hg, hn = _in_proj(h, wconv, wgdn, wnsa)
        y_a = _short_conv(hc.reshape(bsz, s, -1), conv_w[l].T)
        y_b = _gated_deltanet(hg.reshape(bsz, s, -1), gdn_conv_w[l].T,
                              row(jnp.repeat(gdn_a_log[l], HEAD_DIM)), row(jnp.repeat(gdn_dt_bias[l], HEAD_DIM)),
                              row(jnp.tile(gdn_norm_w[l], LANES // HEAD_DIM)))
        pe, w1, w2 = _prep_cmp_weights(cmp_pe_k[l], cmp_pe_v[l], cmp_k_w1[l], cmp_k_w2[l], cmp_v_w1[l], cmp_v_w2[l])
        y_c = _nsa_attention(hn.reshape(bsz, s, -1), pe, w1, w2, overlap)
        wo = w_out[l].astype(BF16)
        h = _out_proj_ln(h, y_a.reshape(m, -1), y_b.reshape(m, -1), y_c.reshape(m, -1),
                         wo[:CONV_WIDTH], wo[CONV_WIDTH:CONV_WIDTH + GDN_WIDTH], wo[CONV_WIDTH + GDN_WIDTH:],
                         row(ln2_g[l]), row(ln2_b[l]), alpha=alpha)

        h = _ffn_ln(h, ffn2_w_gate[l].astype(BF16), ffn2_w_up[l].astype(BF16), ffn2_w_down[l].astype(BF16),
                    row(ln3_g[l]), row(ln3_b[l]), alpha=alpha)
    return h.reshape(bsz, s, dm)
```

```python
import functools

import jax
import jax.numpy as jnp
from jax import lax
from jax.experimental import pallas as pl
from jax.experimental.pallas import tpu as pltpu

F32 = jnp.float32
BF16 = jnp.bfloat16

HEAD_DIM = 64
CONV_WIDTH = 256
CONV_K = 3
GDN_HEADS = 4
GDN_WIDTH = GDN_HEADS * HEAD_DIM
GDN_CONV_K = 4
GDN_CHUNK = 64
NSA_Q_HEADS = 8
NSA_KV_HEADS = 2
NSA_GROUP = NSA_Q_HEADS // NSA_KV_HEADS
NSA_WIDTH = NSA_Q_HEADS * HEAD_DIM
CMP_LEN = 32
CMP_STRIDE = 16
SEL_BLOCK = 64
SEL_TOPN = 8
WINDOW = 512
FORCE_BONUS = 1e3
LN_EPS = 1e-5
NORM_EPS = 1e-6
NEG = -1e30

V7X_VMEM_LIMIT_BYTES = 56 * 1024 * 1024
LANES = 128
HEAD_SHIFT = HEAD_DIM.bit_length() - 1
SEL_SHIFT = SEL_BLOCK.bit_length() - 1

GDN_COLS = 4 * GDN_WIDTH + LANES
NSA_HEAD_COLS = NSA_GROUP * HEAD_DIM + 6 * HEAD_DIM + LANES


def _layer_norm(r, g, b):
    mu = jnp.mean(r, axis=-1, keepdims=True)
    c = r - mu
    var = jnp.mean(c * c, axis=-1, keepdims=True)
    return c * lax.rsqrt(var + LN_EPS) * g + b


def _const_spec(shape):
    return pl.BlockSpec(shape, lambda *_: (0,) * len(shape), pipeline_mode=pl.Buffered(1))


def _ffn_ln_body(x_ref, wg_ref, wu_ref, wd_ref, g_ref, b_ref, o_ref, *, alpha):
    x = x_ref[...]
    xb = x.astype(BF16)
    hg = jnp.dot(xb, wg_ref[...], preferred_element_type=F32)
    hu = jnp.dot(xb, wu_ref[...], preferred_element_type=F32)
    a = (hg * jax.nn.sigmoid(hg) * hu).astype(BF16)
    y = jnp.dot(a, wd_ref[...], preferred_element_type=F32)
    o_ref[...] = _layer_norm(alpha * x + 0.5 * y, g_ref[...], b_ref[...])


def _ffn_ln(x, wg, wu, wd, g, b, *, alpha, tm=512):
    m, d = x.shape
    f = wg.shape[1]
    return pl.pallas_call(
        functools.partial(_ffn_ln_body, alpha=alpha),
        grid=(m // tm,),
        in_specs=[
            pl.BlockSpec((tm, d), lambda i: (i, 0)),
            _const_spec((d, f)), _const_spec((d, f)), _const_spec((f, d)),
            _const_spec((1, d)), _const_spec((1, d)),
        ],
        out_specs=pl.BlockSpec((tm, d), lambda i: (i, 0)),
        out_shape=jax.ShapeDtypeStruct((m, d), F32),
        compiler_params=pltpu.CompilerParams(
            dimension_semantics=("parallel",), vmem_limit_bytes=V7X_VMEM_LIMIT_BYTES),
        name="ffn_ln",
    )(x, wg, wu, wd, g, b)


def _in_proj_body(x_ref, wc_ref, wg_ref, wn_ref, hc_ref, hg_ref, hn_ref):
    xb = x_ref[...].astype(BF16)
    hc_ref[...] = jnp.dot(xb, wc_ref[...], preferred_element_type=F32)
    hg_ref[...] = jnp.dot(xb, wg_ref[...], preferred_element_type=F32)
    hn_ref[...] = jnp.dot(xb, wn_ref[...], preferred_element_type=F32)


def _in_proj(x, wc, wg, wn, *, tm=512):
    m, d = x.shape
    widths = (wc.shape[1], wg.shape[1], wn.shape[1])
    return pl.pallas_call(
        _in_proj_body,
        grid=(m // tm,),
        in_specs=[pl.BlockSpec((tm, d), lambda i: (i, 0))] + [_const_spec((d, w)) for w in widths],
        out_specs=[pl.BlockSpec((tm, w), lambda i: (i, 0)) for w in widths],
        out_shape=[jax.ShapeDtypeStruct((m, w), F32) for w in widths],
        compiler_params=pltpu.CompilerParams(
            dimension_semantics=("parallel",), vmem_limit_bytes=V7X_VMEM_LIMIT_BYTES),
        name="in_proj",
    )(x, wc, wg, wn)


def _out_proj_ln_body(x_ref, ya_ref, yb_ref, yc_ref, wa_ref, wb_ref, wc_ref, g_ref, b_ref, o_ref, *, alpha):
    y = jnp.dot(ya_ref[...].astype(BF16), wa_ref[...], preferred_element_type=F32)
    y += jnp.dot(yb_ref[...].astype(BF16), wb_ref[...], preferred_element_type=F32)
    y += jnp.dot(yc_ref[...].astype(BF16), wc_ref[...], preferred_element_type=F32)
    o_ref[...] = _layer_norm(alpha * x_ref[...] + y, g_ref[...], b_ref[...])


def _out_proj_ln(x, ya, yb, yc, wa, wb, wc, g, b, *, alpha, tm=512):
    m, d = x.shape
    row = lambda w: pl.BlockSpec((tm, w), lambda i: (i, 0))
    return pl.pallas_call(
        functools.partial(_out_proj_ln_body, alpha=alpha),
        grid=(m // tm,),
        in_specs=[row(d), row(ya.shape[1]), row(yb.shape[1]), row(yc.shape[1]),
                  _const_spec(wa.shape), _const_spec(wb.shape), _const_spec(wc.shape),
                  _const_spec((1, d)), _const_spec((1, d))],
        out_specs=row(d),
        out_shape=jax.ShapeDtypeStruct((m, d), F32),
        compiler_params=pltpu.CompilerParams(
            dimension_semantics=("parallel",), vmem_limit_bytes=V7X_VMEM_LIMIT_BYTES),
        name="out_proj_ln",
    )(x, ya, yb, yc, wa, wb, wc, g, b)


def _shift_rows(u, s):
    rows = lax.broadcasted_iota(jnp.int32, u.shape, 0)
    return jnp.where(rows >= s, pltpu.roll(u, s, 0), 0.0)


def _causal_dwconv(u, w):
    k = w.shape[0]
    y = u * w[k - 1:k, :]
    for j in range(k - 1):
        y = y + _shift_rows(u, k - 1 - j) * w[j:j + 1, :]
    return y


def _short_conv_body(h_ref, w_ref, o_ref):
    h = h_ref[0]
    c = CONV_WIDTH
    b_gate, c_gate, hh = h[:, 0:c], h[:, c:2 * c], h[:, 2 * c:3 * c]
    o_ref[0] = b_gate * _causal_dwconv(c_gate * hh, w_ref[...])


def _short_conv(hc, w_taps):
    bsz, s, c3 = hc.shape
    return pl.pallas_call(
        _short_conv_body,
        grid=(bsz,),
        in_specs=[pl.BlockSpec((1, s, c3), lambda b: (b, 0, 0)), _const_spec(w_taps.shape)],
        out_specs=pl.BlockSpec((1, s, CONV_WIDTH), lambda b: (b, 0, 0)),
        out_shape=jax.ShapeDtypeStruct((bsz, s, CONV_WIDTH), F32),
        compiler_params=pltpu.CompilerParams(
            dimension_semantics=("parallel",), vmem_limit_bytes=V7X_VMEM_LIMIT_BYTES),
        name="short_conv",
    )(hc, w_taps)


def _dot(a, b):
    return jnp.dot(a, b, preferred_element_type=F32)


def _dot_nt(a, b):
    return lax.dot_general(a, b, (((1,), (1,)), ((), ())), preferred_element_type=F32)


def _dot_tn(a, b):
    return lax.dot_general(a, b, (((0,), (0,)), ((), ())), preferred_element_type=F32)


def _split_bf16(a, terms):
    parts, rest = [], a
    for _ in range(terms):
        p = rest.astype(BF16)
        parts.append(p)
        rest = rest - p.astype(F32)
    return parts


def _dot_exact_lhs(m_bf16, a, terms):
    return sum(_dot(m_bf16, p) for p in _split_bf16(a, terms))


def _dot_exact_rhs(a, m_bf16, terms):
    return sum(_dot(p, m_bf16) for p in _split_bf16(a, terms))


def _dot_hl(a, b):
    a_hi, a_lo = _split_bf16(a, 2)
    b_hi, b_lo = _split_bf16(b, 2)
    return _dot(a_hi, b_hi) + (_dot(a_hi, b_lo) + _dot(a_lo, b_hi))


def _iota2(shape, dim):
    return lax.broadcasted_iota(jnp.int32, shape, dim)


GDN_INV_BASE = 8
GDN_CHUNKS_PER_STEP = 4


def _block_diag(x, bd_mask):
    return jnp.concatenate([x] * (x.shape[1] // x.shape[0]), axis=0) * bd_mask


def _heads_dot_hl(a, b, bd_mask):
    a_hi, a_lo = _split_bf16(a, 2)
    b_hi, b_lo = _split_bf16(b, 2)
    b_hi = _block_diag(b_hi, bd_mask)
    return _dot(jnp.concatenate([a_hi, a_lo, a_hi], axis=1),
                jnp.concatenate([b_hi, b_hi, _block_diag(b_lo, bd_mask)], axis=0))


def _inverse_masks(c, width):
    ri, ci = _iota2((c, width), 0), _iota2((c, width), 1) & (c - 1)
    base = GDN_INV_BASE.bit_length() - 1
    eye = (ci == ri).astype(F32)
    diag = (ri >> base) == (ci >> base)
    levels = [((ri >> (s + 1)) == (ci >> (s + 1))) & ((ri >> s) != (ci >> s))
              for s in range(base, c.bit_length() - 1)]
    return eye, diag, levels


def _unit_lower_inverse(lmats, masks, bd_mask):
    eye, diag, levels = masks
    c = lmats[0].shape[0]
    hdot = lambda a, b: _heads_dot_hl(a, b, bd_mask)
    l0 = [jnp.where(diag, l, 0.0) for l in lmats]
    p = [eye - x for x in l0]
    m = [hdot(x, x) for x in l0]
    pm = [hdot(jnp.concatenate([pi, mi], axis=0), mi) for pi, mi in zip(p, m)]
    p = [pi + x[0:c] for pi, x in zip(p, pm)]
    inv = [pi + hdot(pi, x[c:2 * c]) for pi, x in zip(p, pm)]
    for level in levels:
        t = [hdot(i, jnp.where(level, l, 0.0)) for i, l in zip(inv, lmats)]
        inv = [i - hdot(ti, i) for i, ti in zip(inv, t)]
    return inv


def _softplus(x):
    return jnp.maximum(x, 0.0) + jnp.log1p(jnp.exp(-jnp.abs(x)))


def _gdn_body(h_ref, taps_ref, alog_ref, dtb_ref, nw_ref, o_ref,
              qn_s, kn_s, kb_s, vb_s, g_s, u_s, w_s, a_s, qd_s, kd_s, egl_s, o_s):
    s = h_ref.shape[1]
    c = GDN_CHUNK
    d = HEAD_DIM
    w = GDN_WIDTH
    nc = s // c

    bd_mask = (_iota2((w, w), 0) >> HEAD_SHIFT == _iota2((w, w), 1) >> HEAD_SHIFT).astype(BF16)

    def head_sum(t):
        return _dot_exact_rhs(t, bd_mask, 2)

    def conv_silu(group):
        cols = slice(group * w, (group + 1) * w)
        y = _causal_dwconv(h_ref[0, :, cols], taps_ref[:, cols])
        return y * jax.nn.sigmoid(y)

    def l2norm(t):
        return t * lax.rsqrt(head_sum(t * t) + NORM_EPS)

    qn_s[...] = l2norm(conv_silu(0)) * (d ** -0.5)
    kn = l2norm(conv_silu(1))
    kn_s[...] = kn
    ab = h_ref[0, :, 4 * w:4 * w + LANES]
    src = _iota2((LANES, w), 0)
    head = _iota2((LANES, w), 1) >> HEAD_SHIFT
    a_rep = _dot_exact_rhs(ab, (src == head).astype(BF16), 3)
    b_rep = _dot_exact_rhs(ab, (src == head + GDN_HEADS).astype(BF16), 3)
    beta = jax.nn.sigmoid(b_rep)
    g_s[...] = -jnp.exp(alog_ref[...]) * _softplus(a_rep + dtb_ref[...])
    kb_s[...] = kn * beta
    vb_s[...] = conv_silu(2) * beta

    ri = _iota2((c, w), 0)
    ci = _iota2((c, w), 1) & (c - 1)
    tril = ci <= ri
    strict = ci < ri
    upper = (ri <= ci).astype(F32)
    inv_masks = _inverse_masks(c, w)
    tril_b = (_iota2((c, c), 1) <= _iota2((c, c), 0)).astype(BF16)
    ones_b = jnp.ones((c, c), BF16)

    def chunk_prep(it, carry):
        ns = [it * GDN_CHUNKS_PER_STEP + i for i in range(GDN_CHUNKS_PER_STEP)]
        rows = [pl.ds(pl.multiple_of(n * c, c), c) for n in ns]
        g = [g_s[r, :] for r in rows]
        gc = [_dot_exact_lhs(tril_b, x, 3) for x in g]
        gct = [_dot_exact_lhs(ones_b, x * upper, 3) for x in g]
        kn_c = [kn_s[r, :] for r in rows]
        kb_c = [kb_s[r, :] for r in rows]
        qn_c = [qn_s[r, :] for r in rows]
        kq = [_dot_nt(jnp.concatenate([kb, q], axis=0).astype(BF16), _block_diag(k.astype(BF16), bd_mask))
              for kb, q, k in zip(kb_c, qn_c, kn_c)]
        decay = [jnp.where(tril, jnp.exp(jnp.where(tril, a - b, 0.0)), 0.0) for a, b in zip(gc, gct)]
        lmat = [jnp.where(strict, x[0:c] * dk, 0.0) for x, dk in zip(kq, decay)]
        tinv = _unit_lower_inverse(lmat, inv_masks, bd_mask)
        egc = [jnp.exp(x) for x in gc]
        rhs = [jnp.concatenate([_block_diag(vb_s[r, :].astype(BF16), bd_mask),
                                _block_diag((kb * e).astype(BF16), bd_mask)], axis=1)
               for r, kb, e in zip(rows, kb_c, egc)]
        uw = [_dot(t.astype(BF16), x) for t, x in zip(tinv, rhs)]
        for i, (n, r) in enumerate(zip(ns, rows)):
            glast = gc[i][c - 1:c, :]
            egl_s[n] = jnp.exp(glast)
            u_s[r, :] = uw[i][:, 0:w]
            w_s[r, :] = uw[i][:, w:2 * w].astype(BF16)
            a_s[r, :] = jnp.where(tril, kq[i][c:2 * c] * decay[i], 0.0).astype(BF16)
            qd_s[r, :] = (qn_c[i] * egc[i]).astype(BF16)
            kd_s[r, :] = (kn_c[i] * jnp.exp(glast - gc[i])).astype(BF16)
        return carry

    lax.fori_loop(0, nc // GDN_CHUNKS_PER_STEP, chunk_prep, 0)

    lane_head = _iota2((d, w), 1) >> HEAD_SHIFT

    def scan_body(n, st):
        rows = pl.ds(pl.multiple_of(n * c, c), c)
        st_bd = _block_diag(st.astype(BF16), bd_mask)
        ws_qs = _dot(jnp.concatenate([w_s[rows, :], qd_s[rows, :]], axis=0), st_bd)
        v_new = u_s[rows, :] - ws_qs[0:c]
        v_b = v_new.astype(BF16)
        o_s[rows, :] = ws_qs[c:2 * c] + _dot(a_s[rows, :], _block_diag(v_b, bd_mask))
        kv = _dot_tn(kd_s[rows, :], v_b)
        upd = jnp.zeros((d, w), F32)
        for h in range(GDN_HEADS):
            upd = upd + jnp.where(lane_head == h, kv[h * d:(h + 1) * d, :], 0.0)
        return st * egl_s[n] + upd

    lax.fori_loop(0, nc, scan_body, jnp.zeros((d, w), F32))

    o = o_s[...]
    o = o * lax.rsqrt(head_sum(o * o) * (1.0 / d) + NORM_EPS) * nw_ref[...]
    z = h_ref[0, :, 3 * w:4 * w]
    o_ref[0] = o * (z * jax.nn.sigmoid(z))


def _gated_deltanet(hg, conv_taps, alog_rep, dtb_rep, nw_rep):
    bsz, s, cols = hg.shape
    w = GDN_WIDTH
    assert GDN_CHUNK == HEAD_DIM and cols == GDN_COLS and s % GDN_CHUNK == 0
    return pl.pallas_call(
        _gdn_body,
        grid=(bsz,),
        in_specs=[pl.BlockSpec((1, s, cols), lambda b: (b, 0, 0), pipeline_mode=pl.Buffered(1)),
                  _const_spec(conv_taps.shape),
                  _const_spec((1, w)), _const_spec((1, w)), _const_spec((1, w))],
        out_specs=pl.BlockSpec((1, s, w), lambda b: (b, 0, 0)),
        out_shape=jax.ShapeDtypeStruct((bsz, s, w), F32),
        scratch_shapes=[pltpu.VMEM((s, w), F32)] * 6 + [pltpu.VMEM((s, w), BF16)] * 4 + [
            pltpu.VMEM((s // GDN_CHUNK, 1, w), F32), pltpu.VMEM((s, w), F32)],
        compiler_params=pltpu.CompilerParams(
            dimension_semantics=("parallel",), vmem_limit_bytes=V7X_VMEM_LIMIT_BYTES),
        name="gated_deltanet",
    )(hg, conv_taps, alog_rep, dtb_rep, nw_rep)


NSA_TQ = 256
NSA_SLC_STEP = 512
NSA_NCMP_PAD = 128


def _softmax_cols(s):
    m = jnp.max(s, axis=0, keepdims=True)
    e = jnp.exp(s - m)
    return e, jnp.sum(e, axis=0, keepdims=True)


def _nsa_body(q_ref, gate_ref, cmp_ref, kv_ref, pe_ref, w1_ref, w2_ref, ovlt_ref, o_ref,
              kc_s, vct_s, ks_s, kw_s, vst_s, vwt_s, sel_s, oslc_s):
    qi = pl.program_id(2)
    tq, d, grp = NSA_TQ, HEAD_DIM, NSA_GROUP
    s_len = kv_ref.shape[1]
    n_sel = s_len // SEL_BLOCK
    seg = CMP_LEN // CMP_STRIDE
    band = WINDOW + tq
    wide = grp * tq

    @pl.when(qi == 0)
    def _():
        def relayout(i, carry):
            r0 = pl.multiple_of(i * LANES, LANES)
            blk = kv_ref[0, pl.ds(r0, LANES), :]
            ks_s[pl.ds(r0, LANES), :] = blk[:, 0:d].astype(BF16)
            kw_s[pl.ds(r0, LANES), :] = blk[:, 2 * d:3 * d].astype(BF16)
            blk_t = blk.T
            vst_s[:, pl.ds(r0, LANES)] = blk_t[d:2 * d, :].astype(BF16)
            vwt_s[:, pl.ds(r0, LANES)] = blk_t[3 * d:4 * d, :].astype(BF16)
            return carry

        lax.fori_loop(0, s_len // LANES, relayout, 0)
        pre = [jnp.zeros((NSA_NCMP_PAD, 2 * LANES), F32) for _ in range(seg)]
        for l in range(CMP_STRIDE):
            t_l = cmp_ref[0, pl.ds(l, NSA_NCMP_PAD, stride=CMP_STRIDE), :]
            for h in range(seg):
                lh = l + h * CMP_STRIDE
                pre[h] = pre[h] + _dot((t_l + pe_ref[lh:lh + 1, :]).astype(BF16), w1_ref[lh])
        hid = pre[0] + pltpu.roll(pre[1], NSA_NCMP_PAD - 1, 0)
        hid = hid * jax.nn.sigmoid(hid)
        kcv = _dot(hid.astype(BF16), w2_ref[...])
        kc_s[...] = kcv[:, 0:d].astype(BF16)
        vct_s[...] = kcv.T[d:2 * d, :].astype(BF16)

    ts = qi * tq
    q = q_ref[0] * (d ** -0.5)
    qs = jnp.concatenate([q[:, g * d:(g + 1) * d] for g in range(grp)], axis=0).astype(BF16)
    t_row = ts + (_iota2((1, wide), 1) & (tq - 1))
    t_row1 = t_row[:, 0:tq]

    n_col = _iota2((NSA_NCMP_PAD, 1), 0)
    cmp_ok = (n_col * CMP_STRIDE + (CMP_LEN - 1) <= t_row) & (n_col < NSA_NCMP_PAD - 1)
    e, den = _softmax_cols(jnp.where(cmp_ok, _dot_nt(kc_s[...], qs), NEG))
    p_cmp = jnp.where(t_row >= CMP_LEN - 1, e / den, 0.0)
    o_cmp = _dot(vct_s[...], p_cmp.astype(BF16))

    p_sum = p_cmp[:, 0:tq]
    for g in range(1, grp):
        p_sum = p_sum + p_cmp[:, g * tq:(g + 1) * tq]
    imp = _dot_exact_lhs(ovlt_ref[...], p_sum, 3)
    j_idx = _iota2((n_sel, tq), 0)
    q_blk = t_row1 >> SEL_SHIFT
    forced = (j_idx == 0) | (j_idx == q_blk) | (j_idx == q_blk - 1)
    imp = jnp.where(forced, imp + FORCE_BONUS, imp)
    causal_blk = j_idx <= q_blk
    imp = jnp.where(causal_blk, imp, NEG)
    rank = jnp.zeros((n_sel, tq), F32)
    for jp in range(n_sel):
        row = imp[jp:jp + 1, :]
        beats = (row > imp) | ((row == imp) & (j_idx > jp))
        rank = rank + beats.astype(F32)
    selected = (rank < SEL_TOPN) & causal_blk

    blk0 = ts >> SEL_SHIFT
    past_bias = jnp.where(selected & (j_idx < blk0), 0.0, NEG)
    sel_s[...] = jnp.where(selected, 0.0, NEG)

    def block_bias(rows):
        per_q = jnp.concatenate([jnp.broadcast_to(r, (SEL_BLOCK, tq)) for r in rows], axis=0)
        return jnp.concatenate([per_q] * grp, axis=1)

    diag_rows = [sel_s[pl.ds(blk0 + i, 1), :] for i in range(tq // SEL_BLOCK)]
    kpos_diag = ts + _iota2((tq, 1), 0)
    sc_diag = jnp.where(kpos_diag <= t_row,
                        _dot_nt(ks_s[pl.ds(ts, tq), :], qs) + block_bias(diag_rows), NEG)

    def slc_attend(nk):
        if nk == 0:
            e, den = _softmax_cols(sc_diag)
            return _dot(vst_s[:, pl.ds(ts, tq)], e.astype(BF16)) / den
        past = [past_bias[j:j + 1, :] for j in range(nk // SEL_BLOCK)]
        sc = jnp.concatenate([_dot_nt(ks_s[0:nk, :], qs) + block_bias(past), sc_diag], axis=0)
        e, den = _softmax_cols(sc)
        e = e.astype(BF16)
        return (_dot(vst_s[:, 0:nk], e[0:nk]) + _dot(vst_s[:, pl.ds(ts, tq)], e[nk:nk + tq])) / den

    n_var = (s_len - tq + NSA_SLC_STEP - 1) // NSA_SLC_STEP + 1
    variant = (ts + NSA_SLC_STEP - 1) // NSA_SLC_STEP
    for v in range(n_var):
        @pl.when(variant == v)
        def _(v=v):
            oslc_s[...] = slc_attend(min(v * NSA_SLC_STEP, s_len))
    o_slc = oslc_s[...]

    w0 = pl.multiple_of(jnp.maximum(ts - WINDOW, 0), tq)
    dist = t_row - (w0 + _iota2((band, 1), 0))
    win_ok = (dist >= 0) & (dist < WINDOW)
    e, den = _softmax_cols(jnp.where(win_ok, _dot_nt(kw_s[pl.ds(w0, band), :], qs), NEG))
    o_win = _dot(vwt_s[:, pl.ds(w0, band)], e.astype(BF16)) / den

    gt = jax.nn.sigmoid(gate_ref[0]).T
    outs = []
    for g in range(grp):
        cols = slice(g * tq, (g + 1) * tq)
        outs.append(gt[g:g + 1, :] * o_cmp[:, cols] + gt[grp + g:grp + g + 1, :] * o_slc[:, cols]
                    + gt[2 * grp + g:2 * grp + g + 1, :] * o_win[:, cols])
    o_ref[0] = jnp.concatenate(outs, axis=0).T


def _nsa_attention(hn, pe, w1, w2, overlap):
    bsz, s, _ = hn.shape
    tq, d = NSA_TQ, HEAD_DIM
    qw = NSA_GROUP * d
    kvw = 4 * d
    assert NSA_HEAD_COLS == qw + 2 * LANES + kvw and qw == kvw and s % tq == 0 and s >= WINDOW + tq
    per_w, per_l = NSA_HEAD_COLS // qw, NSA_HEAD_COLS // LANES
    return pl.pallas_call(
        _nsa_body,
        grid=(bsz, NSA_KV_HEADS, s // tq),
        in_specs=[
            pl.BlockSpec((1, tq, qw), lambda b, h, i: (b, i, per_w * h)),
            pl.BlockSpec((1, tq, LANES), lambda b, h, i: (b, i, per_l * h + qw // LANES)),
            pl.BlockSpec((1, s, LANES), lambda b, h, i: (b, 0, per_l * h + qw // LANES + 1)),
            pl.BlockSpec((1, s, kvw), lambda b, h, i: (b, 0, per_w * h + per_w - 1)),
            _const_spec(pe.shape), _const_spec(w1.shape), _const_spec(w2.shape), _const_spec(overlap.shape),
        ],
        out_specs=pl.BlockSpec((1, tq, qw), lambda b, h, i: (b, i, h)),
        out_shape=jax.ShapeDtypeStruct((bsz, s, NSA_WIDTH), F32),
        scratch_shapes=[pltpu.VMEM((NSA_NCMP_PAD, d), BF16), pltpu.VMEM((d, NSA_NCMP_PAD), BF16),
                        pltpu.VMEM((s, d), BF16), pltpu.VMEM((s, d), BF16),
                        pltpu.VMEM((d, s), BF16), pltpu.VMEM((d, s), BF16),
                        pltpu.VMEM((s // SEL_BLOCK, tq), F32), pltpu.VMEM((d, NSA_GROUP * tq), F32)],
        compiler_params=pltpu.CompilerParams(
            dimension_semantics=("parallel", "parallel", "arbitrary"),
            vmem_limit_bytes=V7X_VMEM_LIMIT_BYTES),
        name="nsa_attention",
    )(hn, hn, hn, hn, pe, w1, w2, overlap)


def _prep_w_in(w):
    dm = w.shape[0]
    sizes = (3 * CONV_WIDTH, 3 * GDN_WIDTH, GDN_WIDTH, GDN_HEADS, GDN_HEADS,
             NSA_WIDTH, 6 * NSA_KV_HEADS * HEAD_DIM, 3 * NSA_Q_HEADS)
    parts, start = [], 0
    for n in sizes:
        parts.append(w[:, start:start + n])
        start += n
    wconv, gqkv, gz, ga, gb, nq, nkv, ngate = parts
    pad = lambda n: jnp.zeros((dm, n), w.dtype)
    wgdn = jnp.concatenate([gqkv, gz, ga, gb, pad(LANES - 2 * GDN_HEADS)], axis=1)
    nkv = nkv.reshape(dm, 6, NSA_KV_HEADS, HEAD_DIM)
    ngate = ngate.reshape(dm, NSA_KV_HEADS, NSA_GROUP, 3)
    per_head = []
    for h in range(NSA_KV_HEADS):
        qw = NSA_GROUP * HEAD_DIM
        gates = ngate[:, h].transpose(0, 2, 1).reshape(dm, 3 * NSA_GROUP)
        per_head += [nq[:, h * qw:(h + 1) * qw], gates, pad(LANES - 3 * NSA_GROUP),
                     nkv[:, :, h, :].reshape(dm, 6 * HEAD_DIM)]
    wnsa = jnp.concatenate(per_head, axis=1)
    return wconv.astype(BF16), wgdn.astype(BF16), wnsa.astype(BF16)


def _prep_cmp_weights(pe_k, pe_v, k_w1, k_w2, v_w1, v_w2):
    d, hid = HEAD_DIM, k_w1.shape[1]
    pe = jnp.concatenate([pe_k, pe_v], axis=1)
    k1 = k_w1.reshape(CMP_LEN, d, hid)
    v1 = v_w1.reshape(CMP_LEN, d, hid)
    z1 = jnp.zeros_like(k1)
    w1 = jnp.concatenate([jnp.concatenate([k1, z1], axis=2),
                          jnp.concatenate([z1, v1], axis=2)], axis=1)
    z2 = jnp.zeros_like(k_w2)
    w2 = jnp.concatenate([jnp.concatenate([k_w2, z2], axis=1),
                          jnp.concatenate([z2, v_w2], axis=1)], axis=0)
    return pe, w1.astype(BF16), w2.astype(BF16)


def _overlap_matrix(s):
    n_cmp = (s - CMP_LEN) // CMP_STRIDE + 1
    t = jnp.arange(s)
    starts = jnp.arange(NSA_NCMP_PAD) * CMP_STRIDE
    cmp_tok = (t[None, :] >= starts[:, None]) & (t[None, :] < starts[:, None] + CMP_LEN)
    cmp_tok = cmp_tok & (jnp.arange(NSA_NCMP_PAD) < n_cmp)[:, None]
    sel_tok = (t[None, :] // SEL_BLOCK) == jnp.arange(s // SEL_BLOCK)[:, None]
    return ((sel_tok.astype(F32) @ cmp_tok.astype(F32).T) / CMP_LEN).astype(BF16)


def kernel(x, ffn1_w_gate, ffn1_w_up, ffn1_w_down, ln1_g, ln1_b, w_in, conv_w, gdn_conv_w, gdn_a_log, gdn_dt_bias, gdn_norm_w, cmp_pe_k, cmp_pe_v, cmp_k_w1, cmp_k_w2, cmp_v_w1, cmp_v_w2, w_out, ln2_g, ln2_b, ffn2_w_gate, ffn2_w_up, ffn2_w_down, ln3_g, ln3_b):
    bsz, s, dm = x.shape
    depth = w_in.shape[0]
    alpha = (2 * depth) ** 0.25
    m = bsz * s
    overlap = _overlap_matrix(s)
    row = lambda v: v.reshape(1, -1)
    h = x.reshape(m, dm)
    for l in range(depth):
        h = _ffn_ln(h, ffn1_w_gate[l].astype(BF16), ffn1_w_up[l].astype(BF16), ffn1_w_down[l].astype(BF16),
                    row(ln1_g[l]), row(ln1_b[l]), alpha=alpha)

        wconv, wgdn, wnsa = _prep_w_in(w_in[l])
        hc, hg, hn = _in_proj(h, wconv, wgdn, wnsa)
        y_a = _short_conv(hc.reshape(bsz, s, -1), conv_w[l].T)
        y_b = _gated_deltanet(hg.reshape(bsz, s, -1), gdn_conv_w[l].T,
                              row(jnp.repeat(gdn_a_log[l], HEAD_DIM)), row(jnp.repeat(gdn_dt_bias[l], HEAD_DIM)),
                              row(jnp.tile(gdn_norm_w[l], GDN_HEADS)))
        pe, w1, w2 = _prep_cmp_weights(cmp_pe_k[l], cmp_pe_v[l], cmp_k_w1[l], cmp_k_w2[l], cmp_v_w1[l], cmp_v_w2[l])
        y_c = _nsa_attention(hn.reshape(bsz, s, -1), pe, w1, w2, overlap)
        wo = w_out[l].astype(BF16)
        h = _out_proj_ln(h, y_a.reshape(m, -1), y_b.reshape(m, -1), y_c.reshape(m, -1),
                         wo[:CONV_WIDTH], wo[CONV_WIDTH:CONV_WIDTH + GDN_WIDTH], wo[CONV_WIDTH + GDN_WIDTH:],
                         row(ln2_g[l]), row(ln2_b[l]), alpha=alpha)

        h = _ffn_ln(h, ffn2_w_gate[l].astype(BF16), ffn2_w_up[l].astype(BF16), ffn2_w_down[l].astype(BF16),
                    row(ln3_g[l]), row(ln3_b[l]), alpha=alpha)
    return h.reshape(bsz, s, dm)
```

```python
import functools

import jax
import jax.numpy as jnp
from jax import lax
from jax.experimental import pallas as pl
from jax.experimental.pallas import tpu as pltpu

F32 = jnp.float32
BF16 = jnp.bfloat16

HEAD_DIM = 64
CONV_WIDTH = 256
CONV_K = 3
GDN_HEADS = 4
GDN_WIDTH = GDN_HEADS * HEAD_DIM
GDN_CONV_K = 4
GDN_CHUNK = 64
NSA_Q_HEADS = 8
NSA_KV_HEADS = 2
NSA_GROUP = NSA_Q_HEADS // NSA_KV_HEADS
NSA_WIDTH = NSA_Q_HEADS * HEAD_DIM
CMP_LEN = 32
CMP_STRIDE = 16
SEL_BLOCK = 64
SEL_TOPN = 8
WINDOW = 512
FORCE_BONUS = 1e3
LN_EPS = 1e-5
NORM_EPS = 1e-6
NEG = -1e30

V7X_VMEM_LIMIT_BYTES = 56 * 1024 * 1024
LANES = 128
HEAD_SHIFT = HEAD_DIM.bit_length() - 1
SEL_SHIFT = SEL_BLOCK.bit_length() - 1

GDN_COLS = 4 * GDN_WIDTH + LANES
NSA_HEAD_COLS = NSA_GROUP * HEAD_DIM + 6 * HEAD_DIM + LANES


def _layer_norm(r, g, b):
    mu = jnp.mean(r, axis=-1, keepdims=True)
    c = r - mu
    var = jnp.mean(c * c, axis=-1, keepdims=True)
    return c * lax.rsqrt(var + LN_EPS) * g + b


def _const_spec(shape):
    return pl.BlockSpec(shape, lambda *_: (0,) * len(shape), pipeline_mode=pl.Buffered(1))


def _ffn_ln_body(x_ref, wg_ref, wu_ref, wd_ref, g_ref, b_ref, o_ref, *, alpha):
    x = x_ref[...]
    xb = x.astype(BF16)
    hg = jnp.dot(xb, wg_ref[...], preferred_element_type=F32)
    hu = jnp.dot(xb, wu_ref[...], preferred_element_type=F32)
    a = (hg * jax.nn.sigmoid(hg) * hu).astype(BF16)
    y = jnp.dot(a, wd_ref[...], preferred_element_type=F32)
    o_ref[...] = _layer_norm(alpha * x + 0.5 * y, g_ref[...], b_ref[...])


def _ffn_ln(x, wg, wu, wd, g, b, *, alpha, tm=512):
    m, d = x.shape
    f = wg.shape[1]
    return pl.pallas_call(
        functools.partial(_ffn_ln_body, alpha=alpha),
        grid=(m // tm,),
        in_specs=[
            pl.BlockSpec((tm, d), lambda i: (i, 0)),
            _const_spec((d, f)), _const_spec((d, f)), _const_spec((f, d)),
            _const_spec((1, d)), _const_spec((1, d)),
        ],
        out_specs=pl.BlockSpec((tm, d), lambda i: (i, 0)),
        out_shape=jax.ShapeDtypeStruct((m, d), F32),
        compiler_params=pltpu.CompilerParams(
            dimension_semantics=("parallel",), vmem_limit_bytes=V7X_VMEM_LIMIT_BYTES),
        name="ffn_ln",
    )(x, wg, wu, wd, g, b)


def _in_proj_body(x_ref, wc_ref, wg_ref, wn_ref, hc_ref, hg_ref, hn_ref):
    xb = x_ref[...].astype(BF16)
    hc_ref[...] = jnp.dot(xb, wc_ref[...], preferred_element_type=F32)
    hg_ref[...] = jnp.dot(xb, wg_ref[...], preferred_element_type=F32)
    hn_ref[...] = jnp.dot(xb, wn_ref[...], preferred_element_type=F32)


def _in_proj(x, wc, wg, wn, *, tm=512):
    m, d = x.shape
    widths = (wc.shape[1], wg.shape[1], wn.shape[1])
    return pl.pallas_call(
        _in_proj_body,
        grid=(m // tm,),
        in_specs=[pl.BlockSpec((tm, d), lambda i: (i, 0))] + [_const_spec((d, w)) for w in widths],
        out_specs=[pl.BlockSpec((tm, w), lambda i: (i, 0)) for w in widths],
        out_shape=[jax.ShapeDtypeStruct((m, w), F32) for w in widths],
        compiler_params=pltpu.CompilerParams(
            dimension_semantics=("parallel",), vmem_limit_bytes=V7X_VMEM_LIMIT_BYTES),
        name="in_proj",
    )(x, wc, wg, wn)


def _out_proj_ln_body(x_ref, ya_ref, yb_ref, yc_ref, wa_ref, wb_ref, wc_ref, g_ref, b_ref, o_ref, *, alpha):
    y = jnp.dot(ya_ref[...].astype(BF16), wa_ref[...], preferred_element_type=F32)
    y += jnp.dot(yb_ref[...].astype(BF16), wb_ref[...], preferred_element_type=F32)
    y += jnp.dot(yc_ref[...].astype(BF16), wc_ref[...], preferred_element_type=F32)
    o_ref[...] = _layer_norm(alpha * x_ref[...] + y, g_ref[...], b_ref[...])


def _out_proj_ln(x, ya, yb, yc, wa, wb, wc, g, b, *, alpha, tm=512):
    m, d = x.shape
    row = lambda w: pl.BlockSpec((tm, w), lambda i: (i, 0))
    return pl.pallas_call(
        functools.partial(_out_proj_ln_body, alpha=alpha),
        grid=(m // tm,),
        in_specs=[row(d), row(ya.shape[1]), row(yb.shape[1]), row(yc.shape[1]),
                  _const_spec(wa.shape), _const_spec(wb.shape), _const_spec(wc.shape),
                  _const_spec((1, d)), _const_spec((1, d))],
        out_specs=row(d),
        out_shape=jax.ShapeDtypeStruct((m, d), F32),
        compiler_params=pltpu.CompilerParams(
            dimension_semantics=("parallel",), vmem_limit_bytes=V7X_VMEM_LIMIT_BYTES),
        name="out_proj_ln",
    )(x, ya, yb, yc, wa, wb, wc, g, b)


def _shift_rows(u, s):
    rows = lax.broadcasted_iota(jnp.int32, u.shape, 0)
    return jnp.where(rows >= s, pltpu.roll(u, s, 0), 0.0)


def _causal_dwconv(u, w):
    k = w.shape[0]
    y = u * w[k - 1:k, :]
    for j in range(k - 1):
        y = y + _shift_rows(u, k - 1 - j) * w[j:j + 1, :]
    return y


def _short_conv_body(h_ref, w_ref, o_ref):
    h = h_ref[0]
    c = CONV_WIDTH
    b_gate, c_gate, hh = h[:, 0:c], h[:, c:2 * c], h[:, 2 * c:3 * c]
    o_ref[0] = b_gate * _causal_dwconv(c_gate * hh, w_ref[...])


def _short_conv(hc, w_taps):
    bsz, s, c3 = hc.shape
    return pl.pallas_call(
        _short_conv_body,
        grid=(bsz,),
        in_specs=[pl.BlockSpec((1, s, c3), lambda b: (b, 0, 0)), _const_spec(w_taps.shape)],
        out_specs=pl.BlockSpec((1, s, CONV_WIDTH), lambda b: (b, 0, 0)),
        out_shape=jax.ShapeDtypeStruct((bsz, s, CONV_WIDTH), F32),
        compiler_params=pltpu.CompilerParams(
            dimension_semantics=("parallel",), vmem_limit_bytes=V7X_VMEM_LIMIT_BYTES),
        name="short_conv",
    )(hc, w_taps)


def _dot(a, b):
    return jnp.dot(a, b, preferred_element_type=F32)


def _dot_nt(a, b):
    return lax.dot_general(a, b, (((1,), (1,)), ((), ())), preferred_element_type=F32)


def _dot_tn(a, b):
    return lax.dot_general(a, b, (((0,), (0,)), ((), ())), preferred_element_type=F32)


def _split_bf16(a, terms):
    parts, rest = [], a
    for _ in range(terms):
        p = rest.astype(BF16)
        parts.append(p)
        rest = rest - p.astype(F32)
    return parts


def _dot_exact_lhs(m_bf16, a, terms):
    return sum(_dot(m_bf16, p) for p in _split_bf16(a, terms))


def _dot_exact_rhs(a, m_bf16, terms):
    return sum(_dot(p, m_bf16) for p in _split_bf16(a, terms))


def _dot_hl(a, b):
    a_hi, a_lo = _split_bf16(a, 2)
    b_hi, b_lo = _split_bf16(b, 2)
    return _dot(a_hi, b_hi) + (_dot(a_hi, b_lo) + _dot(a_lo, b_hi))


def _iota2(shape, dim):
    return lax.broadcasted_iota(jnp.int32, shape, dim)


GDN_INV_BASE = 8
GDN_CHUNKS_PER_STEP = 4


def _block_diag(x, bd_mask):
    return jnp.concatenate([x] * (x.shape[1] // x.shape[0]), axis=0) * bd_mask


def _heads_dot_hl(a, b, bd_mask):
    a_hi, a_lo = _split_bf16(a, 2)
    b_hi, b_lo = _split_bf16(b, 2)
    b_hi = _block_diag(b_hi, bd_mask)
    return _dot(jnp.concatenate([a_hi, a_lo, a_hi], axis=1),
                jnp.concatenate([b_hi, b_hi, _block_diag(b_lo, bd_mask)], axis=0))


def _inverse_masks(c, width):
    ri, ci = _iota2((c, width), 0), _iota2((c, width), 1) & (c - 1)
    base = GDN_INV_BASE.bit_length() - 1
    eye = (ci == ri).astype(F32)
    diag = (ri >> base) == (ci >> base)
    levels = [((ri >> (s + 1)) == (ci >> (s + 1))) & ((ri >> s) != (ci >> s))
              for s in range(base, c.bit_length() - 1)]
    return eye, diag, levels


def _unit_lower_inverse(lmats, masks, bd_mask):
    eye, diag, levels = masks
    c = lmats[0].shape[0]
    hdot = lambda a, b: _heads_dot_hl(a, b, bd_mask)
    l0 = [jnp.where(diag, l, 0.0) for l in lmats]
    p = [eye - x for x in l0]
    m = [hdot(x, x) for x in l0]
    yield
    pm = [hdot(jnp.concatenate([pi, mi], axis=0), mi) for pi, mi in zip(p, m)]
    yield
    p = [pi + x[0:c] for pi, x in zip(p, pm)]
    inv = [pi + hdot(pi, x[c:2 * c]) for pi, x in zip(p, pm)]
    yield
    for level in levels:
        t = [hdot(i, jnp.where(level, l, 0.0)) for i, l in zip(inv, lmats)]
        yield
        inv = [i - hdot(ti, i) for i, ti in zip(inv, t)]
        yield
    return inv


def _interleave(*stage_generators):
    live = list(stage_generators)
    while live:
        for gen in list(live):
            try:
                next(gen)
            except StopIteration:
                live.remove(gen)


def _softplus(x):
    return jnp.maximum(x, 0.0) + jnp.log1p(jnp.exp(-jnp.abs(x)))


def _gdn_body(h_ref, taps_ref, alog_ref, dtb_ref, nw_ref, o_ref,
              qn_s, kn_s, kb_s, vb_s, g_s, u_s, w_s, a_s, qd_s, kd_s, egl_s, o_s):
    s = h_ref.shape[1]
    c = GDN_CHUNK
    d = HEAD_DIM
    w = GDN_WIDTH
    nc = s // c

    bd_mask = (_iota2((w, w), 0) >> HEAD_SHIFT == _iota2((w, w), 1) >> HEAD_SHIFT).astype(BF16)

    def head_sum(t):
        return _dot_exact_rhs(t, bd_mask, 2)

    def conv_silu(group):
        cols = slice(group * w, (group + 1) * w)
        y = _causal_dwconv(h_ref[0, :, cols], taps_ref[:, cols])
        return y * jax.nn.sigmoid(y)

    def l2norm(t):
        return t * lax.rsqrt(head_sum(t * t) + NORM_EPS)

    qn_s[...] = l2norm(conv_silu(0)) * (d ** -0.5)
    kn = l2norm(conv_silu(1))
    kn_s[...] = kn
    ab = h_ref[0, :, 4 * w:4 * w + LANES]
    src = _iota2((LANES, w), 0)
    head = _iota2((LANES, w), 1) >> HEAD_SHIFT
    a_rep = _dot_exact_rhs(ab, (src == head).astype(BF16), 3)
    b_rep = _dot_exact_rhs(ab, (src == head + GDN_HEADS).astype(BF16), 3)
    beta = jax.nn.sigmoid(b_rep)
    g_s[...] = -jnp.exp(alog_ref[...]) * _softplus(a_rep + dtb_ref[...])
    kb_s[...] = kn * beta
    vb_s[...] = conv_silu(2) * beta

    ri = _iota2((c, w), 0)
    ci = _iota2((c, w), 1) & (c - 1)
    tril = ci <= ri
    strict = ci < ri
    upper = (ri <= ci).astype(F32)
    inv_masks = _inverse_masks(c, w)
    tril_b = (_iota2((c, c), 1) <= _iota2((c, c), 0)).astype(BF16)
    ones_b = jnp.ones((c, c), BF16)

    grp = GDN_CHUNKS_PER_STEP
    n_groups = nc // grp

    def chunk_rows(n):
        return pl.ds(n * c if isinstance(n, int) else pl.multiple_of(n * c, c), c)

    def prep_stages(group):
        ns = [group * grp + i for i in range(grp)]
        rows = [chunk_rows(n) for n in ns]
        g = [g_s[r, :] for r in rows]
        gc = [_dot_exact_lhs(tril_b, x, 3) for x in g]
        gct = [_dot_exact_lhs(ones_b, x * upper, 3) for x in g]
        yield
        kn_c = [kn_s[r, :] for r in rows]
        kb_c = [kb_s[r, :] for r in rows]
        qn_c = [qn_s[r, :] for r in rows]
        kq = [_dot_nt(jnp.concatenate([kb, q], axis=0).astype(BF16), _block_diag(k.astype(BF16), bd_mask))
              for kb, q, k in zip(kb_c, qn_c, kn_c)]
        yield
        decay = [jnp.where(tril, jnp.exp(jnp.where(tril, a - b, 0.0)), 0.0) for a, b in zip(gc, gct)]
        lmat = [jnp.where(strict, x[0:c] * dk, 0.0) for x, dk in zip(kq, decay)]
        tinv = yield from _unit_lower_inverse(lmat, inv_masks, bd_mask)
        egc = [jnp.exp(x) for x in gc]
        rhs = [jnp.concatenate([_block_diag(vb_s[r, :].astype(BF16), bd_mask),
                                _block_diag((kb * e).astype(BF16), bd_mask)], axis=1)
               for r, kb, e in zip(rows, kb_c, egc)]
        uw = [_dot(t.astype(BF16), x) for t, x in zip(tinv, rhs)]
        yield
        for i, (n, r) in enumerate(zip(ns, rows)):
            glast = gc[i][c - 1:c, :]
            egl_s[n] = jnp.exp(glast)
            u_s[r, :] = uw[i][:, 0:w]
            w_s[r, :] = uw[i][:, w:2 * w].astype(BF16)
            a_s[r, :] = jnp.where(tril, kq[i][c:2 * c] * decay[i], 0.0).astype(BF16)
            qd_s[r, :] = (qn_c[i] * egc[i]).astype(BF16)
            kd_s[r, :] = (kn_c[i] * jnp.exp(glast - gc[i])).astype(BF16)

    lane_head = _iota2((d, w), 1) >> HEAD_SHIFT

    def scan_stages(group, state):
        for i in range(grp):
            n = group * grp + i
            rows = chunk_rows(n)
            st = state[0]
            st_bd = _block_diag(st.astype(BF16), bd_mask)
            ws_qs = _dot(jnp.concatenate([w_s[rows, :], qd_s[rows, :]], axis=0), st_bd)
            yield
            v_new = u_s[rows, :] - ws_qs[0:c]
            v_b = v_new.astype(BF16)
            o_s[rows, :] = ws_qs[c:2 * c] + _dot(a_s[rows, :], _block_diag(v_b, bd_mask))
            kv = _dot_tn(kd_s[rows, :], v_b)
            upd = jnp.zeros((d, w), F32)
            for h in range(GDN_HEADS):
                upd = upd + jnp.where(lane_head == h, kv[h * d:(h + 1) * d, :], 0.0)
            state[0] = st * egl_s[n] + upd
            yield

    _interleave(prep_stages(0))

    def group_body(group, st):
        state = [st]
        _interleave(prep_stages(group), scan_stages(group - 1, state))
        return state[0]

    state = [lax.fori_loop(1, n_groups, group_body, jnp.zeros((d, w), F32))]
    _interleave(scan_stages(n_groups - 1, state))

    o = o_s[...]
    o = o * lax.rsqrt(head_sum(o * o) * (1.0 / d) + NORM_EPS) * nw_ref[...]
    z = h_ref[0, :, 3 * w:4 * w]
    o_ref[0] = o * (z * jax.nn.sigmoid(z))


def _gated_deltanet(hg, conv_taps, alog_rep, dtb_rep, nw_rep):
    bsz, s, cols = hg.shape
    w = GDN_WIDTH
    assert GDN_CHUNK == HEAD_DIM and cols == GDN_COLS and s % GDN_CHUNK == 0
    return pl.pallas_call(
        _gdn_body,
        grid=(bsz,),
        in_specs=[pl.BlockSpec((1, s, cols), lambda b: (b, 0, 0), pipeline_mode=pl.Buffered(1)),
                  _const_spec(conv_taps.shape),
                  _const_spec((1, w)), _const_spec((1, w)), _const_spec((1, w))],
        out_specs=pl.BlockSpec((1, s, w), lambda b: (b, 0, 0)),
        out_shape=jax.ShapeDtypeStruct((bsz, s, w), F32),
        scratch_shapes=[pltpu.VMEM((s, w), F32)] * 6 + [pltpu.VMEM((s, w), BF16)] * 4 + [
            pltpu.VMEM((s // GDN_CHUNK, 1, w), F32), pltpu.VMEM((s, w), F32)],
        compiler_params=pltpu.CompilerParams(
            dimension_semantics=("parallel",), vmem_limit_bytes=V7X_VMEM_LIMIT_BYTES),
        name="gated_deltanet",
    )(hg, conv_taps, alog_rep, dtb_rep, nw_rep)


NSA_TQ = 256
NSA_SLC_STEP = 256
NSA_NCMP_PAD = 128
NSA_VPAD = 16


def _softmax_cols(s):
    m = jnp.max(s, axis=0, keepdims=True)
    e = jnp.exp(s - m)
    return e, jnp.sum(e, axis=0, keepdims=True)


def _nsa_body(q_ref, gate_ref, cmp_ref, kv_ref, pe_ref, w1_ref, w2_ref, ovlt_ref, o_ref,
              kc_s, vct_s, ks_s, kw_s, vst_s, vwt_s, oslc_s):
    qi = pl.program_id(2)
    tq, d, grp = NSA_TQ, HEAD_DIM, NSA_GROUP
    s_len = kv_ref.shape[1]
    n_sel = s_len // SEL_BLOCK
    seg = CMP_LEN // CMP_STRIDE
    band = WINDOW + tq
    wide = grp * tq

    @pl.when(qi == 0)
    def _():
        def relayout(i, carry):
            r0 = pl.multiple_of(i * LANES, LANES)
            blk = kv_ref[0, pl.ds(r0, LANES), :]
            block_of_row = (r0 + _iota2((LANES, LANES - d), 0)) >> SEL_SHIFT
            one_hot = (block_of_row == _iota2((LANES, LANES - d), 1)).astype(BF16)
            ks_s[pl.ds(r0, LANES), :] = jnp.concatenate([blk[:, 0:d].astype(BF16), one_hot], axis=1)
            kw_s[pl.ds(r0, LANES), :] = blk[:, 2 * d:3 * d].astype(BF16)
            blk_t = blk.T
            ones_row = (_iota2((NSA_VPAD, LANES), 0) == 0).astype(BF16)
            vst_s[:, pl.ds(r0, LANES)] = jnp.concatenate([blk_t[d:2 * d, :].astype(BF16), ones_row], axis=0)
            vwt_s[:, pl.ds(r0, LANES)] = jnp.concatenate([blk_t[3 * d:4 * d, :].astype(BF16), ones_row], axis=0)
            return carry

        lax.fori_loop(0, s_len // LANES, relayout, 0)
        pre = [jnp.zeros((NSA_NCMP_PAD, 2 * LANES), F32) for _ in range(seg)]
        for l in range(CMP_STRIDE):
            t_l = cmp_ref[0, pl.ds(l, NSA_NCMP_PAD, stride=CMP_STRIDE), :]
            for h in range(seg):
                lh = l + h * CMP_STRIDE
                pre[h] = pre[h] + _dot((t_l + pe_ref[lh:lh + 1, :]).astype(BF16), w1_ref[lh])
        hid = pre[0] + pltpu.roll(pre[1], NSA_NCMP_PAD - 1, 0)
        hid = hid * jax.nn.sigmoid(hid)
        kcv = _dot(hid.astype(BF16), w2_ref[...])
        kc_s[...] = kcv[:, 0:d].astype(BF16)
        vct_s[...] = kcv.T[d:2 * d, :].astype(BF16)

    ts = qi * tq
    q_t = (q_ref[0] * (d ** -0.5)).T
    q_t = jnp.concatenate([q_t[g * d:(g + 1) * d, :] for g in range(grp)], axis=1).astype(BF16)
    t_row = ts + (_iota2((1, wide), 1) & (tq - 1))
    t_row1 = t_row[:, 0:tq]

    def attend(score_parts, value_parts):
        m = None
        for sc in score_parts:
            cm = jnp.max(sc, axis=0, keepdims=True)
            m = cm if m is None else jnp.maximum(m, cm)
        acc = None
        for sc, value_t in zip(score_parts, value_parts):
            part = _dot(value_t, jnp.exp(sc - m).astype(BF16))
            acc = part if acc is None else acc + part
        return acc[0:d] / acc[d:d + 1]

    n_col = _iota2((NSA_NCMP_PAD, 1), 0)
    cmp_ok = (n_col * CMP_STRIDE + (CMP_LEN - 1) <= t_row) & (n_col < NSA_NCMP_PAD - 1)
    e, den = _softmax_cols(jnp.where(cmp_ok, _dot(kc_s[...], q_t), NEG))
    p_cmp = jnp.where(t_row >= CMP_LEN - 1, e / den, 0.0)
    o_cmp = _dot(vct_s[...], p_cmp.astype(BF16))

    p_sum = p_cmp[:, 0:tq]
    for g in range(1, grp):
        p_sum = p_sum + p_cmp[:, g * tq:(g + 1) * tq]
    imp = _dot_exact_lhs(ovlt_ref[...], p_sum, 3)
    j_idx = _iota2((n_sel, tq), 0)
    q_blk = t_row1 >> SEL_SHIFT
    forced = (j_idx == 0) | (j_idx == q_blk) | (j_idx == q_blk - 1)
    imp = jnp.where(forced, imp + FORCE_BONUS, imp)
    causal_blk = j_idx <= q_blk
    imp = jnp.where(causal_blk, imp, NEG)
    rank = jnp.zeros((n_sel, tq), F32)
    for jp in range(n_sel):
        row = imp[jp:jp + 1, :]
        beats = (row > imp) | ((row == imp) & (j_idx > jp))
        rank = rank + beats.astype(F32)
    selected = (rank < SEL_TOPN) & causal_blk

    blk0 = ts >> SEL_SHIFT
    sel_bias = jnp.where(selected, 0.0, NEG)
    past_bias = jnp.where(j_idx < blk0, sel_bias, NEG)
    zero_rows = jnp.zeros((LANES - d - n_sel, wide), BF16)

    def with_bias(bias):
        return jnp.concatenate([q_t, jnp.concatenate([bias.astype(BF16)] * grp, axis=1), zero_rows], axis=0)

    sc_diag = jnp.where(ts + _iota2((tq, 1), 0) <= t_row,
                        _dot(ks_s[pl.ds(ts, tq), :], with_bias(sel_bias)), NEG)
    q_past = with_bias(past_bias)

    def slc_attend(nk):
        if nk == 0:
            return attend([sc_diag], [vst_s[:, pl.ds(ts, tq)]])
        return attend([_dot(ks_s[0:nk, :], q_past), sc_diag], [vst_s[:, 0:nk], vst_s[:, pl.ds(ts, tq)]])

    n_var = (s_len - tq + NSA_SLC_STEP - 1) // NSA_SLC_STEP + 1
    variant = (ts + NSA_SLC_STEP - 1) // NSA_SLC_STEP
    for v in range(n_var):
        @pl.when(variant == v)
        def _(v=v):
            oslc_s[...] = slc_attend(min(v * NSA_SLC_STEP, s_len))
    o_slc = oslc_s[...]

    w0 = pl.multiple_of(jnp.maximum(ts - WINDOW, 0), tq)

    dist = t_row - (w0 + _iota2((band, 1), 0))
    sc_win = jnp.where((dist >= 0) & (dist < WINDOW), _dot(kw_s[pl.ds(w0, band), :], q_t), NEG)
    o_win = attend([sc_win], [vwt_s[:, pl.ds(w0, band)]])

    gt = jax.nn.sigmoid(gate_ref[0]).T
    outs = []
    for g in range(grp):
        cols = slice(g * tq, (g + 1) * tq)
        outs.append(gt[g:g + 1, :] * o_cmp[:, cols] + gt[grp + g:grp + g + 1, :] * o_slc[:, cols]
                    + gt[2 * grp + g:2 * grp + g + 1, :] * o_win[:, cols])
    o_ref[0] = jnp.concatenate(outs, axis=0).T


def _nsa_attention(hn, pe, w1, w2, overlap):
    bsz, s, _ = hn.shape
    tq, d = NSA_TQ, HEAD_DIM
    qw = NSA_GROUP * d
    kvw = 4 * d
    assert NSA_HEAD_COLS == qw + 2 * LANES + kvw and qw == kvw and s % tq == 0 and s >= WINDOW + tq
    per_w, per_l = NSA_HEAD_COLS // qw, NSA_HEAD_COLS // LANES
    return pl.pallas_call(
        _nsa_body,
        grid=(bsz, NSA_KV_HEADS, s // tq),
        in_specs=[
            pl.BlockSpec((1, tq, qw), lambda b, h, i: (b, i, per_w * h)),
            pl.BlockSpec((1, tq, LANES), lambda b, h, i: (b, i, per_l * h + qw // LANES)),
            pl.BlockSpec((1, s, LANES), lambda b, h, i: (b, 0, per_l * h + qw // LANES + 1)),
            pl.BlockSpec((1, s, kvw), lambda b, h, i: (b, 0, per_w * h + per_w - 1)),
            _const_spec(pe.shape), _const_spec(w1.shape), _const_spec(w2.shape), _const_spec(overlap.shape),
        ],
        out_specs=pl.BlockSpec((1, tq, qw), lambda b, h, i: (b, i, h)),
        out_shape=jax.ShapeDtypeStruct((bsz, s, NSA_WIDTH), F32),
        scratch_shapes=[pltpu.VMEM((NSA_NCMP_PAD, d), BF16), pltpu.VMEM((d, NSA_NCMP_PAD), BF16),
                        pltpu.VMEM((s, LANES), BF16), pltpu.VMEM((s, d), BF16),
                        pltpu.VMEM((d + NSA_VPAD, s), BF16), pltpu.VMEM((d + NSA_VPAD, s), BF16),
                        pltpu.VMEM((d, NSA_GROUP * tq), F32)],
        compiler_params=pltpu.CompilerParams(
            dimension_semantics=("parallel", "parallel", "arbitrary"),
            vmem_limit_bytes=V7X_VMEM_LIMIT_BYTES),
        name="nsa_attention",
    )(hn, hn, hn, hn, pe, w1, w2, overlap)


def _prep_w_in(w):
    dm = w.shape[0]
    sizes = (3 * CONV_WIDTH, 3 * GDN_WIDTH, GDN_WIDTH, GDN_HEADS, GDN_HEADS,
             NSA_WIDTH, 6 * NSA_KV_HEADS * HEAD_DIM, 3 * NSA_Q_HEADS)
    parts, start = [], 0
    for n in sizes:
        parts.append(w[:, start:start + n])
        start += n
    wconv, gqkv, gz, ga, gb, nq, nkv, ngate = parts
    pad = lambda n: jnp.zeros((dm, n), w.dtype)
    wgdn = jnp.concatenate([gqkv, gz, ga, gb, pad(LANES - 2 * GDN_HEADS)], axis=1)
    nkv = nkv.reshape(dm, 6, NSA_KV_HEADS, HEAD_DIM)
    ngate = ngate.reshape(dm, NSA_KV_HEADS, NSA_GROUP, 3)
    per_head = []
    for h in range(NSA_KV_HEADS):
        qw = NSA_GROUP * HEAD_DIM
        gates = ngate[:, h].transpose(0, 2, 1).reshape(dm, 3 * NSA_GROUP)
        per_head += [nq[:, h * qw:(h + 1) * qw], gates, pad(LANES - 3 * NSA_GROUP),
                     nkv[:, :, h, :].reshape(dm, 6 * HEAD_DIM)]
    wnsa = jnp.concatenate(per_head, axis=1)
    return wconv.astype(BF16), wgdn.astype(BF16), wnsa.astype(BF16)


def _prep_cmp_weights(pe_k, pe_v, k_w1, k_w2, v_w1, v_w2):
    d, hid = HEAD_DIM, k_w1.shape[1]
    pe = jnp.concatenate([pe_k, pe_v], axis=1)
    k1 = k_w1.reshape(CMP_LEN, d, hid)
    v1 = v_w1.reshape(CMP_LEN, d, hid)
    z1 = jnp.zeros_like(k1)
    w1 = jnp.concatenate([jnp.concatenate([k1, z1], axis=2),
                          jnp.concatenate([z1, v1], axis=2)], axis=1)
    z2 = jnp.zeros_like(k_w2)
    w2 = jnp.concatenate([jnp.concatenate([k_w2, z2], axis=1),
                          jnp.concatenate([z2, v_w2], axis=1)], axis=0)
    return pe, w1.astype(BF16), w2.astype(BF16)


def _overlap_matrix(s):
    n_cmp = (s - CMP_LEN) // CMP_STRIDE + 1
    t = jnp.arange(s)
    starts = jnp.arange(NSA_NCMP_PAD) * CMP_STRIDE
    cmp_tok = (t[None, :] >= starts[:, None]) & (t[None, :] < starts[:, None] + CMP_LEN)
    cmp_tok = cmp_tok & (jnp.arange(NSA_NCMP_PAD) < n_cmp)[:, None]
    sel_tok = (t[None, :] // SEL_BLOCK) == jnp.arange(s // SEL_BLOCK)[:, None]
    return ((sel_tok.astype(F32) @ cmp_tok.astype(F32).T) / CMP_LEN).astype(BF16)


def kernel(x, ffn1_w_gate, ffn1_w_up, ffn1_w_down, ln1_g, ln1_b, w_in, conv_w, gdn_conv_w, gdn_a_log, gdn_dt_bias, gdn_norm_w, cmp_pe_k, cmp_pe_v, cmp_k_w1, cmp_k_w2, cmp_v_w1, cmp_v_w2, w_out, ln2_g, ln2_b, ffn2_w_gate, ffn2_w_up, ffn2_w_down, ln3_g, ln3_b):
    bsz, s, dm = x.shape
    depth = w_in.shape[0]
    alpha = (2 * depth) ** 0.25
    m = bsz * s
    overlap = _overlap_matrix(s)
    row = lambda v: v.reshape(1, -1)
    h = x.reshape(m, dm)
    for l in range(depth):
        h = _ffn_ln(h, ffn1_w_gate[l].astype(BF16), ffn1_w_up[l].astype(BF16), ffn1_w_down[l].astype(BF16),
                    row(ln1_g[l]), row(ln1_b[l]), alpha=alpha)

        wconv, wgdn, wnsa = _prep_w_in(w_in[l])
        hc, hg, hn = _in_proj(h, wconv, wgdn, wnsa)
        y_a = _short_conv(hc.reshape(bsz, s, -1), conv_w[l].T)
        y_b = _gated_deltanet(hg.reshape(bsz, s, -1), gdn_conv_w[l].T,
                              row(jnp.repeat(gdn_a_log[l], HEAD_DIM)), row(jnp.repeat(gdn_dt_bias[l], HEAD_DIM)),
                              row(jnp.tile(gdn_norm_w[l], GDN_HEADS)))
        pe, w1, w2 = _prep_cmp_weights(cmp_pe_k[l], cmp_pe_v[l], cmp_k_w1[l], cmp_k_w2[l], cmp_v_w1[l], cmp_v_w2[l])
        y_c = _nsa_attention(hn.reshape(bsz, s, -1), pe, w1, w2, overlap)
        wo = w_out[l].astype(BF16)
        h = _out_proj_ln(h, y_a.reshape(m, -1), y_b.reshape(m, -1), y_c.reshape(m, -1),
                         wo[:CONV_WIDTH], wo[CONV_WIDTH:CONV_WIDTH + GDN_WIDTH], wo[CONV_WIDTH + GDN_WIDTH:],
                         row(ln2_g[l]), row(ln2_b[l]), alpha=alpha)

        h = _ffn_ln(h, ffn2_w_gate[l].astype(BF16), ffn2_w_up[l].astype(BF16), ffn2_w_down[l].astype(BF16),
                    row(ln3_g[l]), row(ln3_b[l]), alpha=alpha)
    return h.reshape(bsz, s, dm)
```

```python
import functools

import jax
import jax.numpy as jnp
from jax import lax
from jax.experimental import pallas as pl
from jax.experimental.pallas import tpu as pltpu

F32 = jnp.float32
BF16 = jnp.bfloat16

HEAD_DIM = 64
CONV_WIDTH = 256
CONV_K = 3
GDN_HEADS = 4
GDN_WIDTH = GDN_HEADS * HEAD_DIM
GDN_CONV_K = 4
GDN_CHUNK = 64
NSA_Q_HEADS = 8
NSA_KV_HEADS = 2
NSA_GROUP = NSA_Q_HEADS // NSA_KV_HEADS
NSA_WIDTH = NSA_Q_HEADS * HEAD_DIM
CMP_LEN = 32
CMP_STRIDE = 16
SEL_BLOCK = 64
SEL_TOPN = 8
WINDOW = 512
FORCE_BONUS = 1e3
LN_EPS = 1e-5
NORM_EPS = 1e-6
NEG = -1e30

V7X_VMEM_LIMIT_BYTES = 56 * 1024 * 1024
LANES = 128
HEAD_SHIFT = HEAD_DIM.bit_length() - 1
SEL_SHIFT = SEL_BLOCK.bit_length() - 1

GDN_COLS = 4 * GDN_WIDTH + LANES
NSA_HEAD_COLS = NSA_GROUP * HEAD_DIM + 6 * HEAD_DIM + LANES


def _layer_norm(r, g, b):
    mu = jnp.mean(r, axis=-1, keepdims=True)
    c = r - mu
    var = jnp.mean(c * c, axis=-1, keepdims=True)
    return c * lax.rsqrt(var + LN_EPS) * g + b


def _const_spec(shape):
    return pl.BlockSpec(shape, lambda *_: (0,) * len(shape), pipeline_mode=pl.Buffered(1))


def _ffn_ln_body(x_ref, wg_ref, wu_ref, wd_ref, g_ref, b_ref, o_ref, *, alpha):
    x = x_ref[...]
    xb = x.astype(BF16)
    hg = jnp.dot(xb, wg_ref[...], preferred_element_type=F32)
    hu = jnp.dot(xb, wu_ref[...], preferred_element_type=F32)
    a = (hg * jax.nn.sigmoid(hg) * hu).astype(BF16)
    y = jnp.dot(a, wd_ref[...], preferred_element_type=F32)
    o_ref[...] = _layer_norm(alpha * x + 0.5 * y, g_ref[...], b_ref[...])


def _ffn_ln(x, wg, wu, wd, g, b, *, alpha, tm=512):
    m, d = x.shape
    f = wg.shape[1]
    return pl.pallas_call(
        functools.partial(_ffn_ln_body, alpha=alpha),
        grid=(m // tm,),
        in_specs=[
            pl.BlockSpec((tm, d), lambda i: (i, 0)),
            _const_spec((d, f)), _const_spec((d, f)), _const_spec((f, d)),
            _const_spec((1, d)), _const_spec((1, d)),
        ],
        out_specs=pl.BlockSpec((tm, d), lambda i: (i, 0)),
        out_shape=jax.ShapeDtypeStruct((m, d), F32),
        compiler_params=pltpu.CompilerParams(
            dimension_semantics=("parallel",), vmem_limit_bytes=V7X_VMEM_LIMIT_BYTES),
        name="ffn_ln",
    )(x, wg, wu, wd, g, b)


def _in_proj_body(x_ref, wc_ref, wg_ref, wn_ref, hc_ref, hg_ref, hn_ref):
    xb = x_ref[...].astype(BF16)
    hc_ref[...] = jnp.dot(xb, wc_ref[...], preferred_element_type=F32)
    hg_ref[...] = jnp.dot(xb, wg_ref[...], preferred_element_type=F32)
    hn_ref[...] = jnp.dot(xb, wn_ref[...], preferred_element_type=F32)


def _in_proj(x, wc, wg, wn, *, tm=512):
    m, d = x.shape
    widths = (wc.shape[1], wg.shape[1], wn.shape[1])
    return pl.pallas_call(
        _in_proj_body,
        grid=(m // tm,),
        in_specs=[pl.BlockSpec((tm, d), lambda i: (i, 0))] + [_const_spec((d, w)) for w in widths],
        out_specs=[pl.BlockSpec((tm, w), lambda i: (i, 0)) for w in widths],
        out_shape=[jax.ShapeDtypeStruct((m, w), F32) for w in widths],
        compiler_params=pltpu.CompilerParams(
            dimension_semantics=("parallel",), vmem_limit_bytes=V7X_VMEM_LIMIT_BYTES),
        name="in_proj",
    )(x, wc, wg, wn)


SUBLANES = 8


def _out_proj_ln_body(x_ref, hc_ref, halo_ref, yb_ref, yc_ref, taps_ref, wa_ref, wb_ref, wc_ref, g_ref, b_ref,
                      o_ref, *, alpha, tiles_per_seq):
    c = CONV_WIDTH
    h = hc_ref[...]
    u = h[:, c:2 * c] * h[:, 2 * c:3 * c]
    hh = halo_ref[...]
    at_seq_start = pl.program_id(0) % tiles_per_seq == 0
    u_prev = jnp.where(at_seq_start, 0.0, hh[:, c:2 * c] * hh[:, 2 * c:3 * c])
    ext = jnp.concatenate([u_prev, u], axis=0)
    taps = taps_ref[...]
    k = taps.shape[0]
    conv = u * taps[k - 1:k, :]
    for j in range(k - 1):
        conv = conv + pltpu.roll(ext, k - 1 - j, 0)[SUBLANES:, :] * taps[j:j + 1, :]
    y_a = h[:, 0:c] * conv
    y = jnp.dot(y_a.astype(BF16), wa_ref[...], preferred_element_type=F32)
    y += jnp.dot(yb_ref[...].astype(BF16), wb_ref[...], preferred_element_type=F32)
    y += jnp.dot(yc_ref[...].astype(BF16), wc_ref[...], preferred_element_type=F32)
    o_ref[...] = _layer_norm(alpha * x_ref[...] + y, g_ref[...], b_ref[...])


def _out_proj_ln(x, hc, yb, yc, conv_taps, wa, wb, wc, g, b, *, alpha, seq_len, tm=512):
    m, d = x.shape
    assert seq_len % tm == 0 and CONV_K - 1 <= SUBLANES
    row = lambda w: pl.BlockSpec((tm, w), lambda i: (i, 0))
    halo = pl.BlockSpec((SUBLANES, hc.shape[1]), lambda i: (jnp.maximum(i * (tm // SUBLANES) - 1, 0), 0))
    return pl.pallas_call(
        functools.partial(_out_proj_ln_body, alpha=alpha, tiles_per_seq=seq_len // tm),
        grid=(m // tm,),
        in_specs=[row(d), row(hc.shape[1]), halo, row(yb.shape[1]), row(yc.shape[1]),
                  _const_spec(conv_taps.shape),
                  _const_spec(wa.shape), _const_spec(wb.shape), _const_spec(wc.shape),
                  _const_spec((1, d)), _const_spec((1, d))],
        out_specs=row(d),
        out_shape=jax.ShapeDtypeStruct((m, d), F32),
        compiler_params=pltpu.CompilerParams(
            dimension_semantics=("parallel",), vmem_limit_bytes=V7X_VMEM_LIMIT_BYTES),
        name="out_proj_ln",
    )(x, hc, hc, yb, yc, conv_taps, wa, wb, wc, g, b)


def _shift_rows(u, s):
    rows = lax.broadcasted_iota(jnp.int32, u.shape, 0)
    return jnp.where(rows >= s, pltpu.roll(u, s, 0), 0.0)


def _causal_dwconv(u, w):
    k = w.shape[0]
    y = u * w[k - 1:k, :]
    for j in range(k - 1):
        y = y + _shift_rows(u, k - 1 - j) * w[j:j + 1, :]
    return y


def _dot(a, b):
    return jnp.dot(a, b, preferred_element_type=F32)


def _dot_nt(a, b):
    return lax.dot_general(a, b, (((1,), (1,)), ((), ())), preferred_element_type=F32)


def _dot_tn(a, b):
    return lax.dot_general(a, b, (((0,), (0,)), ((), ())), preferred_element_type=F32)


def _split_bf16(a, terms):
    parts, rest = [], a
    for _ in range(terms):
        p = rest.astype(BF16)
        parts.append(p)
        rest = rest - p.astype(F32)
    return parts


def _dot_exact_lhs(m_bf16, a, terms):
    return sum(_dot(m_bf16, p) for p in _split_bf16(a, terms))


def _dot_exact_rhs(a, m_bf16, terms):
    return sum(_dot(p, m_bf16) for p in _split_bf16(a, terms))


def _dot_hl(a, b):
    a_hi, a_lo = _split_bf16(a, 2)
    b_hi, b_lo = _split_bf16(b, 2)
    return _dot(a_hi, b_hi) + (_dot(a_hi, b_lo) + _dot(a_lo, b_hi))


def _iota2(shape, dim):
    return lax.broadcasted_iota(jnp.int32, shape, dim)


GDN_INV_BASE = 8
GDN_CHUNKS_PER_STEP = 4


def _block_diag(x, bd_mask):
    return jnp.concatenate([x] * (x.shape[1] // x.shape[0]), axis=0) * bd_mask


def _heads_dot_hl(a, b, bd_mask):
    a_hi, a_lo = _split_bf16(a, 2)
    b_hi, b_lo = _split_bf16(b, 2)
    r = a.shape[0]
    hi = _dot(jnp.concatenate([a_hi, a_lo], axis=0), _block_diag(b_hi, bd_mask))
    return hi[0:r] + hi[r:2 * r] + _dot(a_hi, _block_diag(b_lo, bd_mask))


def _inverse_masks(c, width):
    ri, ci = _iota2((c, width), 0), _iota2((c, width), 1) & (c - 1)
    base = GDN_INV_BASE.bit_length() - 1
    eye = (ci == ri).astype(F32)
    diag = (ri >> base) == (ci >> base)
    levels = [((ri >> (s + 1)) == (ci >> (s + 1))) & ((ri >> s) != (ci >> s))
              for s in range(base, c.bit_length() - 1)]
    return eye, diag, levels


def _unit_lower_inverse(lmats, masks, bd_mask):
    eye, diag, levels = masks
    c = lmats[0].shape[0]
    hdot = lambda a, b: _heads_dot_hl(a, b, bd_mask)
    l0 = [jnp.where(diag, l, 0.0) for l in lmats]
    p = [eye - x for x in l0]
    m = [hdot(x, x) for x in l0]
    yield
    pm = [hdot(jnp.concatenate([pi, mi], axis=0), mi) for pi, mi in zip(p, m)]
    yield
    p = [pi + x[0:c] for pi, x in zip(p, pm)]
    inv = [pi + hdot(pi, x[c:2 * c]) for pi, x in zip(p, pm)]
    yield
    for level in levels:
        t = [hdot(i, jnp.where(level, l, 0.0)) for i, l in zip(inv, lmats)]
        yield
        inv = [i - hdot(ti, i) for i, ti in zip(inv, t)]
        yield
    return inv


def _interleave(*stage_generators):
    live = list(stage_generators)
    while live:
        for gen in list(live):
            try:
                next(gen)
            except StopIteration:
                live.remove(gen)


def _softplus(x):
    return jnp.maximum(x, 0.0) + jnp.log1p(jnp.exp(-jnp.abs(x)))


def _gdn_body(h_ref, taps_ref, alog_ref, dtb_ref, nw_ref, o_ref,
              qn_s, kn_s, kb_s, vb_s, g_s, u_s, w_s, a_s, qd_s, kd_s, egl_s, o_s):
    s = h_ref.shape[1]
    c = GDN_CHUNK
    d = HEAD_DIM
    w = GDN_WIDTH
    nc = s // c

    bd_mask = (_iota2((w, w), 0) >> HEAD_SHIFT == _iota2((w, w), 1) >> HEAD_SHIFT).astype(BF16)

    def head_sum(t):
        return _dot_exact_rhs(t, bd_mask, 2)

    def conv_silu(group):
        cols = slice(group * w, (group + 1) * w)
        y = _causal_dwconv(h_ref[0, :, cols], taps_ref[:, cols])
        return y * jax.nn.sigmoid(y)

    def l2norm(t):
        return t * lax.rsqrt(head_sum(t * t) + NORM_EPS)

    qn_s[...] = l2norm(conv_silu(0)) * (d ** -0.5)
    kn = l2norm(conv_silu(1))
    kn_s[...] = kn
    ab = h_ref[0, :, 4 * w:4 * w + LANES]
    src = _iota2((LANES, w), 0)
    head = _iota2((LANES, w), 1) >> HEAD_SHIFT
    g_small = -jnp.exp(alog_ref[...]) * _softplus(ab + dtb_ref[...])
    g_s[...] = _dot_exact_rhs(g_small, (src == head).astype(BF16), 3)
    beta = _dot_exact_rhs(jax.nn.sigmoid(ab), (src == head + GDN_HEADS).astype(BF16), 3)
    kb_s[...] = kn * beta
    vb_s[...] = conv_silu(2) * beta

    ri = _iota2((c, w), 0)
    ci = _iota2((c, w), 1) & (c - 1)
    tril = ci <= ri
    strict = ci < ri
    upper = (ri <= ci).astype(F32)
    inv_masks = _inverse_masks(c, w)
    tril_b = (_iota2((c, c), 1) <= _iota2((c, c), 0)).astype(BF16)
    ones_b = jnp.ones((c, c), BF16)

    grp = GDN_CHUNKS_PER_STEP
    n_groups = nc // grp

    def chunk_rows(n):
        return pl.ds(n * c if isinstance(n, int) else pl.multiple_of(n * c, c), c)

    def prep_stages(group):
        ns = [group * grp + i for i in range(grp)]
        rows = [chunk_rows(n) for n in ns]
        g = [g_s[r, :] for r in rows]
        gc = [_dot_exact_lhs(tril_b, x, 3) for x in g]
        gct = [_dot_exact_lhs(ones_b, x * upper, 3) for x in g]
        yield
        kn_c = [kn_s[r, :] for r in rows]
        kb_c = [kb_s[r, :] for r in rows]
        qn_c = [qn_s[r, :] for r in rows]
        kq = [_dot_nt(jnp.concatenate([kb, q], axis=0).astype(BF16), _block_diag(k.astype(BF16), bd_mask))
              for kb, q, k in zip(kb_c, qn_c, kn_c)]
        yield
        decay = [jnp.where(tril, jnp.exp(jnp.where(tril, a - b, 0.0)), 0.0) for a, b in zip(gc, gct)]
        lmat = [jnp.where(strict, x[0:c] * dk, 0.0) for x, dk in zip(kq, decay)]
        tinv = yield from _unit_lower_inverse(lmat, inv_masks, bd_mask)
        egc = [jnp.exp(x) for x in gc]
        rhs = [jnp.concatenate([_block_diag(vb_s[r, :].astype(BF16), bd_mask),
                                _block_diag((kb * e).astype(BF16), bd_mask)], axis=1)
               for r, kb, e in zip(rows, kb_c, egc)]
        uw = [_dot(t.astype(BF16), x) for t, x in zip(tinv, rhs)]
        yield
        for i, (n, r) in enumerate(zip(ns, rows)):
            glast = gc[i][c - 1:c, :]
            egl_s[n] = jnp.exp(glast)
            u_s[r, :] = uw[i][:, 0:w]
            w_s[r, :] = uw[i][:, w:2 * w].astype(BF16)
            a_s[r, :] = jnp.where(tril, kq[i][c:2 * c] * decay[i], 0.0).astype(BF16)
            qd_s[r, :] = (qn_c[i] * egc[i]).astype(BF16)
            kd_s[r, :] = (kn_c[i] * jnp.exp(glast - gc[i])).astype(BF16)

    lane_head = _iota2((d, w), 1) >> HEAD_SHIFT

    def scan_stages(group, state):
        for i in range(grp):
            n = group * grp + i
            rows = chunk_rows(n)
            st = state[0]
            st_bd = _block_diag(st.astype(BF16), bd_mask)
            ws_qs = _dot(jnp.concatenate([w_s[rows, :], qd_s[rows, :]], axis=0), st_bd)
            yield
            v_new = u_s[rows, :] - ws_qs[0:c]
            v_b = v_new.astype(BF16)
            o_s[rows, :] = ws_qs[c:2 * c] + _dot(a_s[rows, :], _block_diag(v_b, bd_mask))
            kv = _dot_tn(kd_s[rows, :], v_b)
            upd = jnp.zeros((d, w), F32)
            for h in range(GDN_HEADS):
                upd = upd + jnp.where(lane_head == h, kv[h * d:(h + 1) * d, :], 0.0)
            state[0] = st * egl_s[n] + upd
            yield

    _interleave(prep_stages(0))

    def group_body(group, st):
        state = [st]
        _interleave(prep_stages(group), scan_stages(group - 1, state))
        return state[0]

    state = [lax.fori_loop(1, n_groups, group_body, jnp.zeros((d, w), F32))]
    _interleave(scan_stages(n_groups - 1, state))

    o = o_s[...]
    o = o * lax.rsqrt(head_sum(o * o) * (1.0 / d) + NORM_EPS) * nw_ref[...]
    z = h_ref[0, :, 3 * w:4 * w]
    o_ref[0] = o * (z * jax.nn.sigmoid(z))


def _gated_deltanet(hg, conv_taps, alog_rep, dtb_rep, nw_rep):
    bsz, s, cols = hg.shape
    w = GDN_WIDTH
    assert GDN_CHUNK == HEAD_DIM and cols == GDN_COLS and s % GDN_CHUNK == 0
    return pl.pallas_call(
        _gdn_body,
        grid=(bsz,),
        in_specs=[pl.BlockSpec((1, s, cols), lambda b: (b, 0, 0), pipeline_mode=pl.Buffered(1)),
                  _const_spec(conv_taps.shape),
                  _const_spec((1, LANES)), _const_spec((1, LANES)), _const_spec((1, w))],
        out_specs=pl.BlockSpec((1, s, w), lambda b: (b, 0, 0)),
        out_shape=jax.ShapeDtypeStruct((bsz, s, w), F32),
        scratch_shapes=[pltpu.VMEM((s, w), F32)] * 6 + [pltpu.VMEM((s, w), BF16)] * 4 + [
            pltpu.VMEM((s // GDN_CHUNK, 1, w), F32), pltpu.VMEM((s, w), F32)],
        compiler_params=pltpu.CompilerParams(
            dimension_semantics=("parallel",), vmem_limit_bytes=V7X_VMEM_LIMIT_BYTES),
        name="gated_deltanet",
    )(hg, conv_taps, alog_rep, dtb_rep, nw_rep)


NSA_TQ = 256
NSA_SLC_STEP = 256
NSA_NCMP_PAD = 128
NSA_VPAD = 16


def _softmax_cols(s):
    m = jnp.max(s, axis=0, keepdims=True)
    e = jnp.exp(s - m)
    return e, jnp.sum(e, axis=0, keepdims=True)


def _nsa_body(q_ref, gate_ref, cmp_ref, kv_ref, pe_ref, w1_ref, w2_ref, ovlt_ref, o_ref,
              kc_s, vct_s, ks_s, kw_s, vst_s, vwt_s, oslc_s):
    qi = pl.program_id(2)
    tq, d, grp = NSA_TQ, HEAD_DIM, NSA_GROUP
    s_len = kv_ref.shape[1]
    n_sel = s_len // SEL_BLOCK
    seg = CMP_LEN // CMP_STRIDE
    band = WINDOW + tq
    wide = grp * tq

    @pl.when(qi == 0)
    def _():
        def relayout(i, carry):
            r0 = pl.multiple_of(i * LANES, LANES)
            blk = kv_ref[0, pl.ds(r0, LANES), :]
            block_of_row = (r0 + _iota2((LANES, LANES - d), 0)) >> SEL_SHIFT
            one_hot = (block_of_row == _iota2((LANES, LANES - d), 1)).astype(BF16)
            ks_s[pl.ds(r0, LANES), :] = jnp.concatenate([blk[:, 0:d].astype(BF16), one_hot], axis=1)
            kw_s[pl.ds(r0, LANES), :] = blk[:, 2 * d:3 * d].astype(BF16)
            blk_t = blk.T
            ones_row = (_iota2((NSA_VPAD, LANES), 0) == 0).astype(BF16)
            vst_s[:, pl.ds(r0, LANES)] = jnp.concatenate([blk_t[d:2 * d, :].astype(BF16), ones_row], axis=0)
            vwt_s[:, pl.ds(r0, LANES)] = jnp.concatenate([blk_t[3 * d:4 * d, :].astype(BF16), ones_row], axis=0)
            return carry

        lax.fori_loop(0, s_len // LANES, relayout, 0)
        pre = [jnp.zeros((NSA_NCMP_PAD, 2 * LANES), F32) for _ in range(seg)]
        for l in range(CMP_STRIDE):
            t_l = cmp_ref[0, pl.ds(l, NSA_NCMP_PAD, stride=CMP_STRIDE), :]
            for h in range(seg):
                lh = l + h * CMP_STRIDE
                pre[h] = pre[h] + _dot((t_l + pe_ref[lh:lh + 1, :]).astype(BF16), w1_ref[lh])
        hid = pre[0] + pltpu.roll(pre[1], NSA_NCMP_PAD - 1, 0)
        hid = hid * jax.nn.sigmoid(hid)
        kcv = _dot(hid.astype(BF16), w2_ref[...])
        kc_s[...] = kcv[:, 0:d].astype(BF16)
        vct_s[...] = kcv.T[d:2 * d, :].astype(BF16)

    ts = qi * tq
    q_t = (q_ref[0] * (d ** -0.5)).T
    q_t = jnp.concatenate([q_t[g * d:(g + 1) * d, :] for g in range(grp)], axis=1).astype(BF16)
    t_row = ts + (_iota2((1, wide), 1) & (tq - 1))
    t_row1 = t_row[:, 0:tq]

    def attend(score_parts, value_parts):
        m = None
        for sc in score_parts:
            cm = jnp.max(sc, axis=0, keepdims=True)
            m = cm if m is None else jnp.maximum(m, cm)
        acc = None
        for sc, value_t in zip(score_parts, value_parts):
            part = _dot(value_t, jnp.exp(sc - m).astype(BF16))
            acc = part if acc is None else acc + part
        return acc[0:d] / acc[d:d + 1]

    n_col = _iota2((NSA_NCMP_PAD, 1), 0)
    cmp_ok = (n_col * CMP_STRIDE + (CMP_LEN - 1) <= t_row) & (n_col < NSA_NCMP_PAD - 1)
    e, den = _softmax_cols(jnp.where(cmp_ok, _dot(kc_s[...], q_t), NEG))
    p_cmp = jnp.where(t_row >= CMP_LEN - 1, e / den, 0.0)
    o_cmp = _dot(vct_s[...], p_cmp.astype(BF16))

    p_sum = p_cmp[:, 0:tq]
    for g in range(1, grp):
        p_sum = p_sum + p_cmp[:, g * tq:(g + 1) * tq]
    imp = _dot_exact_lhs(ovlt_ref[...], p_sum, 3)
    j_idx = _iota2((n_sel, tq), 0)
    q_blk = t_row1 >> SEL_SHIFT
    forced = (j_idx == 0) | (j_idx == q_blk) | (j_idx == q_blk - 1)
    imp = jnp.where(forced, imp + FORCE_BONUS, imp)
    causal_blk = j_idx <= q_blk
    imp = jnp.where(causal_blk, imp, NEG)
    rank = jnp.zeros((n_sel, tq), F32)
    for jp in range(n_sel):
        row = imp[jp:jp + 1, :]
        beats = (row > imp) | ((row == imp) & (j_idx > jp))
        rank = rank + beats.astype(F32)
    selected = (rank < SEL_TOPN) & causal_blk

    blk0 = ts >> SEL_SHIFT
    sel_bias = jnp.where(selected, 0.0, NEG)
    past_bias = jnp.where(j_idx < blk0, sel_bias, NEG)
    zero_rows = jnp.zeros((LANES - d - n_sel, wide), BF16)

    def with_bias(bias):
        return jnp.concatenate([q_t, jnp.concatenate([bias.astype(BF16)] * grp, axis=1), zero_rows], axis=0)

    sc_diag = jnp.where(ts + _iota2((tq, 1), 0) <= t_row,
                        _dot(ks_s[pl.ds(ts, tq), :], with_bias(sel_bias)), NEG)
    q_past = with_bias(past_bias)

    def slc_attend(nk):
        if nk == 0:
            return attend([sc_diag], [vst_s[:, pl.ds(ts, tq)]])
        return attend([_dot(ks_s[0:nk, :], q_past), sc_diag], [vst_s[:, 0:nk], vst_s[:, pl.ds(ts, tq)]])

    n_var = (s_len - tq + NSA_SLC_STEP - 1) // NSA_SLC_STEP + 1
    variant = (ts + NSA_SLC_STEP - 1) // NSA_SLC_STEP
    for v in range(n_var):
        @pl.when(variant == v)
        def _(v=v):
            oslc_s[...] = slc_attend(min(v * NSA_SLC_STEP, s_len))
    o_slc = oslc_s[...]

    w0 = pl.multiple_of(jnp.maximum(ts - WINDOW, 0), tq)

    dist = t_row - (w0 + _iota2((band, 1), 0))
    sc_win = jnp.where((dist >= 0) & (dist < WINDOW), _dot(kw_s[pl.ds(w0, band), :], q_t), NEG)
    o_win = attend([sc_win], [vwt_s[:, pl.ds(w0, band)]])

    gt = jax.nn.sigmoid(gate_ref[0]).T
    outs = []
    for g in range(grp):
        cols = slice(g * tq, (g + 1) * tq)
        outs.append(gt[g:g + 1, :] * o_cmp[:, cols] + gt[grp + g:grp + g + 1, :] * o_slc[:, cols]
                    + gt[2 * grp + g:2 * grp + g + 1, :] * o_win[:, cols])
    o_ref[0] = jnp.concatenate(outs, axis=0).T


def _nsa_attention(hn, pe, w1, w2, overlap):
    bsz, s, _ = hn.shape
    tq, d = NSA_TQ, HEAD_DIM
    qw = NSA_GROUP * d
    kvw = 4 * d
    assert NSA_HEAD_COLS == qw + 2 * LANES + kvw and qw == kvw and s % tq == 0 and s >= WINDOW + tq
    per_w, per_l = NSA_HEAD_COLS // qw, NSA_HEAD_COLS // LANES
    return pl.pallas_call(
        _nsa_body,
        grid=(bsz, NSA_KV_HEADS, s // tq),
        in_specs=[
            pl.BlockSpec((1, tq, qw), lambda b, h, i: (b, i, per_w * h)),
            pl.BlockSpec((1, tq, LANES), lambda b, h, i: (b, i, per_l * h + qw // LANES)),
            pl.BlockSpec((1, s, LANES), lambda b, h, i: (b, 0, per_l * h + qw // LANES + 1)),
            pl.BlockSpec((1, s, kvw), lambda b, h, i: (b, 0, per_w * h + per_w - 1)),
            _const_spec(pe.shape), _const_spec(w1.shape), _const_spec(w2.shape), _const_spec(overlap.shape),
        ],
        out_specs=pl.BlockSpec((1, tq, qw), lambda b, h, i: (b, i, h)),
        out_shape=jax.ShapeDtypeStruct((bsz, s, NSA_WIDTH), F32),
        scratch_shapes=[pltpu.VMEM((NSA_NCMP_PAD, d), BF16), pltpu.VMEM((d, NSA_NCMP_PAD), BF16),
                        pltpu.VMEM((s, LANES), BF16), pltpu.VMEM((s, d), BF16),
                        pltpu.VMEM((d + NSA_VPAD, s), BF16), pltpu.VMEM((d + NSA_VPAD, s), BF16),
                        pltpu.VMEM((d, NSA_GROUP * tq), F32)],
        compiler_params=pltpu.CompilerParams(
            dimension_semantics=("parallel", "parallel", "arbitrary"),
            vmem_limit_bytes=V7X_VMEM_LIMIT_BYTES),
        name="nsa_attention",
    )(hn, hn, hn, hn, pe, w1, w2, overlap)


def _prep_w_in(w):
    dm = w.shape[0]
    sizes = (3 * CONV_WIDTH, 3 * GDN_WIDTH, GDN_WIDTH, GDN_HEADS, GDN_HEADS,
             NSA_WIDTH, 6 * NSA_KV_HEADS * HEAD_DIM, 3 * NSA_Q_HEADS)
    parts, start = [], 0
    for n in sizes:
        parts.append(w[:, start:start + n])
        start += n
    wconv, gqkv, gz, ga, gb, nq, nkv, ngate = parts
    pad = lambda n: jnp.zeros((dm, n), w.dtype)
    wgdn = jnp.concatenate([gqkv, gz, ga, gb, pad(LANES - 2 * GDN_HEADS)], axis=1)
    nkv = nkv.reshape(dm, 6, NSA_KV_HEADS, HEAD_DIM)
    ngate = ngate.reshape(dm, NSA_KV_HEADS, NSA_GROUP, 3)
    per_head = []
    for h in range(NSA_KV_HEADS):
        qw = NSA_GROUP * HEAD_DIM
        gates = ngate[:, h].transpose(0, 2, 1).reshape(dm, 3 * NSA_GROUP)
        per_head += [nq[:, h * qw:(h + 1) * qw], gates, pad(LANES - 3 * NSA_GROUP),
                     nkv[:, :, h, :].reshape(dm, 6 * HEAD_DIM)]
    wnsa = jnp.concatenate(per_head, axis=1)
    return wconv.astype(BF16), wgdn.astype(BF16), wnsa.astype(BF16)


def _prep_cmp_weights(pe_k, pe_v, k_w1, k_w2, v_w1, v_w2):
    d, hid = HEAD_DIM, k_w1.shape[1]
    pe = jnp.concatenate([pe_k, pe_v], axis=1)
    k1 = k_w1.reshape(CMP_LEN, d, hid)
    v1 = v_w1.reshape(CMP_LEN, d, hid)
    z1 = jnp.zeros_like(k1)
    w1 = jnp.concatenate([jnp.concatenate([k1, z1], axis=2),
                          jnp.concatenate([z1, v1], axis=2)], axis=1)
    z2 = jnp.zeros_like(k_w2)
    w2 = jnp.concatenate([jnp.concatenate([k_w2, z2], axis=1),
                          jnp.concatenate([z2, v_w2], axis=1)], axis=0)
    return pe, w1.astype(BF16), w2.astype(BF16)


def _overlap_matrix(s):
    n_cmp = (s - CMP_LEN) // CMP_STRIDE + 1
    t = jnp.arange(s)
    starts = jnp.arange(NSA_NCMP_PAD) * CMP_STRIDE
    cmp_tok = (t[None, :] >= starts[:, None]) & (t[None, :] < starts[:, None] + CMP_LEN)
    cmp_tok = cmp_tok & (jnp.arange(NSA_NCMP_PAD) < n_cmp)[:, None]
    sel_tok = (t[None, :] // SEL_BLOCK) == jnp.arange(s // SEL_BLOCK)[:, None]
    return ((sel_tok.astype(F32) @ cmp_tok.astype(F32).T) / CMP_LEN).astype(BF16)


def kernel(x, ffn1_w_gate, ffn1_w_up, ffn1_w_down, ln1_g, ln1_b, w_in, conv_w, gdn_conv_w, gdn_a_log, gdn_dt_bias, gdn_norm_w, cmp_pe_k, cmp_pe_v, cmp_k_w1, cmp_k_w2, cmp_v_w1, cmp_v_w2, w_out, ln2_g, ln2_b, ffn2_w_gate, ffn2_w_up, ffn2_w_down, ln3_g, ln3_b):
    bsz, s, dm = x.shape
    depth = w_in.shape[0]
    alpha = (2 * depth) ** 0.25
    m = bsz * s
    overlap = _overlap_matrix(s)
    row = lambda v: v.reshape(1, -1)
    h = x.reshape(m, dm)
    for l in range(depth):
        h = _ffn_ln(h, ffn1_w_gate[l].astype(BF16), ffn1_w_up[l].astype(BF16), ffn1_w_down[l].astype(BF16),
                    row(ln1_g[l]), row(ln1_b[l]), alpha=alpha)

        wconv, wgdn, wnsa = _prep_w_in(w_in[l])
        hc, hg, hn = _in_proj(h, wconv, wgdn, wnsa)
        y_b = _gated_deltanet(hg.reshape(bsz, s, -1), gdn_conv_w[l].T,
                              row(jnp.pad(gdn_a_log[l], (0, LANES - GDN_HEADS))),
                              row(jnp.pad(gdn_dt_bias[l], (0, LANES - GDN_HEADS))),
                              row(jnp.tile(gdn_norm_w[l], GDN_HEADS)))
        pe, w1, w2 = _prep_cmp_weights(cmp_pe_k[l], cmp_pe_v[l], cmp_k_w1[l], cmp_k_w2[l], cmp_v_w1[l], cmp_v_w2[l])
        y_c = _nsa_attention(hn.reshape(bsz, s, -1), pe, w1, w2, overlap)
        wo = w_out[l].astype(BF16)
        h = _out_proj_ln(h, hc, y_b.reshape(m, -1), y_c.reshape(m, -1), conv_w[l].T,
                         wo[:CONV_WIDTH], wo[CONV_WIDTH:CONV_WIDTH + GDN_WIDTH], wo[CONV_WIDTH + GDN_WIDTH:],
                         row(ln2_g[l]), row(ln2_b[l]), alpha=alpha, seq_len=s)

        h = _ffn_ln(h, ffn2_w_gate[l].astype(BF16), ffn2_w_up[l].astype(BF16), ffn2_w_down[l].astype(BF16),
                    row(ln3_g[l]), row(ln3_b[l]), alpha=alpha)
    return h.reshape(bsz, s, dm)
```

```python
import functools

import jax
import jax.numpy as jnp
from jax import lax
from jax.experimental import pallas as pl
from jax.experimental.pallas import tpu as pltpu

F32 = jnp.float32
BF16 = jnp.bfloat16

HEAD_DIM = 64
CONV_WIDTH = 256
CONV_K = 3
GDN_HEADS = 4
GDN_WIDTH = GDN_HEADS * HEAD_DIM
GDN_CONV_K = 4
GDN_CHUNK = 64
NSA_Q_HEADS = 8
NSA_KV_HEADS = 2
NSA_GROUP = NSA_Q_HEADS // NSA_KV_HEADS
NSA_WIDTH = NSA_Q_HEADS * HEAD_DIM
CMP_LEN = 32
CMP_STRIDE = 16
SEL_BLOCK = 64
SEL_TOPN = 8
WINDOW = 512
FORCE_BONUS = 1e3
LN_EPS = 1e-5
NORM_EPS = 1e-6
NEG = -1e30

V7X_VMEM_LIMIT_BYTES = 56 * 1024 * 1024
LANES = 128
HEAD_SHIFT = HEAD_DIM.bit_length() - 1
SEL_SHIFT = SEL_BLOCK.bit_length() - 1

GDN_COLS = 4 * GDN_WIDTH + LANES
NSA_HEAD_COLS = NSA_GROUP * HEAD_DIM + 6 * HEAD_DIM + LANES


def _layer_norm(r, g, b):
    mu = jnp.mean(r, axis=-1, keepdims=True)
    c = r - mu
    var = jnp.mean(c * c, axis=-1, keepdims=True)
    return c * lax.rsqrt(var + LN_EPS) * g + b


def _const_spec(shape):
    return pl.BlockSpec(shape, lambda *_: (0,) * len(shape), pipeline_mode=pl.Buffered(1))


def _layer_spec(shape, layer, block=None):
    index = (layer,) + tuple(block or (0,) * len(shape))
    return pl.BlockSpec((None,) + tuple(shape), lambda *_: index, pipeline_mode=pl.Buffered(1))


def _ffn_ln_body(x_ref, wg_ref, wu_ref, wd_ref, g_ref, b_ref, o_ref, *, alpha):
    x = x_ref[...]
    xb = x.astype(BF16)
    hg = jnp.dot(xb, wg_ref[...], preferred_element_type=F32)
    hu = jnp.dot(xb, wu_ref[...], preferred_element_type=F32)
    a = (hg * jax.nn.sigmoid(hg) * hu).astype(BF16)
    y = jnp.dot(a, wd_ref[...], preferred_element_type=F32)
    o_ref[...] = _layer_norm(alpha * x + 0.5 * y, g_ref[...], b_ref[...])


def _ffn_ln(x, wg, wu, wd, g, b, *, layer, alpha, tm=512):
    m, d = x.shape
    f = wg.shape[2]
    return pl.pallas_call(
        functools.partial(_ffn_ln_body, alpha=alpha),
        grid=(m // tm,),
        in_specs=[
            pl.BlockSpec((tm, d), lambda i: (i, 0)),
            _layer_spec((d, f), layer), _layer_spec((d, f), layer), _layer_spec((f, d), layer),
            _layer_spec((1, d), layer), _layer_spec((1, d), layer),
        ],
        out_specs=pl.BlockSpec((tm, d), lambda i: (i, 0)),
        out_shape=jax.ShapeDtypeStruct((m, d), F32),
        compiler_params=pltpu.CompilerParams(
            dimension_semantics=("parallel",), vmem_limit_bytes=V7X_VMEM_LIMIT_BYTES),
        name="ffn_ln",
    )(x, wg, wu, wd, g, b)


def _in_proj_body(x_ref, wc_ref, wg_ref, wn_ref, hc_ref, hg_ref, hn_ref):
    xb = x_ref[...].astype(BF16)
    hc_ref[...] = jnp.dot(xb, wc_ref[...], preferred_element_type=F32)
    hg_ref[...] = jnp.dot(xb, wg_ref[...], preferred_element_type=F32)
    hn_ref[...] = jnp.dot(xb, wn_ref[...], preferred_element_type=F32)


def _in_proj(x, wc, wg, wn, *, layer, tm=512):
    m, d = x.shape
    widths = (wc.shape[2], wg.shape[2], wn.shape[2])
    return pl.pallas_call(
        _in_proj_body,
        grid=(m // tm,),
        in_specs=[pl.BlockSpec((tm, d), lambda i: (i, 0))] + [_layer_spec((d, w), layer) for w in widths],
        out_specs=[pl.BlockSpec((tm, w), lambda i: (i, 0)) for w in widths],
        out_shape=[jax.ShapeDtypeStruct((m, w), F32) for w in widths],
        compiler_params=pltpu.CompilerParams(
            dimension_semantics=("parallel",), vmem_limit_bytes=V7X_VMEM_LIMIT_BYTES),
        name="in_proj",
    )(x, wc, wg, wn)


SUBLANES = 8


def _out_proj_ln_body(x_ref, hc_ref, halo_ref, yb_ref, yc_ref, taps_ref, wa_ref, wb_ref, wc_ref, g_ref, b_ref,
                      o_ref, *, alpha, tiles_per_seq):
    c = CONV_WIDTH
    h = hc_ref[...]
    u = h[:, c:2 * c] * h[:, 2 * c:3 * c]
    hh = halo_ref[...]
    at_seq_start = pl.program_id(0) % tiles_per_seq == 0
    u_prev = jnp.where(at_seq_start, 0.0, hh[:, c:2 * c] * hh[:, 2 * c:3 * c])
    ext = jnp.concatenate([u_prev, u], axis=0)
    taps = taps_ref[...]
    k = taps.shape[0]
    conv = u * taps[k - 1:k, :]
    for j in range(k - 1):
        conv = conv + pltpu.roll(ext, k - 1 - j, 0)[SUBLANES:, :] * taps[j:j + 1, :]
    y_a = h[:, 0:c] * conv
    y = jnp.dot(y_a.astype(BF16), wa_ref[...], preferred_element_type=F32)
    y += jnp.dot(yb_ref[...].astype(BF16), wb_ref[...], preferred_element_type=F32)
    y += jnp.dot(yc_ref[...].astype(BF16), wc_ref[...], preferred_element_type=F32)
    o_ref[...] = _layer_norm(alpha * x_ref[...] + y, g_ref[...], b_ref[...])


def _out_proj_ln(x, hc, yb, yc, conv_taps, w_out, g, b, *, layer, alpha, seq_len, tm=512):
    m, d = x.shape
    wa, wb, wc = CONV_WIDTH, yb.shape[1], yc.shape[1]
    assert seq_len % tm == 0 and CONV_K - 1 <= SUBLANES and wa == wb and wc == wa + wb
    row = lambda w: pl.BlockSpec((tm, w), lambda i: (i, 0))
    halo = pl.BlockSpec((SUBLANES, hc.shape[1]), lambda i: (jnp.maximum(i * (tm // SUBLANES) - 1, 0), 0))
    return pl.pallas_call(
        functools.partial(_out_proj_ln_body, alpha=alpha, tiles_per_seq=seq_len // tm),
        grid=(m // tm,),
        in_specs=[row(d), row(hc.shape[1]), halo, row(wb), row(wc),
                  _layer_spec(conv_taps.shape[1:], layer),
                  _layer_spec((wa, d), layer, (0, 0)), _layer_spec((wb, d), layer, (1, 0)),
                  _layer_spec((wc, d), layer, (1, 0)),
                  _layer_spec((1, d), layer), _layer_spec((1, d), layer)],
        out_specs=row(d),
        out_shape=jax.ShapeDtypeStruct((m, d), F32),
        compiler_params=pltpu.CompilerParams(
            dimension_semantics=("parallel",), vmem_limit_bytes=V7X_VMEM_LIMIT_BYTES),
        name="out_proj_ln",
    )(x, hc, hc, yb, yc, conv_taps, w_out, w_out, w_out, g, b)


def _shift_rows(u, s):
    rows = lax.broadcasted_iota(jnp.int32, u.shape, 0)
    return jnp.where(rows >= s, pltpu.roll(u, s, 0), 0.0)


def _causal_dwconv(u, w):
    k = w.shape[0]
    y = u * w[k - 1:k, :]
    for j in range(k - 1):
        y = y + _shift_rows(u, k - 1 - j) * w[j:j + 1, :]
    return y


def _dot(a, b):
    return jnp.dot(a, b, preferred_element_type=F32)


def _dot_nt(a, b):
    return lax.dot_general(a, b, (((1,), (1,)), ((), ())), preferred_element_type=F32)


def _dot_tn(a, b):
    return lax.dot_general(a, b, (((0,), (0,)), ((), ())), preferred_element_type=F32)


def _split_bf16(a, terms):
    parts, rest = [], a
    for _ in range(terms):
        p = rest.astype(BF16)
        parts.append(p)
        rest = rest - p.astype(F32)
    return parts


def _dot_exact_lhs(m_bf16, a, terms):
    return sum(_dot(m_bf16, p) for p in _split_bf16(a, terms))


def _dot_exact_rhs(a, m_bf16, terms):
    return sum(_dot(p, m_bf16) for p in _split_bf16(a, terms))


def _dot_hl(a, b):
    a_hi, a_lo = _split_bf16(a, 2)
    b_hi, b_lo = _split_bf16(b, 2)
    return _dot(a_hi, b_hi) + (_dot(a_hi, b_lo) + _dot(a_lo, b_hi))


def _iota2(shape, dim):
    return lax.broadcasted_iota(jnp.int32, shape, dim)


GDN_INV_BASE = 8
GDN_CHUNKS_PER_STEP = 4


def _block_diag(x, bd_mask):
    return jnp.concatenate([x] * (x.shape[1] // x.shape[0]), axis=0) * bd_mask


def _heads_dot_hl(a, b, bd_mask):
    a_hi, a_lo = _split_bf16(a, 2)
    b_hi, b_lo = _split_bf16(b, 2)
    r = a.shape[0]
    hi = _dot(jnp.concatenate([a_hi, a_lo], axis=0), _block_diag(b_hi, bd_mask))
    return hi[0:r] + hi[r:2 * r] + _dot(a_hi, _block_diag(b_lo, bd_mask))


def _inverse_masks(c, width):
    ri, ci = _iota2((c, width), 0), _iota2((c, width), 1) & (c - 1)
    base = GDN_INV_BASE.bit_length() - 1
    eye = (ci == ri).astype(F32)
    diag = (ri >> base) == (ci >> base)
    levels = [((ri >> (s + 1)) == (ci >> (s + 1))) & ((ri >> s) != (ci >> s))
              for s in range(base, c.bit_length() - 1)]
    return eye, diag, levels


def _unit_lower_inverse(lmats, masks, bd_mask):
    eye, diag, levels = masks
    c = lmats[0].shape[0]
    hdot = lambda a, b: _heads_dot_hl(a, b, bd_mask)
    l0 = [jnp.where(diag, l, 0.0) for l in lmats]
    p = [eye - x for x in l0]
    m = [hdot(x, x) for x in l0]
    yield
    pm = [hdot(jnp.concatenate([pi, mi], axis=0), mi) for pi, mi in zip(p, m)]
    yield
    p = [pi + x[0:c] for pi, x in zip(p, pm)]
    inv = [pi + hdot(pi, x[c:2 * c]) for pi, x in zip(p, pm)]
    yield
    for level in levels:
        t = [hdot(i, jnp.where(level, l, 0.0)) for i, l in zip(inv, lmats)]
        yield
        inv = [i - hdot(ti, i) for i, ti in zip(inv, t)]
        yield
    return inv


def _interleave(*stage_generators):
    live = list(stage_generators)
    while live:
        for gen in list(live):
            try:
                next(gen)
            except StopIteration:
                live.remove(gen)


def _softplus(x):
    return jnp.maximum(x, 0.0) + jnp.log1p(jnp.exp(-jnp.abs(x)))


def _gdn_body(h_ref, taps_ref, alog_ref, dtb_ref, nw_ref, o_ref,
              qn_s, kn_s, kb_s, vb_s, g_s, u_s, w_s, a_s, qd_s, kd_s, egl_s, o_s):
    s = h_ref.shape[1]
    c = GDN_CHUNK
    d = HEAD_DIM
    w = GDN_WIDTH
    nc = s // c

    bd_mask = (_iota2((w, w), 0) >> HEAD_SHIFT == _iota2((w, w), 1) >> HEAD_SHIFT).astype(BF16)

    def head_sum(t):
        return _dot_exact_rhs(t, bd_mask, 2)

    def conv_silu(group):
        cols = slice(group * w, (group + 1) * w)
        y = _causal_dwconv(h_ref[0, :, cols], taps_ref[:, cols])
        return y * jax.nn.sigmoid(y)

    def l2norm(t):
        return t * lax.rsqrt(head_sum(t * t) + NORM_EPS)

    qn_s[...] = l2norm(conv_silu(0)) * (d ** -0.5)
    kn = l2norm(conv_silu(1))
    kn_s[...] = kn
    ab = h_ref[0, :, 4 * w:4 * w + LANES]
    src = _iota2((LANES, w), 0)
    head = _iota2((LANES, w), 1) >> HEAD_SHIFT
    g_small = -jnp.exp(alog_ref[...]) * _softplus(ab + dtb_ref[...])
    g_s[...] = _dot_exact_rhs(g_small, (src == head).astype(BF16), 3)
    beta = _dot_exact_rhs(jax.nn.sigmoid(ab), (src == head + GDN_HEADS).astype(BF16), 3)
    kb_s[...] = kn * beta
    vb_s[...] = conv_silu(2) * beta

    ri = _iota2((c, w), 0)
    ci = _iota2((c, w), 1) & (c - 1)
    tril = ci <= ri
    strict = ci < ri
    upper = (ri <= ci).astype(F32)
    inv_masks = _inverse_masks(c, w)
    tril_b = (_iota2((c, c), 1) <= _iota2((c, c), 0)).astype(BF16)
    ones_b = jnp.ones((c, c), BF16)

    grp = GDN_CHUNKS_PER_STEP
    n_groups = nc // grp

    def chunk_rows(n):
        return pl.ds(n * c if isinstance(n, int) else pl.multiple_of(n * c, c), c)

    def prep_stages(group):
        ns = [group * grp + i for i in range(grp)]
        rows = [chunk_rows(n) for n in ns]
        g = [g_s[r, :] for r in rows]
        gc = [_dot_exact_lhs(tril_b, x, 3) for x in g]
        gct = [_dot_exact_lhs(ones_b, x * upper, 3) for x in g]
        yield
        kn_c = [kn_s[r, :] for r in rows]
        kb_c = [kb_s[r, :] for r in rows]
        qn_c = [qn_s[r, :] for r in rows]
        kq = [_dot_nt(jnp.concatenate([kb, q], axis=0).astype(BF16), _block_diag(k.astype(BF16), bd_mask))
              for kb, q, k in zip(kb_c, qn_c, kn_c)]
        yield
        decay = [jnp.where(tril, jnp.exp(jnp.where(tril, a - b, 0.0)), 0.0) for a, b in zip(gc, gct)]
        lmat = [jnp.where(strict, x[0:c] * dk, 0.0) for x, dk in zip(kq, decay)]
        tinv = yield from _unit_lower_inverse(lmat, inv_masks, bd_mask)
        egc = [jnp.exp(x) for x in gc]
        rhs = [jnp.concatenate([_block_diag(vb_s[r, :].astype(BF16), bd_mask),
                                _block_diag((kb * e).astype(BF16), bd_mask)], axis=1)
               for r, kb, e in zip(rows, kb_c, egc)]
        uw = [_dot(t.astype(BF16), x) for t, x in zip(tinv, rhs)]
        yield
        for i, (n, r) in enumerate(zip(ns, rows)):
            glast = gc[i][c - 1:c, :]
            egl_s[n] = jnp.exp(glast)
            u_s[r, :] = uw[i][:, 0:w]
            w_s[r, :] = uw[i][:, w:2 * w].astype(BF16)
            a_s[r, :] = jnp.where(tril, kq[i][c:2 * c] * decay[i], 0.0).astype(BF16)
            qd_s[r, :] = (qn_c[i] * egc[i]).astype(BF16)
            kd_s[r, :] = (kn_c[i] * jnp.exp(glast - gc[i])).astype(BF16)

    lane_head = _iota2((d, w), 1) >> HEAD_SHIFT

    def scan_stages(group, state):
        for i in range(grp):
            n = group * grp + i
            rows = chunk_rows(n)
            st = state[0]
            st_bd = _block_diag(st.astype(BF16), bd_mask)
            ws_qs = _dot(jnp.concatenate([w_s[rows, :], qd_s[rows, :]], axis=0), st_bd)
            yield
            v_new = u_s[rows, :] - ws_qs[0:c]
            v_b = v_new.astype(BF16)
            o_s[rows, :] = ws_qs[c:2 * c] + _dot(a_s[rows, :], _block_diag(v_b, bd_mask))
            kv = _dot_tn(kd_s[rows, :], v_b)
            upd = jnp.zeros((d, w), F32)
            for h in range(GDN_HEADS):
                upd = upd + jnp.where(lane_head == h, kv[h * d:(h + 1) * d, :], 0.0)
            state[0] = st * egl_s[n] + upd
            yield

    _interleave(prep_stages(0))

    def group_body(group, st):
        state = [st]
        _interleave(prep_stages(group), scan_stages(group - 1, state))
        return state[0]

    state = [lax.fori_loop(1, n_groups, group_body, jnp.zeros((d, w), F32))]
    _interleave(scan_stages(n_groups - 1, state))

    o = o_s[...]
    o = o * lax.rsqrt(head_sum(o * o) * (1.0 / d) + NORM_EPS) * nw_ref[...]
    z = h_ref[0, :, 3 * w:4 * w]
    o_ref[0] = o * (z * jax.nn.sigmoid(z))


def _gated_deltanet(hg, conv_taps, alog_rep, dtb_rep, nw_rep):
    bsz, s, cols = hg.shape
    w = GDN_WIDTH
    assert GDN_CHUNK == HEAD_DIM and cols == GDN_COLS and s % GDN_CHUNK == 0
    return pl.pallas_call(
        _gdn_body,
        grid=(bsz,),
        in_specs=[pl.BlockSpec((1, s, cols), lambda b: (b, 0, 0)),
                  _const_spec(conv_taps.shape),
                  _const_spec((1, LANES)), _const_spec((1, LANES)), _const_spec((1, w))],
        out_specs=pl.BlockSpec((1, s, w), lambda b: (b, 0, 0)),
        out_shape=jax.ShapeDtypeStruct((bsz, s, w), F32),
        scratch_shapes=[pltpu.VMEM((s, w), F32)] * 6 + [pltpu.VMEM((s, w), BF16)] * 4 + [
            pltpu.VMEM((s // GDN_CHUNK, 1, w), F32), pltpu.VMEM((s, w), F32)],
        compiler_params=pltpu.CompilerParams(
            dimension_semantics=("parallel",), vmem_limit_bytes=V7X_VMEM_LIMIT_BYTES),
        name="gated_deltanet",
    )(hg, conv_taps, alog_rep, dtb_rep, nw_rep)


NSA_TQ = 256
NSA_SLC_STEP = 256
NSA_NCMP_PAD = 128
NSA_VPAD = 16


def _softmax_cols(s):
    m = jnp.max(s, axis=0, keepdims=True)
    e = jnp.exp(s - m)
    return e, jnp.sum(e, axis=0, keepdims=True)


def _nsa_body(q_ref, gate_ref, cmp_ref, kv_ref, pe_ref, w1_ref, w2_ref, ovlt_ref, o_ref,
              kc_s, vct_s, ks_s, kw_s, vst_s, vwt_s, oslc_s, owin_s):
    qi = pl.program_id(2)
    tq, d, grp = NSA_TQ, HEAD_DIM, NSA_GROUP
    s_len = kv_ref.shape[1]
    n_sel = s_len // SEL_BLOCK
    seg = CMP_LEN // CMP_STRIDE
    band = WINDOW + tq
    wide = grp * tq

    @pl.when(qi == 0)
    def _():
        def relayout(i, carry):
            r0 = pl.multiple_of(i * LANES, LANES)
            blk = kv_ref[0, pl.ds(r0, LANES), :]
            block_of_row = (r0 + _iota2((LANES, LANES - d), 0)) >> SEL_SHIFT
            one_hot = (block_of_row == _iota2((LANES, LANES - d), 1)).astype(BF16)
            ks_s[pl.ds(r0, LANES), :] = jnp.concatenate([blk[:, 0:d].astype(BF16), one_hot], axis=1)
            kw_s[pl.ds(r0, LANES), :] = blk[:, 2 * d:3 * d].astype(BF16)
            blk_t = blk.T
            ones_row = (_iota2((NSA_VPAD, LANES), 0) == 0).astype(BF16)
            vst_s[:, pl.ds(r0, LANES)] = jnp.concatenate([blk_t[d:2 * d, :].astype(BF16), ones_row], axis=0)
            vwt_s[:, pl.ds(r0, LANES)] = jnp.concatenate([blk_t[3 * d:4 * d, :].astype(BF16), ones_row], axis=0)
            return carry

        lax.fori_loop(0, s_len // LANES, relayout, 0)
        pre = [jnp.zeros((NSA_NCMP_PAD, 2 * LANES), F32) for _ in range(seg)]
        for l in range(CMP_STRIDE):
            t_l = cmp_ref[0, pl.ds(l, NSA_NCMP_PAD, stride=CMP_STRIDE), :]
            for h in range(seg):
                lh = l + h * CMP_STRIDE
                pre[h] = pre[h] + _dot((t_l + pe_ref[lh:lh + 1, :]).astype(BF16), w1_ref[lh])
        hid = pre[0] + pltpu.roll(pre[1], NSA_NCMP_PAD - 1, 0)
        hid = hid * jax.nn.sigmoid(hid)
        kcv = _dot(hid.astype(BF16), w2_ref[...])
        kc_s[...] = kcv[:, 0:d].astype(BF16)
        vct_s[...] = kcv.T[d:2 * d, :].astype(BF16)

    ts = qi * tq
    q_t = (q_ref[0] * (d ** -0.5)).T
    q_t = jnp.concatenate([q_t[g * d:(g + 1) * d, :] for g in range(grp)], axis=1).astype(BF16)
    t_row = ts + (_iota2((1, wide), 1) & (tq - 1))
    t_row1 = t_row[:, 0:tq]

    w0 = pl.multiple_of(jnp.maximum(ts - WINDOW, 0), tq)
    dist = t_row - (w0 + _iota2((band, 1), 0))
    sc_win = jnp.where((dist >= 0) & (dist < WINDOW), _dot(kw_s[pl.ds(w0, band), :], q_t), NEG)
    m_win = jnp.max(sc_win, axis=0, keepdims=True)

    n_col =_iota2((NSA_NCMP_PAD, 1), 0)
    cmp_ok = (n_col * CMP_STRIDE + (CMP_LEN - 1) <= t_row) & (n_col < NSA_NCMP_PAD - 1)
    e, den = _softmax_cols(jnp.where(cmp_ok, _dot(kc_s[...], q_t), NEG))
    p_cmp = jnp.where(t_row >= CMP_LEN - 1, e / den, 0.0)
    o_cmp = _dot(vct_s[...], p_cmp.astype(BF16))

    p_sum = p_cmp[:, 0:tq]
    for g in range(1, grp):
        p_sum = p_sum + p_cmp[:, g * tq:(g + 1) * tq]
    imp = _dot_exact_lhs(ovlt_ref[...], p_sum, 3)
    j_idx = _iota2((n_sel, tq), 0)
    q_blk = t_row1 >> SEL_SHIFT
    forced = (j_idx == 0) | (j_idx == q_blk) | (j_idx == q_blk - 1)
    imp = jnp.where(forced, imp + FORCE_BONUS, imp)
    causal_blk = j_idx <= q_blk
    imp = jnp.where(causal_blk, imp, NEG)
    rank = jnp.zeros((n_sel, tq), F32)
    for jp in range(n_sel):
        row = imp[jp:jp + 1, :]
        beats = (row > imp) | ((row == imp) & (j_idx > jp))
        rank = rank + beats.astype(F32)
    selected = (rank < SEL_TOPN) & causal_blk

    blk0 = ts >> SEL_SHIFT
    sel_bias = jnp.where(selected, 0.0, NEG)
    past_bias = jnp.where(j_idx < blk0, sel_bias, NEG)
    zero_rows = jnp.zeros((LANES - d - n_sel, wide), BF16)

    def with_bias(bias):
        return jnp.concatenate([q_t, jnp.concatenate([bias.astype(BF16)] * grp, axis=1), zero_rows], axis=0)

    sc_diag = jnp.where(ts + _iota2((tq, 1), 0) <= t_row,
                        _dot(ks_s[pl.ds(ts, tq), :], with_bias(sel_bias)), NEG)
    q_past = with_bias(past_bias)

    ch = NSA_SLC_STEP

    def weighted_values(value_t, sc, m):
        return _dot(value_t, jnp.exp(sc - m).astype(BF16))

    def normalised(acc):
        return acc[0:d] / acc[d:d + 1]

    def slc_and_win(nk):
        n_past, n_win = nk // ch, band // ch
        past_sc, win_acc = [], None
        for i in range(max(n_past, n_win)):
            if i < n_past:
                past_sc.append(_dot(ks_s[i * ch:(i + 1) * ch, :], q_past))
            if i < n_win:
                term = weighted_values(vwt_s[:, pl.ds(w0 + i * ch, ch)], sc_win[i * ch:(i + 1) * ch], m_win)
                win_acc = term if win_acc is None else win_acc + term
        owin_s[...] = normalised(win_acc)
        m = functools.reduce(jnp.maximum, [jnp.max(p, axis=0, keepdims=True) for p in past_sc + [sc_diag]])
        acc = None
        for i in range(n_past):
            term = weighted_values(vst_s[:, i * ch:(i + 1) * ch], past_sc[i], m)
            acc = term if acc is None else acc + term
        term = weighted_values(vst_s[:, pl.ds(ts, tq)], sc_diag, m)
        oslc_s[...] = normalised(term if acc is None else acc + term)

    n_var = (s_len - tq + NSA_SLC_STEP - 1) // NSA_SLC_STEP + 1
    variant = (ts + NSA_SLC_STEP - 1) // NSA_SLC_STEP
    for v in range(n_var):
        @pl.when(variant == v)
        def _(v=v):
            slc_and_win(min(v * NSA_SLC_STEP, s_len))
    o_slc, o_win = oslc_s[...], owin_s[...]

    gt = jax.nn.sigmoid(gate_ref[0]).T
    outs = []
    for g in range(grp):
        cols = slice(g * tq, (g + 1) * tq)
        outs.append(gt[g:g + 1, :] * o_cmp[:, cols] + gt[grp + g:grp + g + 1, :] * o_slc[:, cols]
                    + gt[2 * grp + g:2 * grp + g + 1, :] * o_win[:, cols])
    o_ref[0] = jnp.concatenate(outs, axis=0).T


def _nsa_attention(hn, pe, w1, w2, overlap):
    bsz, s, _ = hn.shape
    tq, d = NSA_TQ, HEAD_DIM
    qw = NSA_GROUP * d
    kvw = 4 * d
    assert NSA_HEAD_COLS == qw + 2 * LANES + kvw and qw == kvw and s % tq == 0 and s >= WINDOW + tq
    per_w, per_l = NSA_HEAD_COLS // qw, NSA_HEAD_COLS // LANES
    return pl.pallas_call(
        _nsa_body,
        grid=(bsz, NSA_KV_HEADS, s // tq),
        in_specs=[
            pl.BlockSpec((1, tq, qw), lambda b, h, i: (b, i, per_w * h)),
            pl.BlockSpec((1, tq, LANES), lambda b, h, i: (b, i, per_l * h + qw // LANES)),
            pl.BlockSpec((1, s, LANES), lambda b, h, i: (b, 0, per_l * h + qw // LANES + 1)),
            pl.BlockSpec((1, s, kvw), lambda b, h, i: (b, 0, per_w * h + per_w - 1)),
            _const_spec(pe.shape), _const_spec(w1.shape), _const_spec(w2.shape), _const_spec(overlap.shape),
        ],
        out_specs=pl.BlockSpec((1, tq, qw), lambda b, h, i: (b, i, h)),
        out_shape=jax.ShapeDtypeStruct((bsz, s, NSA_WIDTH), F32),
        scratch_shapes=[pltpu.VMEM((NSA_NCMP_PAD, d), BF16), pltpu.VMEM((d, NSA_NCMP_PAD), BF16),
                        pltpu.VMEM((s, LANES), BF16), pltpu.VMEM((s, d), BF16),
                        pltpu.VMEM((d + NSA_VPAD, s), BF16), pltpu.VMEM((d + NSA_VPAD, s), BF16),
                        pltpu.VMEM((d, NSA_GROUP * tq), F32), pltpu.VMEM((d, NSA_GROUP * tq), F32)],
        compiler_params=pltpu.CompilerParams(
            dimension_semantics=("parallel", "parallel", "arbitrary"),
            vmem_limit_bytes=V7X_VMEM_LIMIT_BYTES),
        name="nsa_attention",
    )(hn, hn, hn, hn, pe, w1, w2, overlap)


def _prep_w_in(w):
    w = w.astype(BF16)
    lead = w.shape[:-1]
    sizes = (3 * CONV_WIDTH, 3 * GDN_WIDTH, GDN_WIDTH, GDN_HEADS, GDN_HEADS,
             NSA_WIDTH, 6 * NSA_KV_HEADS * HEAD_DIM, 3 * NSA_Q_HEADS)
    parts, start = [], 0
    for n in sizes:
        parts.append(w[..., start:start + n])
        start += n
    wconv, gqkv, gz, ga, gb, nq, nkv, ngate = parts
    pad = lambda n: jnp.zeros(lead + (n,), w.dtype)
    wgdn = jnp.concatenate([gqkv, gz, ga, gb, pad(LANES - 2 * GDN_HEADS)], axis=-1)
    nkv = nkv.reshape(lead + (6, NSA_KV_HEADS, HEAD_DIM))
    ngate = ngate.reshape(lead + (NSA_KV_HEADS, NSA_GROUP, 3))
    per_head = []
    for h in range(NSA_KV_HEADS):
        qw = NSA_GROUP * HEAD_DIM
        gates = jnp.swapaxes(ngate[..., h, :, :], -1, -2).reshape(lead + (3 * NSA_GROUP,))
        per_head += [nq[..., h * qw:(h + 1) * qw], gates, pad(LANES - 3 * NSA_GROUP),
                     nkv[..., h, :].reshape(lead + (6 * HEAD_DIM,))]
    wnsa = jnp.concatenate(per_head, axis=-1)
    return wconv, wgdn, wnsa


def _prep_cmp_weights(pe_k, pe_v, k_w1, k_w2, v_w1, v_w2):
    d, hid = HEAD_DIM, k_w1.shape[1]
    pe = jnp.concatenate([pe_k, pe_v], axis=1)
    k1 = k_w1.reshape(CMP_LEN, d, hid)
    v1 = v_w1.reshape(CMP_LEN, d, hid)
    z1 = jnp.zeros_like(k1)
    w1 = jnp.concatenate([jnp.concatenate([k1, z1], axis=2),
                          jnp.concatenate([z1, v1], axis=2)], axis=1)
    z2 = jnp.zeros_like(k_w2)
    w2 = jnp.concatenate([jnp.concatenate([k_w2, z2], axis=1),
                          jnp.concatenate([z2, v_w2], axis=1)], axis=0)
    return pe, w1.astype(BF16), w2.astype(BF16)


def _overlap_matrix(s):
    n_cmp = (s - CMP_LEN) // CMP_STRIDE + 1
    t = jnp.arange(s)
    starts = jnp.arange(NSA_NCMP_PAD) * CMP_STRIDE
    cmp_tok = (t[None, :] >= starts[:, None]) & (t[None, :] < starts[:, None] + CMP_LEN)
    cmp_tok = cmp_tok & (jnp.arange(NSA_NCMP_PAD) < n_cmp)[:, None]
    sel_tok = (t[None, :] // SEL_BLOCK) == jnp.arange(s // SEL_BLOCK)[:, None]
    return ((sel_tok.astype(F32) @ cmp_tok.astype(F32).T) / CMP_LEN).astype(BF16)


def kernel(x, ffn1_w_gate, ffn1_w_up, ffn1_w_down, ln1_g, ln1_b, w_in, conv_w, gdn_conv_w, gdn_a_log, gdn_dt_bias, gdn_norm_w, cmp_pe_k, cmp_pe_v, cmp_k_w1, cmp_k_w2, cmp_v_w1, cmp_v_w2, w_out, ln2_g, ln2_b, ffn2_w_gate, ffn2_w_up, ffn2_w_down, ln3_g, ln3_b):
    bsz, s, dm = x.shape
    depth = w_in.shape[0]
    alpha = (2 * depth) ** 0.25
    m = bsz * s
    overlap = _overlap_matrix(s)
    row = lambda v: v.reshape(1, -1)
    rows = lambda v: v[:, None, :]
    bf = lambda v: v.astype(BF16)
    ffn1, ffn2 = (bf(ffn1_w_gate), bf(ffn1_w_up), bf(ffn1_w_down)), (bf(ffn2_w_gate), bf(ffn2_w_up), bf(ffn2_w_down))
    wconv, wgdn, wnsa = _prep_w_in(w_in)
    wo = bf(w_out)
    conv_taps = jnp.swapaxes(conv_w, 1, 2)
    h = x.reshape(m, dm)
    for l in range(depth):
        h = _ffn_ln(h, *ffn1, rows(ln1_g), rows(ln1_b), layer=l, alpha=alpha)

        hc, hg, hn = _in_proj(h, wconv, wgdn, wnsa, layer=l)
        y_b = _gated_deltanet(hg.reshape(bsz, s, -1), gdn_conv_w[l].T,
                              row(jnp.pad(gdn_a_log[l], (0, LANES - GDN_HEADS))),
                              row(jnp.pad(gdn_dt_bias[l], (0, LANES - GDN_HEADS))),
                              row(jnp.tile(gdn_norm_w[l], GDN_HEADS)))
        pe, w1, w2 = _prep_cmp_weights(cmp_pe_k[l], cmp_pe_v[l], cmp_k_w1[l], cmp_k_w2[l], cmp_v_w1[l], cmp_v_w2[l])
        y_c = _nsa_attention(hn.reshape(bsz, s, -1), pe, w1, w2, overlap)
        h = _out_proj_ln(h, hc, y_b.reshape(m, -1), y_c.reshape(m, -1), conv_taps, wo,
                         rows(ln2_g), rows(ln2_b), layer=l, alpha=alpha, seq_len=s)

        h = _ffn_ln(h, *ffn2, rows(ln3_g), rows(ln3_b), layer=l, alpha=alpha)
    return h.reshape(bsz, s, dm)
```

```python
import functools

import jax
import jax.numpy as jnp
from jax import lax
from jax.experimental import pallas as pl
from jax.experimental.pallas import tpu as pltpu

F32 = jnp.float32
BF16 = jnp.bfloat16

HEAD_DIM = 64
CONV_WIDTH = 256
CONV_K = 3
GDN_HEADS = 4
GDN_WIDTH = GDN_HEADS * HEAD_DIM
GDN_CONV_K = 4
GDN_CHUNK = 64
NSA_Q_HEADS = 8
NSA_KV_HEADS = 2
NSA_GROUP = NSA_Q_HEADS // NSA_KV_HEADS
NSA_WIDTH = NSA_Q_HEADS * HEAD_DIM
CMP_LEN = 32
CMP_STRIDE = 16
SEL_BLOCK = 64
SEL_TOPN = 8
WINDOW = 512
FORCE_BONUS = 1e3
LN_EPS = 1e-5
NORM_EPS = 1e-6
NEG = -1e30

V7X_VMEM_LIMIT_BYTES = 56 * 1024 * 1024
LANES = 128
HEAD_SHIFT = HEAD_DIM.bit_length() - 1
SEL_SHIFT = SEL_BLOCK.bit_length() - 1

GDN_COLS = 4 * GDN_WIDTH + LANES


def _layer_norm(r, g, b):
    mu = jnp.mean(r, axis=-1, keepdims=True)
    c = r - mu
    var = jnp.mean(c * c, axis=-1, keepdims=True)
    return c * lax.rsqrt(var + LN_EPS) * g + b


def _const_spec(shape):
    return pl.BlockSpec(shape, lambda *_: (0,) * len(shape), pipeline_mode=pl.Buffered(1))


def _layer_spec(shape, layer, block=None):
    index = (layer,) + tuple(block or (0,) * len(shape))
    return pl.BlockSpec((None,) + tuple(shape), lambda *_: index, pipeline_mode=pl.Buffered(1))


def _ffn_ln_body(x_ref, wg_ref, wu_ref, wd_ref, g_ref, b_ref, o_ref, *, alpha):
    x = x_ref[...]
    xb = x.astype(BF16)
    hg = jnp.dot(xb, wg_ref[...], preferred_element_type=F32)
    hu = jnp.dot(xb, wu_ref[...], preferred_element_type=F32)
    a = (hg * jax.nn.sigmoid(hg) * hu).astype(BF16)
    y = jnp.dot(a, wd_ref[...], preferred_element_type=F32)
    o_ref[...] = _layer_norm(alpha * x + 0.5 * y, g_ref[...], b_ref[...])


def _ffn_ln(x, wg, wu, wd, g, b, *, layer, alpha, tm=512):
    m, d = x.shape
    f = wg.shape[2]
    return pl.pallas_call(
        functools.partial(_ffn_ln_body, alpha=alpha),
        grid=(m // tm,),
        in_specs=[
            pl.BlockSpec((tm, d), lambda i: (i, 0)),
            _layer_spec((d, f), layer), _layer_spec((d, f), layer), _layer_spec((f, d), layer),
            _layer_spec((1, d), layer), _layer_spec((1, d), layer),
        ],
        out_specs=pl.BlockSpec((tm, d), lambda i: (i, 0)),
        out_shape=jax.ShapeDtypeStruct((m, d), F32),
        compiler_params=pltpu.CompilerParams(
            dimension_semantics=("parallel",), vmem_limit_bytes=V7X_VMEM_LIMIT_BYTES),
        name="ffn_ln",
    )(x, wg, wu, wd, g, b)


def _in_proj_body(x_ref, *refs):
    n = len(refs) // 2
    xb = x_ref[...].astype(BF16)
    for w_ref, o_ref in zip(refs[:n], refs[n:]):
        o_ref[...] = jnp.dot(xb, w_ref[...], preferred_element_type=F32).astype(o_ref.dtype)


def _in_proj(x, weights, out_dtypes, *, layer, tm=512):
    m, d = x.shape
    widths = [w.shape[2] for w in weights]
    return pl.pallas_call(
        _in_proj_body,
        grid=(m // tm,),
        in_specs=[pl.BlockSpec((tm, d), lambda i: (i, 0))] + [_layer_spec((d, w), layer) for w in widths],
        out_specs=[pl.BlockSpec((tm, w), lambda i: (i, 0)) for w in widths],
        out_shape=[jax.ShapeDtypeStruct((m, w), dt) for w, dt in zip(widths, out_dtypes)],
        compiler_params=pltpu.CompilerParams(
            dimension_semantics=("parallel",), vmem_limit_bytes=V7X_VMEM_LIMIT_BYTES),
        name="in_proj",
    )(x, *weights)


SUBLANES = 8


def _out_proj_ln_body(x_ref, hc_ref, halo_ref, yb_ref, yc_ref, taps_ref, wa_ref, wb_ref, wc_ref, g_ref, b_ref,
                      o_ref, *, alpha, tiles_per_seq):
    c = CONV_WIDTH
    h = hc_ref[...]
    u = h[:, c:2 * c] * h[:, 2 * c:3 * c]
    hh = halo_ref[...]
    at_seq_start = pl.program_id(0) % tiles_per_seq == 0
    u_prev = jnp.where(at_seq_start, 0.0, hh[:, c:2 * c] * hh[:, 2 * c:3 * c])
    ext = jnp.concatenate([u_prev, u], axis=0)
    taps = taps_ref[...]
    k = taps.shape[0]
    conv = u * taps[k - 1:k, :]
    for j in range(k - 1):
        conv = conv + pltpu.roll(ext, k - 1 - j, 0)[SUBLANES:, :] * taps[j:j + 1, :]
    y_a = h[:, 0:c] * conv
    y = jnp.dot(y_a.astype(BF16), wa_ref[...], preferred_element_type=F32)
    y += jnp.dot(yb_ref[...].astype(BF16), wb_ref[...], preferred_element_type=F32)
    y += jnp.dot(yc_ref[...].astype(BF16), wc_ref[...], preferred_element_type=F32)
    o_ref[...] = _layer_norm(alpha * x_ref[...] + y, g_ref[...], b_ref[...])


def _out_proj_ln(x, hc, yb, yc, conv_taps, w_out, g, b, *, layer, alpha, seq_len, tm=512):
    m, d = x.shape
    wa, wb, wc = CONV_WIDTH, yb.shape[1], yc.shape[1]
    assert seq_len % tm == 0 and CONV_K - 1 <= SUBLANES and wa == wb and wc == wa + wb
    row = lambda w: pl.BlockSpec((tm, w), lambda i: (i, 0))
    halo = pl.BlockSpec((SUBLANES, hc.shape[1]), lambda i: (jnp.maximum(i * (tm // SUBLANES) - 1, 0), 0))
    return pl.pallas_call(
        functools.partial(_out_proj_ln_body, alpha=alpha, tiles_per_seq=seq_len // tm),
        grid=(m // tm,),
        in_specs=[row(d), row(hc.shape[1]), halo, row(wb), row(wc),
                  _layer_spec(conv_taps.shape[1:], layer),
                  _layer_spec((wa, d), layer, (0, 0)), _layer_spec((wb, d), layer, (1, 0)),
                  _layer_spec((wc, d), layer, (1, 0)),
                  _layer_spec((1, d), layer), _layer_spec((1, d), layer)],
        out_specs=row(d),
        out_shape=jax.ShapeDtypeStruct((m, d), F32),
        compiler_params=pltpu.CompilerParams(
            dimension_semantics=("parallel",), vmem_limit_bytes=V7X_VMEM_LIMIT_BYTES),
        name="out_proj_ln",
    )(x, hc, hc, yb, yc, conv_taps, w_out, w_out, w_out, g, b)


def _shift_rows(u, s):
    rows = lax.broadcasted_iota(jnp.int32, u.shape, 0)
    return jnp.where(rows >= s, pltpu.roll(u, s, 0), 0.0)


def _causal_dwconv(u, w):
    k = w.shape[0]
    y = u * w[k - 1:k, :]
    for j in range(k - 1):
        y = y + _shift_rows(u, k - 1 - j) * w[j:j + 1, :]
    return y


def _dot(a, b):
    return jnp.dot(a, b, preferred_element_type=F32)


def _dot_nt(a, b):
    return lax.dot_general(a, b, (((1,), (1,)), ((), ())), preferred_element_type=F32)


def _dot_tn(a, b):
    return lax.dot_general(a, b, (((0,), (0,)), ((), ())), preferred_element_type=F32)


def _split_bf16(a, terms):
    parts, rest = [], a
    for _ in range(terms):
        p = rest.astype(BF16)
        parts.append(p)
        rest = rest - p.astype(F32)
    return parts


def _dot_exact_lhs(m_bf16, a, terms):
    return sum(_dot(m_bf16, p) for p in _split_bf16(a, terms))


def _dot_exact_rhs(a, m_bf16, terms):
    return sum(_dot(p, m_bf16) for p in _split_bf16(a, terms))


def _dot_hl(a, b):
    a_hi, a_lo = _split_bf16(a, 2)
    b_hi, b_lo = _split_bf16(b, 2)
    return _dot(a_hi, b_hi) + (_dot(a_hi, b_lo) + _dot(a_lo, b_hi))


def _iota2(shape, dim):
    return lax.broadcasted_iota(jnp.int32, shape, dim)


GDN_INV_BASE = 8
GDN_CHUNKS_PER_STEP = 4


def _block_diag(x, bd_mask):
    return jnp.concatenate([x] * (x.shape[1] // x.shape[0]), axis=0) * bd_mask


def _heads_dot_hl(a, b, bd_mask):
    a_hi, a_lo = _split_bf16(a, 2)
    b_hi, b_lo = _split_bf16(b, 2)
    r = a.shape[0]
    hi = _dot(jnp.concatenate([a_hi, a_lo], axis=0), _block_diag(b_hi, bd_mask))
    return hi[0:r] + hi[r:2 * r] + _dot(a_hi, _block_diag(b_lo, bd_mask))


def _inverse_masks(c, width):
    ri, ci = _iota2((c, width), 0), _iota2((c, width), 1) & (c - 1)
    base = GDN_INV_BASE.bit_length() - 1
    eye = (ci == ri).astype(F32)
    diag = (ri >> base) == (ci >> base)
    levels = [((ri >> (s + 1)) == (ci >> (s + 1))) & ((ri >> s) != (ci >> s))
              for s in range(base, c.bit_length() - 1)]
    return eye, diag, levels


def _unit_lower_inverse(lmats, masks, bd_mask):
    eye, diag, levels = masks
    c = lmats[0].shape[0]
    hdot = lambda a, b: _heads_dot_hl(a, b, bd_mask)
    l0 = [jnp.where(diag, l, 0.0) for l in lmats]
    p = [eye - x for x in l0]
    m = [hdot(x, x) for x in l0]
    yield
    pm = [hdot(jnp.concatenate([pi, mi], axis=0), mi) for pi, mi in zip(p, m)]
    yield
    p = [pi + x[0:c] for pi, x in zip(p, pm)]
    inv = [pi + hdot(pi, x[c:2 * c]) for pi, x in zip(p, pm)]
    yield
    for level in levels:
        t = [hdot(i, jnp.where(level, l, 0.0)) for i, l in zip(inv, lmats)]
        yield
        inv = [i - hdot(ti, i) for i, ti in zip(inv, t)]
        yield
    return inv


def _interleave(*stage_generators):
    live = list(stage_generators)
    while live:
        for gen in list(live):
            try:
                next(gen)
            except StopIteration:
                live.remove(gen)


def _softplus(x):
    return jnp.maximum(x, 0.0) + jnp.log1p(jnp.exp(-jnp.abs(x)))


def _gdn_body(h_ref, taps_ref, alog_ref, dtb_ref, nw_ref, o_ref,
              qn_s, kn_s, kb_s, vb_s, g_s, u_s, w_s, a_s, qd_s, kd_s, egl_s, o_s):
    s = h_ref.shape[1]
    c = GDN_CHUNK
    d = HEAD_DIM
    w = GDN_WIDTH
    nc = s // c

    bd_mask = (_iota2((w, w), 0) >> HEAD_SHIFT == _iota2((w, w), 1) >> HEAD_SHIFT).astype(BF16)

    def head_sum(t):
        return _dot_exact_rhs(t, bd_mask, 2)

    def conv_silu(group):
        cols = slice(group * w, (group + 1) * w)
        y = _causal_dwconv(h_ref[0, :, cols], taps_ref[:, cols])
        return y * jax.nn.sigmoid(y)

    def l2norm(t):
        return t * lax.rsqrt(head_sum(t * t) + NORM_EPS)

    qn_s[...] = l2norm(conv_silu(0)) * (d ** -0.5)
    kn = l2norm(conv_silu(1))
    kn_s[...] = kn
    ab = h_ref[0, :, 4 * w:4 * w + LANES]
    src = _iota2((LANES, w), 0)
    head = _iota2((LANES, w), 1) >> HEAD_SHIFT
    g_small = -jnp.exp(alog_ref[...]) * _softplus(ab + dtb_ref[...])
    g_s[...] = _dot_exact_rhs(g_small, (src == head).astype(BF16), 3)
    beta = _dot_exact_rhs(jax.nn.sigmoid(ab), (src == head + GDN_HEADS).astype(BF16), 3)
    kb_s[...] = kn * beta
    vb_s[...] = conv_silu(2) * beta

    ri = _iota2((c, w), 0)
    ci = _iota2((c, w), 1) & (c - 1)
    tril = ci <= ri
    strict = ci < ri
    upper = (ri <= ci).astype(F32)
    inv_masks = _inverse_masks(c, w)
    tril_b = (_iota2((c, c), 1) <= _iota2((c, c), 0)).astype(BF16)
    ones_b = jnp.ones((c, c), BF16)

    grp = GDN_CHUNKS_PER_STEP
    n_groups = nc // grp

    def chunk_rows(n):
        return pl.ds(n * c if isinstance(n, int) else pl.multiple_of(n * c, c), c)

    def prep_stages(group):
        ns = [group * grp + i for i in range(grp)]
        rows = [chunk_rows(n) for n in ns]
        g = [g_s[r, :] for r in rows]
        gc = [_dot_exact_lhs(tril_b, x, 3) for x in g]
        gct = [_dot_exact_lhs(ones_b, x * upper, 3) for x in g]
        yield
        kn_c = [kn_s[r, :] for r in rows]
        kb_c = [kb_s[r, :] for r in rows]
        qn_c = [qn_s[r, :] for r in rows]
        kq = [_dot_nt(jnp.concatenate([kb, q], axis=0).astype(BF16), _block_diag(k.astype(BF16), bd_mask))
              for kb, q, k in zip(kb_c, qn_c, kn_c)]
        yield
        decay = [jnp.where(tril, jnp.exp(jnp.where(tril, a - b, 0.0)), 0.0) for a, b in zip(gc, gct)]
        lmat = [jnp.where(strict, x[0:c] * dk, 0.0) for x, dk in zip(kq, decay)]
        tinv = yield from _unit_lower_inverse(lmat, inv_masks, bd_mask)
        egc = [jnp.exp(x) for x in gc]
        rhs = [jnp.concatenate([_block_diag(vb_s[r, :].astype(BF16), bd_mask),
                                _block_diag((kb * e).astype(BF16), bd_mask)], axis=1)
               for r, kb, e in zip(rows, kb_c, egc)]
        uw = [_dot(t.astype(BF16), x) for t, x in zip(tinv, rhs)]
        yield
        for i, (n, r) in enumerate(zip(ns, rows)):
            glast = gc[i][c - 1:c, :]
            egl_s[n] = jnp.exp(glast)
            u_s[r, :] = uw[i][:, 0:w]
            w_s[r, :] = uw[i][:, w:2 * w].astype(BF16)
            a_s[r, :] = jnp.where(tril, kq[i][c:2 * c] * decay[i], 0.0).astype(BF16)
            qd_s[r, :] = (qn_c[i] * egc[i]).astype(BF16)
            kd_s[r, :] = (kn_c[i] * jnp.exp(glast - gc[i])).astype(BF16)

    lane_head = _iota2((d, w), 1) >> HEAD_SHIFT

    def scan_stages(group, state):
        for i in range(grp):
            n = group * grp + i
            rows = chunk_rows(n)
            st = state[0]
            st_bd = _block_diag(st.astype(BF16), bd_mask)
            ws_qs = _dot(jnp.concatenate([w_s[rows, :], qd_s[rows, :]], axis=0), st_bd)
            yield
            v_new = u_s[rows, :] - ws_qs[0:c]
            v_b = v_new.astype(BF16)
            o_s[rows, :] = ws_qs[c:2 * c] + _dot(a_s[rows, :], _block_diag(v_b, bd_mask))
            kv = _dot_tn(kd_s[rows, :], v_b)
            upd = jnp.zeros((d, w), F32)
            for h in range(GDN_HEADS):
                upd = upd + jnp.where(lane_head == h, kv[h * d:(h + 1) * d, :], 0.0)
            state[0] = st * egl_s[n] + upd
            yield

    _interleave(prep_stages(0))

    def group_body(group, st):
        state = [st]
        _interleave(prep_stages(group), scan_stages(group - 1, state))
        return state[0]

    state = [lax.fori_loop(1, n_groups, group_body, jnp.zeros((d, w), F32))]
    _interleave(scan_stages(n_groups - 1, state))

    o = o_s[...]
    o = o * lax.rsqrt(head_sum(o * o) * (1.0 / d) + NORM_EPS) * nw_ref[...]
    z = h_ref[0, :, 3 * w:4 * w]
    o_ref[0] = (o * (z * jax.nn.sigmoid(z))).astype(o_ref.dtype)


def _gated_deltanet(hg, conv_taps, alog_rep, dtb_rep, nw_rep):
    bsz, s, cols = hg.shape
    w = GDN_WIDTH
    assert GDN_CHUNK == HEAD_DIM and cols == GDN_COLS and s % GDN_CHUNK == 0
    return pl.pallas_call(
        _gdn_body,
        grid=(bsz,),
        in_specs=[pl.BlockSpec((1, s, cols), lambda b: (b, 0, 0)),
                  _const_spec(conv_taps.shape),
                  _const_spec((1, LANES)), _const_spec((1, LANES)), _const_spec((1, w))],
        out_specs=pl.BlockSpec((1, s, w), lambda b: (b, 0, 0)),
        out_shape=jax.ShapeDtypeStruct((bsz, s, w), BF16),
        scratch_shapes=[pltpu.VMEM((s, w), F32)] * 6 + [pltpu.VMEM((s, w), BF16)] * 4 + [
            pltpu.VMEM((s // GDN_CHUNK, 1, w), F32), pltpu.VMEM((s, w), F32)],
        compiler_params=pltpu.CompilerParams(
            dimension_semantics=("parallel",), vmem_limit_bytes=V7X_VMEM_LIMIT_BYTES),
        name="gated_deltanet",
    )(hg, conv_taps, alog_rep, dtb_rep, nw_rep)


NSA_TQ = 256
NSA_SLC_STEP = 256
NSA_NCMP_PAD = 128
NSA_VPAD = 16


def _softmax_cols(s):
    m = jnp.max(s, axis=0, keepdims=True)
    e = jnp.exp(s - m)
    return e, jnp.sum(e, axis=0, keepdims=True)


def _nsa_body(q_ref, gate_ref, cmp_ref, kv_ref, pe_ref, w1_ref, w2_ref, ovlt_ref, o_ref,
              kc_s, vct_s, ks_s, kw_s, vst_s, vwt_s, oslc_s, owin_s):
    qi = pl.program_id(2)
    tq, d, grp = NSA_TQ, HEAD_DIM, NSA_GROUP
    s_len = kv_ref.shape[1]
    n_sel = s_len // SEL_BLOCK
    seg = CMP_LEN // CMP_STRIDE
    band = WINDOW + tq
    wide = grp * tq

    @pl.when(qi == 0)
    def _():
        def relayout(i, carry):
            r0 = pl.multiple_of(i * LANES, LANES)
            blk = kv_ref[0, pl.ds(r0, LANES), :]
            block_of_row = (r0 + _iota2((LANES, LANES - d), 0)) >> SEL_SHIFT
            one_hot = (block_of_row == _iota2((LANES, LANES - d), 1)).astype(BF16)
            ks_s[pl.ds(r0, LANES), :] = jnp.concatenate([blk[:, 0:d].astype(BF16), one_hot], axis=1)
            kw_s[pl.ds(r0, LANES), :] = blk[:, 2 * d:3 * d].astype(BF16)
            blk_t = blk.astype(F32).T
            ones_row = (_iota2((NSA_VPAD, LANES), 0) == 0).astype(BF16)
            vst_s[:, pl.ds(r0, LANES)] = jnp.concatenate([blk_t[d:2 * d, :].astype(BF16), ones_row], axis=0)
            vwt_s[:, pl.ds(r0, LANES)] = jnp.concatenate([blk_t[3 * d:4 * d, :].astype(BF16), ones_row], axis=0)
            return carry

        lax.fori_loop(0, s_len // LANES, relayout, 0)
        pre = [jnp.zeros((NSA_NCMP_PAD, 2 * LANES), F32) for _ in range(seg)]
        for l in range(CMP_STRIDE):
            t_l = cmp_ref[0, pl.ds(l, NSA_NCMP_PAD, stride=CMP_STRIDE), :]
            for h in range(seg):
                lh = l + h * CMP_STRIDE
                pre[h] = pre[h] + _dot((t_l + pe_ref[lh:lh + 1, :]).astype(BF16), w1_ref[lh])
        hid = pre[0] + pltpu.roll(pre[1], NSA_NCMP_PAD - 1, 0)
        hid = hid * jax.nn.sigmoid(hid)
        kcv = _dot(hid.astype(BF16), w2_ref[...])
        kc_s[...] = kcv[:, 0:d].astype(BF16)
        vct_s[...] = kcv.T[d:2 * d, :].astype(BF16)

    ts = qi * tq
    q_t = (q_ref[0].astype(F32) * (d ** -0.5)).T
    q_t = jnp.concatenate([q_t[g * d:(g + 1) * d, :] for g in range(grp)], axis=1).astype(BF16)
    t_row = ts + (_iota2((1, wide), 1) & (tq - 1))
    t_row1 = t_row[:, 0:tq]

    w0 = pl.multiple_of(jnp.maximum(ts - WINDOW, 0), tq)
    dist = t_row - (w0 + _iota2((band, 1), 0))
    sc_win = jnp.where((dist >= 0) & (dist < WINDOW), _dot(kw_s[pl.ds(w0, band), :], q_t), NEG)
    m_win = jnp.max(sc_win, axis=0, keepdims=True)

    n_col =_iota2((NSA_NCMP_PAD, 1), 0)
    cmp_ok = (n_col * CMP_STRIDE + (CMP_LEN - 1) <= t_row) & (n_col < NSA_NCMP_PAD - 1)
    e, den = _softmax_cols(jnp.where(cmp_ok, _dot(kc_s[...], q_t), NEG))
    p_cmp = jnp.where(t_row >= CMP_LEN - 1, e / den, 0.0)
    o_cmp = _dot(vct_s[...], p_cmp.astype(BF16))

    p_sum = p_cmp[:, 0:tq]
    for g in range(1, grp):
        p_sum = p_sum + p_cmp[:, g * tq:(g + 1) * tq]
    imp = _dot_exact_lhs(ovlt_ref[...], p_sum, 3)
    j_idx = _iota2((n_sel, tq), 0)
    q_blk = t_row1 >> SEL_SHIFT
    forced = (j_idx == 0) | (j_idx == q_blk) | (j_idx == q_blk - 1)
    imp = jnp.where(forced, imp + FORCE_BONUS, imp)
    causal_blk = j_idx <= q_blk
    imp = jnp.where(causal_blk, imp, NEG)
    rank = jnp.zeros((n_sel, tq), F32)
    for jp in range(n_sel):
        row = imp[jp:jp + 1, :]
        beats = (row > imp) | ((row == imp) & (j_idx > jp))
        rank = rank + beats.astype(F32)
    selected = (rank < SEL_TOPN) & causal_blk

    blk0 = ts >> SEL_SHIFT
    sel_bias = jnp.where(selected, 0.0, NEG)
    past_bias = jnp.where(j_idx < blk0, sel_bias, NEG)
    zero_rows = jnp.zeros((LANES - d - n_sel, wide), BF16)

    def with_bias(bias):
        return jnp.concatenate([q_t, jnp.concatenate([bias.astype(BF16)] * grp, axis=1), zero_rows], axis=0)

    sc_diag = jnp.where(ts + _iota2((tq, 1), 0) <= t_row,
                        _dot(ks_s[pl.ds(ts, tq), :], with_bias(sel_bias)), NEG)
    q_past = with_bias(past_bias)

    ch = NSA_SLC_STEP

    def weighted_values(value_t, sc, m):
        return _dot(value_t, jnp.exp(sc - m).astype(BF16))

    def normalised(acc):
        return acc[0:d] / acc[d:d + 1]

    def slc_and_win(nk):
        n_past, n_win = nk // ch, band // ch
        past_sc, win_acc = [], None
        for i in range(max(n_past, n_win)):
            if i < n_past:
                past_sc.append(_dot(ks_s[i * ch:(i + 1) * ch, :], q_past))
            if i < n_win:
                term = weighted_values(vwt_s[:, pl.ds(w0 + i * ch, ch)], sc_win[i * ch:(i + 1) * ch], m_win)
                win_acc = term if win_acc is None else win_acc + term
        owin_s[...] = normalised(win_acc)
        m = functools.reduce(jnp.maximum, [jnp.max(p, axis=0, keepdims=True) for p in past_sc + [sc_diag]])
        acc = None
        for i in range(n_past):
            term = weighted_values(vst_s[:, i * ch:(i + 1) * ch], past_sc[i], m)
            acc = term if acc is None else acc + term
        term = weighted_values(vst_s[:, pl.ds(ts, tq)], sc_diag, m)
        oslc_s[...] = normalised(term if acc is None else acc + term)

    n_var = (s_len - tq + NSA_SLC_STEP - 1) // NSA_SLC_STEP + 1
    variant = (ts + NSA_SLC_STEP - 1) // NSA_SLC_STEP
    for v in range(n_var):
        @pl.when(variant == v)
        def _(v=v):
            slc_and_win(min(v * NSA_SLC_STEP, s_len))
    o_slc, o_win = oslc_s[...], owin_s[...]

    gt = jax.nn.sigmoid(gate_ref[0]).T
    outs = []
    for g in range(grp):
        cols = slice(g * tq, (g + 1) * tq)
        outs.append(gt[g:g + 1, :] * o_cmp[:, cols] + gt[grp + g:grp + g + 1, :] * o_slc[:, cols]
                    + gt[2 * grp + g:2 * grp + g + 1, :] * o_win[:, cols])
    o_ref[0] = jnp.concatenate(outs, axis=0).T.astype(o_ref.dtype)


def _nsa_attention(h_mm, h_f32, pe, w1, w2, overlap):
    bsz, s, _ = h_mm.shape
    tq, d = NSA_TQ, HEAD_DIM
    qw = NSA_GROUP * d
    assert h_mm.shape[2] == NSA_KV_HEADS * 2 * qw and h_f32.shape[2] == NSA_KV_HEADS * 2 * LANES
    assert s % tq == 0 and s >= WINDOW + tq and (WINDOW + tq) % NSA_SLC_STEP == 0 and d + s // SEL_BLOCK <= LANES
    return pl.pallas_call(
        _nsa_body,
        grid=(bsz, NSA_KV_HEADS, s // tq),
        in_specs=[
            pl.BlockSpec((1, tq, qw), lambda b, h, i: (b, i, 2 * h)),
            pl.BlockSpec((1, tq, LANES), lambda b, h, i: (b, i, 2 * h)),
            pl.BlockSpec((1, s, LANES), lambda b, h, i: (b, 0, 2 * h + 1)),
            pl.BlockSpec((1, s, qw), lambda b, h, i: (b, 0, 2 * h + 1)),
            _const_spec(pe.shape), _const_spec(w1.shape), _const_spec(w2.shape), _const_spec(overlap.shape),
        ],
        out_specs=pl.BlockSpec((1, tq, qw), lambda b, h, i: (b, i, h)),
        out_shape=jax.ShapeDtypeStruct((bsz, s, NSA_WIDTH), BF16),
        scratch_shapes=[pltpu.VMEM((NSA_NCMP_PAD, d), BF16), pltpu.VMEM((d, NSA_NCMP_PAD), BF16),
                        pltpu.VMEM((s, LANES), BF16), pltpu.VMEM((s, d), BF16),
                        pltpu.VMEM((d + NSA_VPAD, s), BF16), pltpu.VMEM((d + NSA_VPAD, s), BF16),
                        pltpu.VMEM((d, NSA_GROUP * tq), F32), pltpu.VMEM((d, NSA_GROUP * tq), F32)],
        compiler_params=pltpu.CompilerParams(
            dimension_semantics=("parallel", "parallel", "arbitrary"),
            vmem_limit_bytes=V7X_VMEM_LIMIT_BYTES),
        name="nsa_attention",
    )(h_mm, h_f32, h_f32, h_mm, pe, w1, w2, overlap)


def _prep_w_in(w):
    w = w.astype(BF16)
    lead = w.shape[:-1]
    sizes = (3 * CONV_WIDTH, 3 * GDN_WIDTH, GDN_WIDTH, GDN_HEADS, GDN_HEADS,
             NSA_WIDTH, 6 * NSA_KV_HEADS * HEAD_DIM, 3 * NSA_Q_HEADS)
    parts, start = [], 0
    for n in sizes:
        parts.append(w[..., start:start + n])
        start += n
    wconv, gqkv, gz, ga, gb, nq, nkv, ngate = parts
    pad = lambda n: jnp.zeros(lead + (n,), w.dtype)
    wgdn = jnp.concatenate([gqkv, gz, ga, gb, pad(LANES - 2 * GDN_HEADS)], axis=-1)
    nkv = nkv.reshape(lead + (6, NSA_KV_HEADS, HEAD_DIM))
    ngate = ngate.reshape(lead + (NSA_KV_HEADS, NSA_GROUP, 3))
    mm_cols, f32_cols = [], []
    for h in range(NSA_KV_HEADS):
        qw = NSA_GROUP * HEAD_DIM
        gates = jnp.swapaxes(ngate[..., h, :, :], -1, -2).reshape(lead + (3 * NSA_GROUP,))
        kv = nkv[..., h, :].reshape(lead + (6 * HEAD_DIM,))
        mm_cols += [nq[..., h * qw:(h + 1) * qw], kv[..., 2 * HEAD_DIM:]]
        f32_cols += [gates, pad(LANES - 3 * NSA_GROUP), kv[..., :2 * HEAD_DIM]]
    return wconv, wgdn, jnp.concatenate(mm_cols, axis=-1), jnp.concatenate(f32_cols, axis=-1)


def _prep_cmp_weights(pe_k, pe_v, k_w1, k_w2, v_w1, v_w2):
    d, hid = HEAD_DIM, k_w1.shape[1]
    pe = jnp.concatenate([pe_k, pe_v], axis=1)
    k1 = k_w1.reshape(CMP_LEN, d, hid)
    v1 = v_w1.reshape(CMP_LEN, d, hid)
    z1 = jnp.zeros_like(k1)
    w1 = jnp.concatenate([jnp.concatenate([k1, z1], axis=2),
                          jnp.concatenate([z1, v1], axis=2)], axis=1)
    z2 = jnp.zeros_like(k_w2)
    w2 = jnp.concatenate([jnp.concatenate([k_w2, z2], axis=1),
                          jnp.concatenate([z2, v_w2], axis=1)], axis=0)
    return pe, w1.astype(BF16), w2.astype(BF16)


def _overlap_matrix(s):
    n_cmp = (s - CMP_LEN) // CMP_STRIDE + 1
    t = jnp.arange(s)
    starts = jnp.arange(NSA_NCMP_PAD) * CMP_STRIDE
    cmp_tok = (t[None, :] >= starts[:, None]) & (t[None, :] < starts[:, None] + CMP_LEN)
    cmp_tok = cmp_tok & (jnp.arange(NSA_NCMP_PAD) < n_cmp)[:, None]
    sel_tok = (t[None, :] // SEL_BLOCK) == jnp.arange(s // SEL_BLOCK)[:, None]
    return ((sel_tok.astype(F32) @ cmp_tok.astype(F32).T) / CMP_LEN).astype(BF16)


def kernel(x, ffn1_w_gate, ffn1_w_up, ffn1_w_down, ln1_g, ln1_b, w_in, conv_w, gdn_conv_w, gdn_a_log, gdn_dt_bias, gdn_norm_w, cmp_pe_k, cmp_pe_v, cmp_k_w1, cmp_k_w2, cmp_v_w1, cmp_v_w2, w_out, ln2_g, ln2_b, ffn2_w_gate, ffn2_w_up, ffn2_w_down, ln3_g, ln3_b):
    bsz, s, dm = x.shape
    depth = w_in.shape[0]
    alpha = (2 * depth) ** 0.25
    m = bsz * s
    overlap = _overlap_matrix(s)
    row = lambda v: v.reshape(1, -1)
    rows = lambda v: v[:, None, :]
    bf = lambda v: v.astype(BF16)
    ffn1, ffn2 = (bf(ffn1_w_gate), bf(ffn1_w_up), bf(ffn1_w_down)), (bf(ffn2_w_gate), bf(ffn2_w_up), bf(ffn2_w_down))
    w_in_groups = _prep_w_in(w_in)
    wo = bf(w_out)
    conv_taps = jnp.swapaxes(conv_w, 1, 2)
    h = x.reshape(m, dm)
    for l in range(depth):
        h = _ffn_ln(h, *ffn1, rows(ln1_g), rows(ln1_b), layer=l, alpha=alpha)

        hc, hg, hn_mm, hn_f32 = _in_proj(h, w_in_groups, (F32, F32, BF16, F32), layer=l)
        y_b = _gated_deltanet(hg.reshape(bsz, s, -1), gdn_conv_w[l].T,
                              row(jnp.pad(gdn_a_log[l], (0, LANES - GDN_HEADS))),
                              row(jnp.pad(gdn_dt_bias[l], (0, LANES - GDN_HEADS))),
                              row(jnp.tile(gdn_norm_w[l], GDN_HEADS)))
        pe, w1, w2 = _prep_cmp_weights(cmp_pe_k[l], cmp_pe_v[l], cmp_k_w1[l], cmp_k_w2[l], cmp_v_w1[l], cmp_v_w2[l])
        y_c = _nsa_attention(hn_mm.reshape(bsz, s, -1), hn_f32.reshape(bsz, s, -1), pe, w1, w2, overlap)
        h = _out_proj_ln(h, hc, y_b.reshape(m, -1), y_c.reshape(m, -1), conv_taps, wo,
                         rows(ln2_g), rows(ln2_b), layer=l, alpha=alpha, seq_len=s)

        h = _ffn_ln(h, *ffn2, rows(ln3_g), rows(ln3_b), layer=l, alpha=alpha)
    return h.reshape(bsz, s, dm)
```

```python
import functools

import jax
import jax.numpy as jnp
from jax import lax
from jax.experimental import pallas as pl
from jax.experimental.pallas import tpu as pltpu

F32 = jnp.float32
BF16 = jnp.bfloat16

HEAD_DIM = 64
CONV_WIDTH = 256
CONV_K = 3
GDN_HEADS = 4
GDN_WIDTH = GDN_HEADS * HEAD_DIM
GDN_CONV_K = 4
GDN_CHUNK = 64
NSA_Q_HEADS = 8
NSA_KV_HEADS = 2
NSA_GROUP = NSA_Q_HEADS // NSA_KV_HEADS
NSA_WIDTH = NSA_Q_HEADS * HEAD_DIM
CMP_LEN = 32
CMP_STRIDE = 16
SEL_BLOCK = 64
SEL_TOPN = 8
WINDOW = 512
FORCE_BONUS = 1e3
LN_EPS = 1e-5
NORM_EPS = 1e-6
NEG = -1e30

V7X_VMEM_LIMIT_BYTES = 56 * 1024 * 1024
LANES = 128
HEAD_SHIFT = HEAD_DIM.bit_length() - 1
SEL_SHIFT = SEL_BLOCK.bit_length() - 1

GDN_COLS = 4 * GDN_WIDTH + LANES


def _layer_norm(r, g, b):
    mu = jnp.mean(r, axis=-1, keepdims=True)
    c = r - mu
    var = jnp.mean(c * c, axis=-1, keepdims=True)
    return c * lax.rsqrt(var + LN_EPS) * g + b


def _const_spec(shape):
    return pl.BlockSpec(shape, lambda *_: (0,) * len(shape), pipeline_mode=pl.Buffered(1))


def _layer_spec(shape, layer, block=None):
    index = (layer,) + tuple(block or (0,) * len(shape))
    return pl.BlockSpec((None,) + tuple(shape), lambda *_: index, pipeline_mode=pl.Buffered(1))


def _ffn_ln_body(x_ref, wg_ref, wu_ref, wd_ref, g_ref, b_ref, o_ref, *, alpha):
    x = x_ref[...]
    xb = x.astype(BF16)
    hg = jnp.dot(xb, wg_ref[...], preferred_element_type=F32)
    hu = jnp.dot(xb, wu_ref[...], preferred_element_type=F32)
    a = (hg * jax.nn.sigmoid(hg) * hu).astype(BF16)
    y = jnp.dot(a, wd_ref[...], preferred_element_type=F32)
    o_ref[...] = _layer_norm(alpha * x + 0.5 * y, g_ref[...], b_ref[...])


def _ffn_ln(x, wg, wu, wd, g, b, *, layer, alpha, tm=512):
    m, d = x.shape
    f = wg.shape[2]
    return pl.pallas_call(
        functools.partial(_ffn_ln_body, alpha=alpha),
        grid=(m // tm,),
        in_specs=[
            pl.BlockSpec((tm, d), lambda i: (i, 0)),
            _layer_spec((d, f), layer), _layer_spec((d, f), layer), _layer_spec((f, d), layer),
            _layer_spec((1, d), layer), _layer_spec((1, d), layer),
        ],
        out_specs=pl.BlockSpec((tm, d), lambda i: (i, 0)),
        out_shape=jax.ShapeDtypeStruct((m, d), F32),
        compiler_params=pltpu.CompilerParams(
            dimension_semantics=("parallel",), vmem_limit_bytes=V7X_VMEM_LIMIT_BYTES),
        name="ffn_ln",
    )(x, wg, wu, wd, g, b)


def _in_proj_body(x_ref, *refs):
    n = len(refs) // 2
    xb = x_ref[...].astype(BF16)
    for w_ref, o_ref in zip(refs[:n], refs[n:]):
        o_ref[...] = jnp.dot(xb, w_ref[...], preferred_element_type=F32).astype(o_ref.dtype)


def _in_proj(x, weights, out_dtypes, *, layer, tm=512):
    m, d = x.shape
    widths = [w.shape[2] for w in weights]
    return pl.pallas_call(
        _in_proj_body,
        grid=(m // tm,),
        in_specs=[pl.BlockSpec((tm, d), lambda i: (i, 0))] + [_layer_spec((d, w), layer) for w in widths],
        out_specs=[pl.BlockSpec((tm, w), lambda i: (i, 0)) for w in widths],
        out_shape=[jax.ShapeDtypeStruct((m, w), dt) for w, dt in zip(widths, out_dtypes)],
        compiler_params=pltpu.CompilerParams(
            dimension_semantics=("parallel",), vmem_limit_bytes=V7X_VMEM_LIMIT_BYTES),
        name="in_proj",
    )(x, *weights)


SUBLANES = 8


def _out_proj_ln_body(x_ref, hc_ref, halo_ref, yb_ref, yc_ref, taps_ref, wa_ref, wb_ref, wc_ref, g_ref, b_ref,
                      o_ref, *, alpha, tiles_per_seq):
    c = CONV_WIDTH
    h = hc_ref[...]
    u = h[:, c:2 * c] * h[:, 2 * c:3 * c]
    hh = halo_ref[...]
    at_seq_start = pl.program_id(0) % tiles_per_seq == 0
    u_prev = jnp.where(at_seq_start, 0.0, hh[:, c:2 * c] * hh[:, 2 * c:3 * c])
    y_a = h[:, 0:c] * _causal_dwconv(u_prev, u, taps_ref[...])
    y = jnp.dot(y_a.astype(BF16), wa_ref[...], preferred_element_type=F32)
    y += jnp.dot(yb_ref[...].astype(BF16), wb_ref[...], preferred_element_type=F32)
    y += jnp.dot(yc_ref[...].astype(BF16), wc_ref[...], preferred_element_type=F32)
    o_ref[...] = _layer_norm(alpha * x_ref[...] + y, g_ref[...], b_ref[...])


def _out_proj_ln(x, hc, yb, yc, conv_taps, w_out, g, b, *, layer, alpha, seq_len, tm=512):
    m, d = x.shape
    wa, wb, wc = CONV_WIDTH, yb.shape[1], yc.shape[1]
    assert seq_len % tm == 0 and CONV_K - 1 <= SUBLANES and wa == wb and wc == wa + wb
    row = lambda w: pl.BlockSpec((tm, w), lambda i: (i, 0))
    halo = pl.BlockSpec((SUBLANES, hc.shape[1]), lambda i: (jnp.maximum(i * (tm // SUBLANES) - 1, 0), 0))
    return pl.pallas_call(
        functools.partial(_out_proj_ln_body, alpha=alpha, tiles_per_seq=seq_len // tm),
        grid=(m // tm,),
        in_specs=[row(d), row(hc.shape[1]), halo, row(wb), row(wc),
                  _layer_spec(conv_taps.shape[1:], layer),
                  _layer_spec((wa, d), layer, (0, 0)), _layer_spec((wb, d), layer, (1, 0)),
                  _layer_spec((wc, d), layer, (1, 0)),
                  _layer_spec((1, d), layer), _layer_spec((1, d), layer)],
        out_specs=row(d),
        out_shape=jax.ShapeDtypeStruct((m, d), F32),
        compiler_params=pltpu.CompilerParams(
            dimension_semantics=("parallel",), vmem_limit_bytes=V7X_VMEM_LIMIT_BYTES),
        name="out_proj_ln",
    )(x, hc, hc, yb, yc, conv_taps, w_out, w_out, w_out, g, b)


def _causal_dwconv(prev, body, taps):
    ext = jnp.concatenate([prev, body], axis=0)
    k = taps.shape[0]
    y = body * taps[k - 1:k, :]
    for j in range(k - 1):
        y = y + pltpu.roll(ext, k - 1 - j, 0)[SUBLANES:, :] * taps[j:j + 1, :]
    return y


def _dot(a, b):
    return jnp.dot(a, b, preferred_element_type=F32)


def _dot_nt(a, b):
    return lax.dot_general(a, b, (((1,), (1,)), ((), ())), preferred_element_type=F32)


def _dot_tn(a, b):
    return lax.dot_general(a, b, (((0,), (0,)), ((), ())), preferred_element_type=F32)


def _split_bf16(a, terms):
    parts, rest = [], a
    for _ in range(terms):
        p = rest.astype(BF16)
        parts.append(p)
        rest = rest - p.astype(F32)
    return parts


def _dot_exact_lhs(m_bf16, a, terms):
    return sum(_dot(m_bf16, p) for p in _split_bf16(a, terms))


def _dot_exact_rhs(a, m_bf16, terms):
    return sum(_dot(p, m_bf16) for p in _split_bf16(a, terms))


def _dot_hl(a, b):
    a_hi, a_lo = _split_bf16(a, 2)
    b_hi, b_lo = _split_bf16(b, 2)
    return _dot(a_hi, b_hi) + (_dot(a_hi, b_lo) + _dot(a_lo, b_hi))


def _iota2(shape, dim):
    return lax.broadcasted_iota(jnp.int32, shape, dim)


GDN_INV_BASE = 8
GDN_CHUNKS_PER_STEP = 4


def _block_diag(x, bd_mask):
    return jnp.concatenate([x] * (x.shape[1] // x.shape[0]), axis=0) * bd_mask


def _heads_dot_hl(a, b, bd_mask):
    a_hi, a_lo = _split_bf16(a, 2)
    b_hi, b_lo = _split_bf16(b, 2)
    r = a.shape[0]
    hi = _dot(jnp.concatenate([a_hi, a_lo], axis=0), _block_diag(b_hi, bd_mask))
    return hi[0:r] + hi[r:2 * r] + _dot(a_hi, _block_diag(b_lo, bd_mask))


def _inverse_masks(c, width):
    ri, ci = _iota2((c, width), 0), _iota2((c, width), 1) & (c - 1)
    base = GDN_INV_BASE.bit_length() - 1
    eye = (ci == ri).astype(F32)
    diag = (ri >> base) == (ci >> base)
    levels = [((ri >> (s + 1)) == (ci >> (s + 1))) & ((ri >> s) != (ci >> s))
              for s in range(base, c.bit_length() - 1)]
    return eye, diag, levels


def _unit_lower_inverse(lmats, masks, bd_mask):
    eye, diag, levels = masks
    c = lmats[0].shape[0]
    hdot = lambda a, b: _heads_dot_hl(a, b, bd_mask)
    l0 = [jnp.where(diag, l, 0.0) for l in lmats]
    p = [eye - x for x in l0]
    m = [hdot(x, x) for x in l0]
    yield
    pm = [hdot(jnp.concatenate([pi, mi], axis=0), mi) for pi, mi in zip(p, m)]
    yield
    p = [pi + x[0:c] for pi, x in zip(p, pm)]
    inv = [pi + hdot(pi, x[c:2 * c]) for pi, x in zip(p, pm)]
    yield
    for level in levels:
        t = [hdot(i, jnp.where(level, l, 0.0)) for i, l in zip(inv, lmats)]
        yield
        inv = [i - hdot(ti, i) for i, ti in zip(inv, t)]
        yield
    return inv


def _interleave(*stage_generators):
    live = list(stage_generators)
    while live:
        for gen in list(live):
            try:
                next(gen)
            except StopIteration:
                live.remove(gen)


def _softplus(x):
    return jnp.maximum(x, 0.0) + jnp.log1p(jnp.exp(-jnp.abs(x)))


def _gdn_body(h_ref, taps_ref, alog_ref, dtb_ref, nw_ref, o_ref,
              qn_s, kn_s, kb_s, vb_s, g_s, u_s, w_s, a_s, qd_s, kd_s, egl_s, o_s):
    s = h_ref.shape[1]
    c = GDN_CHUNK
    d = HEAD_DIM
    w = GDN_WIDTH
    nc = s // c

    bd_mask = (_iota2((w, w), 0) >> HEAD_SHIFT == _iota2((w, w), 1) >> HEAD_SHIFT).astype(BF16)

    def head_sum(t):
        return _dot_exact_rhs(t, bd_mask, 2)

    def l2norm(t):
        return t * lax.rsqrt(head_sum(t * t) + NORM_EPS)

    def conv_silu(slab):
        cols = slice(slab * w, (slab + 1) * w)
        y = _causal_dwconv(jnp.zeros((SUBLANES, w), F32), h_ref[0, :, cols], taps_ref[:, cols])
        return y * jax.nn.sigmoid(y)

    qn_s[...] = l2norm(conv_silu(0)) * (d ** -0.5)
    kn = l2norm(conv_silu(1))
    kn_s[...] = kn
    ab = h_ref[0, :, 4 * w:4 * w + LANES]
    src = _iota2((LANES, w), 0)
    head = _iota2((LANES, w), 1) >> HEAD_SHIFT
    g_small = -jnp.exp(alog_ref[...]) * _softplus(ab + dtb_ref[...])
    g_s[...] = _dot_exact_rhs(g_small, (src == head).astype(BF16), 3)
    beta = _dot_exact_rhs(jax.nn.sigmoid(ab), (src == head + GDN_HEADS).astype(BF16), 3)
    kb_s[...] = kn * beta
    vb_s[...] = conv_silu(2) * beta

    ri = _iota2((c, w), 0)
    ci = _iota2((c, w), 1) & (c - 1)
    tril = ci <= ri
    strict = ci < ri
    upper = (ri <= ci).astype(F32)
    inv_masks = _inverse_masks(c, w)
    tril_b = (_iota2((c, c), 1) <= _iota2((c, c), 0)).astype(BF16)
    ones_b = jnp.ones((c, c), BF16)

    grp = GDN_CHUNKS_PER_STEP
    n_groups = nc // grp

    def chunk_rows(n):
        return pl.ds(n * c if isinstance(n, int) else pl.multiple_of(n * c, c), c)

    def prep_stages(group):
        ns = [group * grp + i for i in range(grp)]
        rows = [chunk_rows(n) for n in ns]
        g = [g_s[r, :] for r in rows]
        gc = [_dot_exact_lhs(tril_b, x, 3) for x in g]
        gct = [_dot_exact_lhs(ones_b, x * upper, 3) for x in g]
        yield
        kn_c = [kn_s[r, :] for r in rows]
        kb_c = [kb_s[r, :] for r in rows]
        qn_c = [qn_s[r, :] for r in rows]
        kq = [_dot_nt(jnp.concatenate([kb, q], axis=0).astype(BF16), _block_diag(k.astype(BF16), bd_mask))
              for kb, q, k in zip(kb_c, qn_c, kn_c)]
        yield
        decay = [jnp.where(tril, jnp.exp(jnp.where(tril, a - b, 0.0)), 0.0) for a, b in zip(gc, gct)]
        lmat = [jnp.where(strict, x[0:c] * dk, 0.0) for x, dk in zip(kq, decay)]
        tinv = yield from _unit_lower_inverse(lmat, inv_masks, bd_mask)
        egc = [jnp.exp(x) for x in gc]
        rhs = [jnp.concatenate([_block_diag(vb_s[r, :].astype(BF16), bd_mask),
                                _block_diag((kb * e).astype(BF16), bd_mask)], axis=1)
               for r, kb, e in zip(rows, kb_c, egc)]
        uw = [_dot(t.astype(BF16), x) for t, x in zip(tinv, rhs)]
        yield
        for i, (n, r) in enumerate(zip(ns, rows)):
            glast = gc[i][c - 1:c, :]
            egl_s[n] = jnp.exp(glast)
            u_s[r, :] = uw[i][:, 0:w]
            w_s[r, :] = uw[i][:, w:2 * w].astype(BF16)
            a_s[r, :] = jnp.where(tril, kq[i][c:2 * c] * decay[i], 0.0).astype(BF16)
            qd_s[r, :] = (qn_c[i] * egc[i]).astype(BF16)
            kd_s[r, :] = (kn_c[i] * jnp.exp(glast - gc[i])).astype(BF16)

    lane_head = _iota2((d, w), 1) >> HEAD_SHIFT

    def scan_stages(group, state):
        for i in range(grp):
            n = group * grp + i
            rows = chunk_rows(n)
            st = state[0]
            st_bd = _block_diag(st.astype(BF16), bd_mask)
            ws_qs = _dot(jnp.concatenate([w_s[rows, :], qd_s[rows, :]], axis=0), st_bd)
            yield
            v_new = u_s[rows, :] - ws_qs[0:c]
            v_b = v_new.astype(BF16)
            o_s[rows, :] = ws_qs[c:2 * c] + _dot(a_s[rows, :], _block_diag(v_b, bd_mask))
            kv = _dot_tn(kd_s[rows, :], v_b)
            upd = jnp.zeros((d, w), F32)
            for h in range(GDN_HEADS):
                upd = upd + jnp.where(lane_head == h, kv[h * d:(h + 1) * d, :], 0.0)
            state[0] = st * egl_s[n] + upd
            yield

    _interleave(prep_stages(0))

    def group_body(group, st):
        state = [st]
        _interleave(prep_stages(group), scan_stages(group - 1, state))
        return state[0]

    state = [lax.fori_loop(1, n_groups, group_body, jnp.zeros((d, w), F32))]
    _interleave(scan_stages(n_groups - 1, state))

    o = o_s[...]
    o = o * lax.rsqrt(head_sum(o * o) * (1.0 / d) + NORM_EPS) * nw_ref[...]
    z = h_ref[0, :, 3 * w:4 * w]
    o_ref[0] = (o * (z * jax.nn.sigmoid(z))).astype(o_ref.dtype)


def _gated_deltanet(hg, conv_taps, alog_rep, dtb_rep, nw_rep):
    bsz, s, cols = hg.shape
    w = GDN_WIDTH
    assert GDN_CHUNK == HEAD_DIM and cols == GDN_COLS and s % (GDN_CHUNK * GDN_CHUNKS_PER_STEP) == 0
    return pl.pallas_call(
        _gdn_body,
        grid=(bsz,),
        in_specs=[pl.BlockSpec((1, s, cols), lambda b: (b, 0, 0)),
                  _const_spec(conv_taps.shape),
                  _const_spec((1, LANES)), _const_spec((1, LANES)), _const_spec((1, w))],
        out_specs=pl.BlockSpec((1, s, w), lambda b: (b, 0, 0)),
        out_shape=jax.ShapeDtypeStruct((bsz, s, w), BF16),
        scratch_shapes=[pltpu.VMEM((s, w), F32)] * 6 + [pltpu.VMEM((s, w), BF16)] * 4 + [
            pltpu.VMEM((s // GDN_CHUNK, 1, w), F32), pltpu.VMEM((s, w), F32)],
        compiler_params=pltpu.CompilerParams(
            dimension_semantics=("parallel",), vmem_limit_bytes=V7X_VMEM_LIMIT_BYTES),
        name="gated_deltanet",
    )(hg, conv_taps, alog_rep, dtb_rep, nw_rep)


NSA_TQ = 256
NSA_SLC_STEP = 256
NSA_NCMP_PAD = 128
NSA_VPAD = 16


def _softmax_cols(s):
    m = jnp.max(s, axis=0, keepdims=True)
    e = jnp.exp(s - m)
    return e, jnp.sum(e, axis=0, keepdims=True)


def _nsa_body(mm_q_ref, f32_q_ref, cmp0_ref, cmp1_ref, mm_kv_ref, pe_ref, w1_ref, w2_ref, ovlt_ref, o_ref,
              kc_all, vct_all, ks_all, kw_all, vst_all, vwt_all, oslc_s, owin_s):
    for hk, cmp_ref in enumerate((cmp0_ref, cmp1_ref)):
        _nsa_head(hk, mm_q_ref, f32_q_ref, cmp_ref, mm_kv_ref, pe_ref, w1_ref, w2_ref, ovlt_ref, o_ref,
                  kc_all.at[hk], vct_all.at[hk], ks_all.at[hk], kw_all.at[hk], vst_all.at[hk], vwt_all.at[hk],
                  oslc_s, owin_s)


def _nsa_head(hk, mm_q_ref, f32_q_ref, cmp_ref, mm_kv_ref, pe_ref, w1_ref, w2_ref, ovlt_ref, o_ref,
              kc_s, vct_s, ks_s, kw_s, vst_s, vwt_s, oslc_s, owin_s):
    qi = pl.program_id(1)
    tq, d, grp = NSA_TQ, HEAD_DIM, NSA_GROUP
    s_len = mm_kv_ref.shape[1]
    n_sel = s_len // SEL_BLOCK
    seg = CMP_LEN // CMP_STRIDE
    band = WINDOW + tq
    wide = grp * tq
    qw = grp * d
    q_cols = slice(hk * 2 * qw, hk * 2 * qw + qw)
    kv_cols = slice(hk * 2 * qw + qw, (hk + 1) * 2 * qw)
    gate_cols = slice(hk * 2 * LANES, hk * 2 * LANES + LANES)
    out_cols = slice(hk * qw, (hk + 1) * qw)

    @pl.when(qi == 0)
    def _():
        def relayout(i, carry):
            r0 = pl.multiple_of(i * LANES, LANES)
            blk = mm_kv_ref[0, pl.ds(r0, LANES), kv_cols]
            block_of_row = (r0 + _iota2((LANES, LANES - d), 0)) >> SEL_SHIFT
            one_hot = (block_of_row == _iota2((LANES, LANES - d), 1)).astype(BF16)
            ks_s[pl.ds(r0, LANES), :] = jnp.concatenate([blk[:, 0:d].astype(BF16), one_hot], axis=1)
            kw_s[pl.ds(r0, LANES), :] = blk[:, 2 * d:3 * d].astype(BF16)
            blk_t = blk.astype(F32).T
            ones_row = (_iota2((NSA_VPAD, LANES), 0) == 0).astype(BF16)
            vst_s[:, pl.ds(r0, LANES)] = jnp.concatenate([blk_t[d:2 * d, :].astype(BF16), ones_row], axis=0)
            vwt_s[:, pl.ds(r0, LANES)] = jnp.concatenate([blk_t[3 * d:4 * d, :].astype(BF16), ones_row], axis=0)
            return carry

        lax.fori_loop(0, s_len // LANES, relayout, 0)
        pre = [jnp.zeros((NSA_NCMP_PAD, 2 * LANES), F32) for _ in range(seg)]
        for l in range(CMP_STRIDE):
            t_l = cmp_ref[0, pl.ds(l, NSA_NCMP_PAD, stride=CMP_STRIDE), :]
            for h in range(seg):
                lh = l + h * CMP_STRIDE
                pre[h] = pre[h] + _dot((t_l + pe_ref[lh:lh + 1, :]).astype(BF16), w1_ref[lh])
        hid = pre[0] + pltpu.roll(pre[1], NSA_NCMP_PAD - 1, 0)
        hid = hid * jax.nn.sigmoid(hid)
        kcv = _dot(hid.astype(BF16), w2_ref[...])
        kc_s[...] = kcv[:, 0:d].astype(BF16)
        vct_s[...] = kcv.T[d:2 * d, :].astype(BF16)

    ts = qi * tq
    q_t = (mm_q_ref[0, :, q_cols].astype(F32) * (d ** -0.5)).T
    q_t = jnp.concatenate([q_t[g * d:(g + 1) * d, :] for g in range(grp)], axis=1).astype(BF16)
    t_row = ts + (_iota2((1, wide), 1) & (tq - 1))
    t_row1 = t_row[:, 0:tq]

    w0 = pl.multiple_of(jnp.maximum(ts - WINDOW, 0), tq)
    dist = t_row - (w0 + _iota2((band, 1), 0))
    sc_win = jnp.where((dist >= 0) & (dist < WINDOW), _dot(kw_s[pl.ds(w0, band), :], q_t), NEG)
    m_win = jnp.max(sc_win, axis=0, keepdims=True)

    n_col =_iota2((NSA_NCMP_PAD, 1), 0)
    cmp_ok = (n_col * CMP_STRIDE + (CMP_LEN - 1) <= t_row) & (n_col < NSA_NCMP_PAD - 1)
    e, den = _softmax_cols(jnp.where(cmp_ok, _dot(kc_s[...], q_t), NEG))
    p_cmp = jnp.where(t_row >= CMP_LEN - 1, e / den, 0.0)
    o_cmp = _dot(vct_s[...], p_cmp.astype(BF16))

    p_sum = p_cmp[:, 0:tq]
    for g in range(1, grp):
        p_sum = p_sum + p_cmp[:, g * tq:(g + 1) * tq]
    imp = _dot_exact_lhs(ovlt_ref[...], p_sum, 3)
    j_idx = _iota2((n_sel, tq), 0)
    q_blk = t_row1 >> SEL_SHIFT
    forced = (j_idx == 0) | (j_idx == q_blk) | (j_idx == q_blk - 1)
    imp = jnp.where(forced, imp + FORCE_BONUS, imp)
    causal_blk = j_idx <= q_blk
    imp = jnp.where(causal_blk, imp, NEG)
    rank = jnp.zeros((n_sel, tq), F32)
    for jp in range(n_sel):
        row = imp[jp:jp + 1, :]
        beats = (row > imp) | ((row == imp) & (j_idx > jp))
        rank = rank + beats.astype(F32)
    selected = (rank < SEL_TOPN) & causal_blk

    blk0 = ts >> SEL_SHIFT
    sel_bias = jnp.where(selected, 0.0, NEG)
    past_bias = jnp.where(j_idx < blk0, sel_bias, NEG)
    zero_rows = jnp.zeros((LANES - d - n_sel, wide), BF16)

    def with_bias(bias):
        return jnp.concatenate([q_t, jnp.concatenate([bias.astype(BF16)] * grp, axis=1), zero_rows], axis=0)

    sc_diag = jnp.where(ts + _iota2((tq, 1), 0) <= t_row,
                        _dot(ks_s[pl.ds(ts, tq), :], with_bias(sel_bias)), NEG)
    q_past = with_bias(past_bias)

    ch = NSA_SLC_STEP

    def weighted_values(value_t, sc, m):
        return _dot(value_t, jnp.exp(sc - m).astype(BF16))

    def normalised(acc):
        return acc[0:d] / acc[d:d + 1]

    def slc_and_win(nk):
        n_past, n_win = nk // ch, band // ch
        past_sc, win_acc = [], None
        for i in range(max(n_past, n_win)):
            if i < n_past:
                past_sc.append(_dot(ks_s[i * ch:(i + 1) * ch, :], q_past))
            if i < n_win:
                term = weighted_values(vwt_s[:, pl.ds(w0 + i * ch, ch)], sc_win[i * ch:(i + 1) * ch], m_win)
                win_acc = term if win_acc is None else win_acc + term
        owin_s[...] = normalised(win_acc)
        m = functools.reduce(jnp.maximum, [jnp.max(p, axis=0, keepdims=True) for p in past_sc + [sc_diag]])
        acc = None
        for i in range(n_past):
            term = weighted_values(vst_s[:, i * ch:(i + 1) * ch], past_sc[i], m)
            acc = term if acc is None else acc + term
        term = weighted_values(vst_s[:, pl.ds(ts, tq)], sc_diag, m)
        oslc_s[...] = normalised(term if acc is None else acc + term)

    n_var = (s_len - tq + NSA_SLC_STEP - 1) // NSA_SLC_STEP + 1
    variant = (ts + NSA_SLC_STEP - 1) // NSA_SLC_STEP
    for v in range(n_var):
        @pl.when(variant == v)
        def _(v=v):
            slc_and_win(min(v * NSA_SLC_STEP, s_len))
    o_slc, o_win = oslc_s[...], owin_s[...]

    gt = jax.nn.sigmoid(f32_q_ref[0, :, gate_cols]).T
    outs = []
    for g in range(grp):
        cols = slice(g * tq, (g + 1) * tq)
        outs.append(gt[g:g + 1, :] * o_cmp[:, cols] + gt[grp + g:grp + g + 1, :] * o_slc[:, cols]
                    + gt[2 * grp + g:2 * grp + g + 1, :] * o_win[:, cols])
    o_ref[0, :, out_cols] = jnp.concatenate(outs, axis=0).T.astype(o_ref.dtype)


def _nsa_attention(h_mm, h_f32, pe, w1, w2, overlap):
    bsz, s, _ = h_mm.shape
    tq, d = NSA_TQ, HEAD_DIM
    qw = NSA_GROUP * d
    assert h_mm.shape[2] == NSA_KV_HEADS * 2 * qw and h_f32.shape[2] == NSA_KV_HEADS * 2 * LANES
    assert s % tq == 0 and s >= WINDOW + tq and (WINDOW + tq) % NSA_SLC_STEP == 0 and d + s // SEL_BLOCK <= LANES
    hk = NSA_KV_HEADS
    assert hk == 2
    mm_w, f32_w = h_mm.shape[2], h_f32.shape[2]
    return pl.pallas_call(
        _nsa_body,
        grid=(bsz, s // tq),
        in_specs=[
            pl.BlockSpec((1, tq, mm_w), lambda b, i: (b, i, 0)),
            pl.BlockSpec((1, tq, f32_w), lambda b, i: (b, i, 0)),
            pl.BlockSpec((1, s, LANES), lambda b, i: (b, 0, 1)),
            pl.BlockSpec((1, s, LANES), lambda b, i: (b, 0, 3)),
            pl.BlockSpec((1, s, mm_w), lambda b, i: (b, 0, 0)),
            _const_spec(pe.shape), _const_spec(w1.shape), _const_spec(w2.shape), _const_spec(overlap.shape),
        ],
        out_specs=pl.BlockSpec((1, tq, NSA_WIDTH), lambda b, i: (b, i, 0)),
        out_shape=jax.ShapeDtypeStruct((bsz, s, NSA_WIDTH), BF16),
        scratch_shapes=[pltpu.VMEM((hk, NSA_NCMP_PAD, d), BF16), pltpu.VMEM((hk, d, NSA_NCMP_PAD), BF16),
                        pltpu.VMEM((hk, s, LANES), BF16), pltpu.VMEM((hk, s, d), BF16),
                        pltpu.VMEM((hk, d + NSA_VPAD, s), BF16), pltpu.VMEM((hk, d + NSA_VPAD, s), BF16),
                        pltpu.VMEM((d, NSA_GROUP * tq), F32), pltpu.VMEM((d, NSA_GROUP * tq), F32)],
        compiler_params=pltpu.CompilerParams(
            dimension_semantics=("parallel", "arbitrary"),
            vmem_limit_bytes=V7X_VMEM_LIMIT_BYTES),
        name="nsa_attention",
    )(h_mm, h_f32, h_f32, h_f32, h_mm, pe, w1, w2, overlap)


def _prep_w_in(w):
    w = w.astype(BF16)
    lead = w.shape[:-1]
    sizes = (3 * CONV_WIDTH, 3 * GDN_WIDTH, GDN_WIDTH, GDN_HEADS, GDN_HEADS,
             NSA_WIDTH, 6 * NSA_KV_HEADS * HEAD_DIM, 3 * NSA_Q_HEADS)
    parts, start = [], 0
    for n in sizes:
        parts.append(w[..., start:start + n])
        start += n
    wconv, gqkv, gz, ga, gb, nq, nkv, ngate = parts
    pad = lambda n: jnp.zeros(lead + (n,), w.dtype)
    wgdn = jnp.concatenate([gqkv, gz, ga, gb, pad(LANES - 2 * GDN_HEADS)], axis=-1)
    nkv = nkv.reshape(lead + (6, NSA_KV_HEADS, HEAD_DIM))
    ngate = ngate.reshape(lead + (NSA_KV_HEADS, NSA_GROUP, 3))
    mm_cols, f32_cols = [], []
    for h in range(NSA_KV_HEADS):
        qw = NSA_GROUP * HEAD_DIM
        gates = jnp.swapaxes(ngate[..., h, :, :], -1, -2).reshape(lead + (3 * NSA_GROUP,))
        kv = nkv[..., h, :].reshape(lead + (6 * HEAD_DIM,))
        mm_cols += [nq[..., h * qw:(h + 1) * qw], kv[..., 2 * HEAD_DIM:]]
        f32_cols += [gates, pad(LANES - 3 * NSA_GROUP), kv[..., :2 * HEAD_DIM]]
    return wconv, wgdn, jnp.concatenate(mm_cols, axis=-1), jnp.concatenate(f32_cols, axis=-1)


def _prep_cmp_weights(pe_k, pe_v, k_w1, k_w2, v_w1, v_w2):
    d, hid = HEAD_DIM, k_w1.shape[1]
    pe = jnp.concatenate([pe_k, pe_v], axis=1)
    k1 = k_w1.reshape(CMP_LEN, d, hid)
    v1 = v_w1.reshape(CMP_LEN, d, hid)
    z1 = jnp.zeros_like(k1)
    w1 = jnp.concatenate([jnp.concatenate([k1, z1], axis=2),
                          jnp.concatenate([z1, v1], axis=2)], axis=1)
    z2 = jnp.zeros_like(k_w2)
    w2 = jnp.concatenate([jnp.concatenate([k_w2, z2], axis=1),
                          jnp.concatenate([z2, v_w2], axis=1)], axis=0)
    return pe, w1.astype(BF16), w2.astype(BF16)


def _overlap_matrix(s):
    n_cmp = (s - CMP_LEN) // CMP_STRIDE + 1
    t = jnp.arange(s)
    starts = jnp.arange(NSA_NCMP_PAD) * CMP_STRIDE
    cmp_tok = (t[None, :] >= starts[:, None]) & (t[None, :] < starts[:, None] + CMP_LEN)
    cmp_tok = cmp_tok & (jnp.arange(NSA_NCMP_PAD) < n_cmp)[:, None]
    sel_tok = (t[None, :] // SEL_BLOCK) == jnp.arange(s // SEL_BLOCK)[:, None]
    return ((sel_tok.astype(F32) @ cmp_tok.astype(F32).T) / CMP_LEN).astype(BF16)


def kernel(x, ffn1_w_gate, ffn1_w_up, ffn1_w_down, ln1_g, ln1_b, w_in, conv_w, gdn_conv_w, gdn_a_log, gdn_dt_bias, gdn_norm_w, cmp_pe_k, cmp_pe_v, cmp_k_w1, cmp_k_w2, cmp_v_w1, cmp_v_w2, w_out, ln2_g, ln2_b, ffn2_w_gate, ffn2_w_up, ffn2_w_down, ln3_g, ln3_b):
    bsz, s, dm = x.shape
    depth = w_in.shape[0]
    alpha = (2 * depth) ** 0.25
    m = bsz * s
    overlap = _overlap_matrix(s)
    row = lambda v: v.reshape(1, -1)
    rows = lambda v: v[:, None, :]
    bf = lambda v: v.astype(BF16)
    ffn1, ffn2 = (bf(ffn1_w_gate), bf(ffn1_w_up), bf(ffn1_w_down)), (bf(ffn2_w_gate), bf(ffn2_w_up), bf(ffn2_w_down))
    w_in_groups = _prep_w_in(w_in)
    wo = bf(w_out)
    conv_taps = jnp.swapaxes(conv_w, 1, 2)
    h = x.reshape(m, dm)
    for l in range(depth):
        h = _ffn_ln(h, *ffn1, rows(ln1_g), rows(ln1_b), layer=l, alpha=alpha)

        hc, hg, hn_mm, hn_f32 = _in_proj(h, w_in_groups, (F32, F32, BF16, F32), layer=l)
        y_b = _gated_deltanet(hg.reshape(bsz, s, -1), gdn_conv_w[l].T,
                              row(jnp.pad(gdn_a_log[l], (0, LANES - GDN_HEADS))),
                              row(jnp.pad(gdn_dt_bias[l], (0, LANES - GDN_HEADS))),
                              row(jnp.tile(gdn_norm_w[l], GDN_HEADS)))
        pe, w1, w2 = _prep_cmp_weights(cmp_pe_k[l], cmp_pe_v[l], cmp_k_w1[l], cmp_k_w2[l], cmp_v_w1[l], cmp_v_w2[l])
        y_c = _nsa_attention(hn_mm.reshape(bsz, s, -1), hn_f32.reshape(bsz, s, -1), pe, w1, w2, overlap)
        h = _out_proj_ln(h, hc, y_b.reshape(m, -1), y_c.reshape(m, -1), conv_taps, wo,
                         rows(ln2_g), rows(ln2_b), layer=l, alpha=alpha, seq_len=s)

        h = _ffn_ln(h, *ffn2, rows(ln3_g), rows(ln3_b), layer=l, alpha=alpha)
    return h.reshape(bsz, s, dm)
```

```python
import functools

import jax
import jax.numpy as jnp
from jax import lax
from jax.experimental import pallas as pl
from jax.experimental.pallas import tpu as pltpu

F32 = jnp.float32
BF16 = jnp.bfloat16

HEAD_DIM = 64
CONV_WIDTH = 256
CONV_K = 3
GDN_HEADS = 4
GDN_WIDTH = GDN_HEADS * HEAD_DIM
GDN_CONV_K = 4
GDN_CHUNK = 64
NSA_Q_HEADS = 8
NSA_KV_HEADS = 2
NSA_GROUP = NSA_Q_HEADS // NSA_KV_HEADS
NSA_WIDTH = NSA_Q_HEADS * HEAD_DIM
CMP_LEN = 32
CMP_STRIDE = 16
SEL_BLOCK = 64
SEL_TOPN = 8
WINDOW = 512
FORCE_BONUS = 1e3
LN_EPS = 1e-5
NORM_EPS = 1e-6
NEG = -1e30

V7X_VMEM_LIMIT_BYTES = 56 * 1024 * 1024
LANES = 128
HEAD_SHIFT = HEAD_DIM.bit_length() - 1
SEL_SHIFT = SEL_BLOCK.bit_length() - 1

GDN_COLS = 4 * GDN_WIDTH + LANES


def _layer_norm(r, g, b):
    mu = jnp.mean(r, axis=-1, keepdims=True)
    c = r - mu
    var = jnp.mean(c * c, axis=-1, keepdims=True)
    return c * lax.rsqrt(var + LN_EPS) * g + b


def _const_spec(shape):
    return pl.BlockSpec(shape, lambda *_: (0,) * len(shape), pipeline_mode=pl.Buffered(1))


def _layer_spec(shape, layer, block=None):
    index = (layer,) + tuple(block or (0,) * len(shape))
    return pl.BlockSpec((None,) + tuple(shape), lambda *_: index, pipeline_mode=pl.Buffered(1))


def _ffn_ln_body(x_ref, wg_ref, wu_ref, wd_ref, g_ref, b_ref, o_ref, *, alpha):
    x = x_ref[...]
    xb = x.astype(BF16)
    hg = jnp.dot(xb, wg_ref[...], preferred_element_type=F32)
    hu = jnp.dot(xb, wu_ref[...], preferred_element_type=F32)
    a = (hg * jax.nn.sigmoid(hg) * hu).astype(BF16)
    y = jnp.dot(a, wd_ref[...], preferred_element_type=F32)
    o_ref[...] = _layer_norm(alpha * x + 0.5 * y, g_ref[...], b_ref[...])


def _ffn_ln(x, wg, wu, wd, g, b, *, layer, alpha, tm=512):
    m, d = x.shape
    f = wg.shape[2]
    return pl.pallas_call(
        functools.partial(_ffn_ln_body, alpha=alpha),
        grid=(m // tm,),
        in_specs=[
            pl.BlockSpec((tm, d), lambda i: (i, 0)),
            _layer_spec((d, f), layer), _layer_spec((d, f), layer), _layer_spec((f, d), layer),
            _layer_spec((1, d), layer), _layer_spec((1, d), layer),
        ],
        out_specs=pl.BlockSpec((tm, d), lambda i: (i, 0)),
        out_shape=jax.ShapeDtypeStruct((m, d), F32),
        compiler_params=pltpu.CompilerParams(
            dimension_semantics=("parallel",), vmem_limit_bytes=V7X_VMEM_LIMIT_BYTES),
        name="ffn_ln",
    )(x, wg, wu, wd, g, b)


def _in_proj_body(x_ref, *refs):
    n = len(refs) // 2
    xb = x_ref[...].astype(BF16)
    for w_ref, o_ref in zip(refs[:n], refs[n:]):
        o_ref[...] = jnp.dot(xb, w_ref[...], preferred_element_type=F32).astype(o_ref.dtype)


def _in_proj(x, weights, out_dtypes, *, layer, tm=512):
    m, d = x.shape
    widths = [w.shape[2] for w in weights]
    return pl.pallas_call(
        _in_proj_body,
        grid=(m // tm,),
        in_specs=[pl.BlockSpec((tm, d), lambda i: (i, 0))] + [_layer_spec((d, w), layer) for w in widths],
        out_specs=[pl.BlockSpec((tm, w), lambda i: (i, 0)) for w in widths],
        out_shape=[jax.ShapeDtypeStruct((m, w), dt) for w, dt in zip(widths, out_dtypes)],
        compiler_params=pltpu.CompilerParams(
            dimension_semantics=("parallel",), vmem_limit_bytes=V7X_VMEM_LIMIT_BYTES),
        name="in_proj",
    )(x, *weights)


SUBLANES = 8


def _out_proj_ln_body(x_ref, hc_ref, halo_ref, yb_ref, yc_ref, taps_ref, wa_ref, wb_ref, wc_ref, g_ref, b_ref,
                      o_ref, *, alpha, tiles_per_seq):
    c = CONV_WIDTH
    h = hc_ref[...]
    u = h[:, c:2 * c] * h[:, 2 * c:3 * c]
    hh = halo_ref[...]
    at_seq_start = pl.program_id(0) % tiles_per_seq == 0
    u_prev = jnp.where(at_seq_start, 0.0, hh[:, c:2 * c] * hh[:, 2 * c:3 * c])
    y_a = h[:, 0:c] * _causal_dwconv(u_prev, u, taps_ref[...])
    y = jnp.dot(y_a.astype(BF16), wa_ref[...], preferred_element_type=F32)
    y += jnp.dot(yb_ref[...].astype(BF16), wb_ref[...], preferred_element_type=F32)
    y += jnp.dot(yc_ref[...].astype(BF16), wc_ref[...], preferred_element_type=F32)
    o_ref[...] = _layer_norm(alpha * x_ref[...] + y, g_ref[...], b_ref[...])


def _out_proj_ln(x, hc, yb, yc, conv_taps, w_out, g, b, *, layer, alpha, seq_len, tm=512):
    m, d = x.shape
    wa, wb, wc = CONV_WIDTH, yb.shape[1], yc.shape[1]
    assert seq_len % tm == 0 and CONV_K - 1 <= SUBLANES and wa == wb and wc == wa + wb
    row = lambda w: pl.BlockSpec((tm, w), lambda i: (i, 0))
    halo = pl.BlockSpec((SUBLANES, hc.shape[1]), lambda i: (jnp.maximum(i * (tm // SUBLANES) - 1, 0), 0))
    return pl.pallas_call(
        functools.partial(_out_proj_ln_body, alpha=alpha, tiles_per_seq=seq_len // tm),
        grid=(m // tm,),
        in_specs=[row(d), row(hc.shape[1]), halo, row(wb), row(wc),
                  _layer_spec(conv_taps.shape[1:], layer),
                  _layer_spec((wa, d), layer, (0, 0)), _layer_spec((wb, d), layer, (1, 0)),
                  _layer_spec((wc, d), layer, (1, 0)),
                  _layer_spec((1, d), layer), _layer_spec((1, d), layer)],
        out_specs=row(d),
        out_shape=jax.ShapeDtypeStruct((m, d), F32),
        compiler_params=pltpu.CompilerParams(
            dimension_semantics=("parallel",), vmem_limit_bytes=V7X_VMEM_LIMIT_BYTES),
        name="out_proj_ln",
    )(x, hc, hc, yb, yc, conv_taps, w_out, w_out, w_out, g, b)


def _causal_dwconv(prev, body, taps):
    ext = jnp.concatenate([prev, body], axis=0)
    k = taps.shape[0]
    y = body * taps[k - 1:k, :]
    for j in range(k - 1):
        y = y + pltpu.roll(ext, k - 1 - j, 0)[SUBLANES:, :] * taps[j:j + 1, :]
    return y


def _dot(a, b):
    return jnp.dot(a, b, preferred_element_type=F32)


def _dot_nt(a, b):
    return lax.dot_general(a, b, (((1,), (1,)), ((), ())), preferred_element_type=F32)


def _dot_tn(a, b):
    return lax.dot_general(a, b, (((0,), (0,)), ((), ())), preferred_element_type=F32)


def _split_bf16(a, terms):
    parts, rest = [], a
    for _ in range(terms):
        p = rest.astype(BF16)
        parts.append(p)
        rest = rest - p.astype(F32)
    return parts


def _dot_exact_lhs(m_bf16, a, terms):
    return sum(_dot(m_bf16, p) for p in _split_bf16(a, terms))


def _dot_exact_rhs(a, m_bf16, terms):
    return sum(_dot(p, m_bf16) for p in _split_bf16(a, terms))


def _iota2(shape, dim):
    return lax.broadcasted_iota(jnp.int32, shape, dim)


GDN_INV_BASE = 8
GDN_CHUNKS_PER_STEP = 4


def _block_diag(x, bd_mask):
    return jnp.concatenate([x] * (x.shape[1] // x.shape[0]), axis=0) * bd_mask


def _heads_dot_hl(a, b, bd_mask):
    a_hi, a_lo = _split_bf16(a, 2)
    b_hi, b_lo = _split_bf16(b, 2)
    r = a.shape[0]
    hi = _dot(jnp.concatenate([a_hi, a_lo], axis=0), _block_diag(b_hi, bd_mask))
    return hi[0:r] + hi[r:2 * r] + _dot(a_hi, _block_diag(b_lo, bd_mask))


def _inverse_masks(c, width):
    ri, ci = _iota2((c, width), 0), _iota2((c, width), 1) & (c - 1)
    base = GDN_INV_BASE.bit_length() - 1
    eye = (ci == ri).astype(F32)
    diag = (ri >> base) == (ci >> base)
    levels = [((ri >> (s + 1)) == (ci >> (s + 1))) & ((ri >> s) != (ci >> s))
              for s in range(base, c.bit_length() - 1)]
    return eye, diag, levels


def _unit_lower_inverse(lmats, masks, bd_mask):
    eye, diag, levels = masks
    c = lmats[0].shape[0]
    hdot = lambda a, b: _heads_dot_hl(a, b, bd_mask)
    l0 = [jnp.where(diag, l, 0.0) for l in lmats]
    p = [eye - x for x in l0]
    m = [hdot(x, x) for x in l0]
    yield
    pm = [hdot(jnp.concatenate([pi, mi], axis=0), mi) for pi, mi in zip(p, m)]
    yield
    p = [pi + x[0:c] for pi, x in zip(p, pm)]
    inv = [pi + hdot(pi, x[c:2 * c]) for pi, x in zip(p, pm)]
    yield
    for level in levels:
        t = [hdot(i, jnp.where(level, l, 0.0)) for i, l in zip(inv, lmats)]
        yield
        inv = [i - hdot(ti, i) for i, ti in zip(inv, t)]
        yield
    return inv


def _interleave(*stage_generators):
    live = list(stage_generators)
    while live:
        for gen in list(live):
            try:
                next(gen)
            except StopIteration:
                live.remove(gen)


def _softplus(x):
    return jnp.maximum(x, 0.0) + jnp.log1p(jnp.exp(-jnp.abs(x)))


def _gdn_body(h_ref, taps_ref, alog_ref, dtb_ref, nw_ref, o_ref,
              qn_s, kn_s, kb_s, vb_s, g_s, u_s, w_s, a_s, qd_s, kd_s, egl_s, o_s):
    s = h_ref.shape[1]
    c = GDN_CHUNK
    d = HEAD_DIM
    w = GDN_WIDTH
    nc = s // c

    bd_mask = (_iota2((w, w), 0) >> HEAD_SHIFT == _iota2((w, w), 1) >> HEAD_SHIFT).astype(BF16)

    def head_sum(t):
        return _dot_exact_rhs(t, bd_mask, 2)

    def l2norm(t):
        return t * lax.rsqrt(head_sum(t * t) + NORM_EPS)

    def conv_silu(slab):
        cols = slice(slab * w, (slab + 1) * w)
        y = _causal_dwconv(jnp.zeros((SUBLANES, w), F32), h_ref[0, :, cols], taps_ref[:, cols])
        return y * jax.nn.sigmoid(y)

    qn_s[...] = l2norm(conv_silu(0)) * (d ** -0.5)
    kn = l2norm(conv_silu(1))
    kn_s[...] = kn
    ab = h_ref[0, :, 4 * w:4 * w + LANES]
    src = _iota2((LANES, w), 0)
    head = _iota2((LANES, w), 1) >> HEAD_SHIFT
    g_small = -jnp.exp(alog_ref[...]) * _softplus(ab + dtb_ref[...])
    g_s[...] = _dot_exact_rhs(g_small, (src == head).astype(BF16), 3)
    beta = _dot_exact_rhs(jax.nn.sigmoid(ab), (src == head + GDN_HEADS).astype(BF16), 3)
    kb_s[...] = kn * beta
    vb_s[...] = conv_silu(2) * beta

    ri = _iota2((c, w), 0)
    ci = _iota2((c, w), 1) & (c - 1)
    tril = ci <= ri
    strict = ci < ri
    upper = (ri <= ci).astype(F32)
    inv_masks = _inverse_masks(c, w)
    tril_b = (_iota2((c, c), 1) <= _iota2((c, c), 0)).astype(BF16)
    ones_b = jnp.ones((c, c), BF16)

    grp = GDN_CHUNKS_PER_STEP
    n_groups = nc // grp

    def chunk_rows(n):
        return pl.ds(n * c if isinstance(n, int) else pl.multiple_of(n * c, c), c)

    def prep_stages(group):
        ns = [group * grp + i for i in range(grp)]
        rows = [chunk_rows(n) for n in ns]
        g = [g_s[r, :] for r in rows]
        gc = [_dot_exact_lhs(tril_b, x, 3) for x in g]
        gct = [_dot_exact_lhs(ones_b, x * upper, 3) for x in g]
        yield
        kn_c = [kn_s[r, :] for r in rows]
        kb_c = [kb_s[r, :] for r in rows]
        qn_c = [qn_s[r, :] for r in rows]
        kq = [_dot_nt(jnp.concatenate([kb, q], axis=0).astype(BF16), _block_diag(k.astype(BF16), bd_mask))
              for kb, q, k in zip(kb_c, qn_c, kn_c)]
        yield
        decay = [jnp.where(tril, jnp.exp(jnp.where(tril, a - b, 0.0)), 0.0) for a, b in zip(gc, gct)]
        lmat = [jnp.where(strict, x[0:c] * dk, 0.0) for x, dk in zip(kq, decay)]
        tinv = yield from _unit_lower_inverse(lmat, inv_masks, bd_mask)
        egc = [jnp.exp(x) for x in gc]
        rhs = [jnp.concatenate([_block_diag(vb_s[r, :].astype(BF16), bd_mask),
                                _block_diag((kb * e).astype(BF16), bd_mask)], axis=1)
               for r, kb, e in zip(rows, kb_c, egc)]
        uw = [_dot(t.astype(BF16), x) for t, x in zip(tinv, rhs)]
        yield
        for i, (n, r) in enumerate(zip(ns, rows)):
            glast = gc[i][c - 1:c, :]
            egl_s[n] = jnp.exp(glast)
            u_s[r, :] = uw[i][:, 0:w]
            w_s[r, :] = uw[i][:, w:2 * w].astype(BF16)
            a_s[r, :] = jnp.where(tril, kq[i][c:2 * c] * decay[i], 0.0).astype(BF16)
            qd_s[r, :] = (qn_c[i] * egc[i]).astype(BF16)
            kd_s[r, :] = (kn_c[i] * jnp.exp(glast - gc[i])).astype(BF16)

    lane_head = _iota2((d, w), 1) >> HEAD_SHIFT

    def scan_stages(group, state):
        for i in range(grp):
            n = group * grp + i
            rows = chunk_rows(n)
            st = state[0]
            st_bd = _block_diag(st.astype(BF16), bd_mask)
            ws_qs = _dot(jnp.concatenate([w_s[rows, :], qd_s[rows, :]], axis=0), st_bd)
            yield
            v_new = u_s[rows, :] - ws_qs[0:c]
            v_b = v_new.astype(BF16)
            o_s[rows, :] = ws_qs[c:2 * c] + _dot(a_s[rows, :], _block_diag(v_b, bd_mask))
            kv = _dot_tn(kd_s[rows, :], v_b)
            upd = jnp.zeros((d, w), F32)
            for h in range(GDN_HEADS):
                upd = upd + jnp.where(lane_head == h, kv[h * d:(h + 1) * d, :], 0.0)
            state[0] = st * egl_s[n] + upd
            yield

    _interleave(prep_stages(0))

    def group_body(group, st):
        state = [st]
        _interleave(prep_stages(group), scan_stages(group - 1, state))
        return state[0]

    state = [lax.fori_loop(1, n_groups, group_body, jnp.zeros((d, w), F32))]
    _interleave(scan_stages(n_groups - 1, state))

    o = o_s[...]
    o = o * lax.rsqrt(head_sum(o * o) * (1.0 / d) + NORM_EPS) * nw_ref[...]
    z = h_ref[0, :, 3 * w:4 * w]
    o_ref[0] = (o * (z * jax.nn.sigmoid(z))).astype(o_ref.dtype)


def _gated_deltanet(hg, conv_taps, alog_rep, dtb_rep, nw_rep):
    bsz, s, cols = hg.shape
    w = GDN_WIDTH
    assert GDN_CHUNK == HEAD_DIM and cols == GDN_COLS and s % (GDN_CHUNK * GDN_CHUNKS_PER_STEP) == 0
    assert conv_taps.shape[0] == GDN_CONV_K and GDN_CONV_K - 1 <= SUBLANES
    return pl.pallas_call(
        _gdn_body,
        grid=(bsz,),
        in_specs=[pl.BlockSpec((1, s, cols), lambda b: (b, 0, 0)),
                  _const_spec(conv_taps.shape),
                  _const_spec((1, LANES)), _const_spec((1, LANES)), _const_spec((1, w))],
        out_specs=pl.BlockSpec((1, s, w), lambda b: (b, 0, 0)),
        out_shape=jax.ShapeDtypeStruct((bsz, s, w), BF16),
        scratch_shapes=[pltpu.VMEM((s, w), F32)] * 6 + [pltpu.VMEM((s, w), BF16)] * 4 + [
            pltpu.VMEM((s // GDN_CHUNK, 1, w), F32), pltpu.VMEM((s, w), F32)],
        compiler_params=pltpu.CompilerParams(
            dimension_semantics=("parallel",), vmem_limit_bytes=V7X_VMEM_LIMIT_BYTES),
        name="gated_deltanet",
    )(hg, conv_taps, alog_rep, dtb_rep, nw_rep)


NSA_TQ = 256
NSA_SLC_STEP = 256
NSA_NCMP_PAD = 128
NSA_VPAD = 16


def _softmax_cols(s):
    m = jnp.max(s, axis=0, keepdims=True)
    e = jnp.exp(s - m)
    return e, jnp.sum(e, axis=0, keepdims=True)


def _nsa_body(q_ref, gate_ref, cmp_ref, kv_ref, pe_ref, w1_ref, w2_ref, ovlt_ref, o_ref,
              kc_s, vct_s, ks_s, kw_s, vst_s, vwt_s, oslc_s, owin_s):
    qi = pl.program_id(2)
    tq, d, grp = NSA_TQ, HEAD_DIM, NSA_GROUP
    s_len = kv_ref.shape[1]
    n_sel = s_len // SEL_BLOCK
    seg = CMP_LEN // CMP_STRIDE
    band = WINDOW + tq
    wide = grp * tq

    @pl.when(qi == 0)
    def _():
        def relayout(i, carry):
            r0 = pl.multiple_of(i * LANES, LANES)
            blk = kv_ref[0, pl.ds(r0, LANES), :]
            block_of_row = (r0 + _iota2((LANES, LANES - d), 0)) >> SEL_SHIFT
            one_hot = (block_of_row == _iota2((LANES, LANES - d), 1)).astype(BF16)
            ks_s[pl.ds(r0, LANES), :] = jnp.concatenate([blk[:, 0:d].astype(BF16), one_hot], axis=1)
            kw_s[pl.ds(r0, LANES), :] = blk[:, 2 * d:3 * d].astype(BF16)
            blk_t = blk.astype(F32).T
            ones_row = (_iota2((NSA_VPAD, LANES), 0) == 0).astype(BF16)
            vst_s[:, pl.ds(r0, LANES)] = jnp.concatenate([blk_t[d:2 * d, :].astype(BF16), ones_row], axis=0)
            vwt_s[:, pl.ds(r0, LANES)] = jnp.concatenate([blk_t[3 * d:4 * d, :].astype(BF16), ones_row], axis=0)
            return carry

        lax.fori_loop(0, s_len // LANES, relayout, 0)
        pre = [jnp.zeros((NSA_NCMP_PAD, 2 * LANES), F32) for _ in range(seg)]
        for l in range(CMP_STRIDE):
            t_l = cmp_ref[0, pl.ds(l, NSA_NCMP_PAD, stride=CMP_STRIDE), :]
            for h in range(seg):
                lh = l + h * CMP_STRIDE
                pre[h] = pre[h] + _dot((t_l + pe_ref[lh:lh + 1, :]).astype(BF16), w1_ref[lh])
        hid = pre[0] + pltpu.roll(pre[1], NSA_NCMP_PAD - 1, 0)
        hid = hid * jax.nn.sigmoid(hid)
        kcv = _dot(hid.astype(BF16), w2_ref[...])
        kc_s[...] = kcv[:, 0:d].astype(BF16)
        vct_s[...] = kcv.T[d:2 * d, :].astype(BF16)

    ts = qi * tq
    q_t = (q_ref[0].astype(F32) * (d ** -0.5)).T
    q_t = jnp.concatenate([q_t[g * d:(g + 1) * d, :] for g in range(grp)], axis=1).astype(BF16)
    t_row = ts + (_iota2((1, wide), 1) & (tq - 1))
    t_row1 = t_row[:, 0:tq]

    w0 = pl.multiple_of(jnp.maximum(ts - WINDOW, 0), tq)
    dist = t_row - (w0 + _iota2((band, 1), 0))
    sc_win = jnp.where((dist >= 0) & (dist < WINDOW), _dot(kw_s[pl.ds(w0, band), :], q_t), NEG)
    m_win = jnp.max(sc_win, axis=0, keepdims=True)

    n_col =_iota2((NSA_NCMP_PAD, 1), 0)
    cmp_ok = (n_col * CMP_STRIDE + (CMP_LEN - 1) <= t_row) & (n_col < NSA_NCMP_PAD - 1)
    e, den = _softmax_cols(jnp.where(cmp_ok, _dot(kc_s[...], q_t), NEG))
    p_cmp = jnp.where(t_row >= CMP_LEN - 1, e / den, 0.0)
    o_cmp = _dot(vct_s[...], p_cmp.astype(BF16))

    p_sum = p_cmp[:, 0:tq]
    for g in range(1, grp):
        p_sum = p_sum + p_cmp[:, g * tq:(g + 1) * tq]
    imp = _dot_exact_lhs(ovlt_ref[...], p_sum, 3)
    j_idx = _iota2((n_sel, tq), 0)
    q_blk = t_row1 >> SEL_SHIFT
    forced = (j_idx == 0) | (j_idx == q_blk) | (j_idx == q_blk - 1)
    imp = jnp.where(forced, imp + FORCE_BONUS, imp)
    causal_blk = j_idx <= q_blk
    imp = jnp.where(causal_blk, imp, NEG)
    rank = jnp.zeros((n_sel, tq), F32)
    for jp in range(n_sel):
        row = imp[jp:jp + 1, :]
        beats = (row > imp) | ((row == imp) & (j_idx > jp))
        rank = rank + beats.astype(F32)
    selected = (rank < SEL_TOPN) & causal_blk

    blk0 = ts >> SEL_SHIFT
    sel_bias = jnp.where(selected, 0.0, NEG)
    past_bias = jnp.where(j_idx < blk0, sel_bias, NEG)
    zero_rows = jnp.zeros((LANES - d - n_sel, wide), BF16)

    def with_bias(bias):
        return jnp.concatenate([q_t, jnp.concatenate([bias.astype(BF16)] * grp, axis=1), zero_rows], axis=0)

    sc_diag = jnp.where(ts + _iota2((tq, 1), 0) <= t_row,
                        _dot(ks_s[pl.ds(ts, tq), :], with_bias(sel_bias)), NEG)
    q_past = with_bias(past_bias)

    ch = NSA_SLC_STEP

    def weighted_values(value_t, sc, m):
        return _dot(value_t, jnp.exp(sc - m).astype(BF16))

    def normalised(acc):
        return acc[0:d] / acc[d:d + 1]

    def slc_and_win(nk):
        n_past, n_win = nk // ch, band // ch
        past_sc, win_acc = [], None
        for i in range(max(n_past, n_win)):
            if i < n_past:
                past_sc.append(_dot(ks_s[i * ch:(i + 1) * ch, :], q_past))
            if i < n_win:
                term = weighted_values(vwt_s[:, pl.ds(w0 + i * ch, ch)], sc_win[i * ch:(i + 1) * ch], m_win)
                win_acc = term if win_acc is None else win_acc + term
        owin_s[...] = normalised(win_acc)
        m = functools.reduce(jnp.maximum, [jnp.max(p, axis=0, keepdims=True) for p in past_sc + [sc_diag]])
        acc = None
        for i in range(n_past):
            term = weighted_values(vst_s[:, i * ch:(i + 1) * ch], past_sc[i], m)
            acc = term if acc is None else acc + term
        term = weighted_values(vst_s[:, pl.ds(ts, tq)], sc_diag, m)
        oslc_s[...] = normalised(term if acc is None else acc + term)

    n_var = (s_len - tq + NSA_SLC_STEP - 1) // NSA_SLC_STEP + 1
    variant = (ts + NSA_SLC_STEP - 1) // NSA_SLC_STEP
    for v in range(n_var):
        @pl.when(variant == v)
        def _(v=v):
            slc_and_win(min(v * NSA_SLC_STEP, s_len))
    o_slc, o_win = oslc_s[...], owin_s[...]

    gt = jax.nn.sigmoid(gate_ref[0]).T
    outs = []
    for g in range(grp):
        cols = slice(g * tq, (g + 1) * tq)
        outs.append(gt[g:g + 1, :] * o_cmp[:, cols] + gt[grp + g:grp + g + 1, :] * o_slc[:, cols]
                    + gt[2 * grp + g:2 * grp + g + 1, :] * o_win[:, cols])
    o_ref[0] = jnp.concatenate(outs, axis=0).T.astype(o_ref.dtype)


def _nsa_attention(h_mm, h_f32, pe, w1, w2, overlap):
    bsz, s, _ = h_mm.shape
    tq, d = NSA_TQ, HEAD_DIM
    qw = NSA_GROUP * d
    assert h_mm.shape[2] == NSA_KV_HEADS * 2 * qw and h_f32.shape[2] == NSA_KV_HEADS * 2 * LANES
    assert s % tq == 0 and s >= WINDOW + tq and (WINDOW + tq) % NSA_SLC_STEP == 0 and d + s // SEL_BLOCK <= LANES
    return pl.pallas_call(
        _nsa_body,
        grid=(bsz, NSA_KV_HEADS, s // tq),
        in_specs=[
            pl.BlockSpec((1, tq, qw), lambda b, h, i: (b, i, 2 * h)),
            pl.BlockSpec((1, tq, LANES), lambda b, h, i: (b, i, 2 * h)),
            pl.BlockSpec((1, s, LANES), lambda b, h, i: (b, 0, 2 * h + 1)),
            pl.BlockSpec((1, s, qw), lambda b, h, i: (b, 0, 2 * h + 1)),
            _const_spec(pe.shape), _const_spec(w1.shape), _const_spec(w2.shape), _const_spec(overlap.shape),
        ],
        out_specs=pl.BlockSpec((1, tq, qw), lambda b, h, i: (b, i, h)),
        out_shape=jax.ShapeDtypeStruct((bsz, s, NSA_WIDTH), BF16),
        scratch_shapes=[pltpu.VMEM((NSA_NCMP_PAD, d), BF16), pltpu.VMEM((d, NSA_NCMP_PAD), BF16),
                        pltpu.VMEM((s, LANES), BF16), pltpu.VMEM((s, d), BF16),
                        pltpu.VMEM((d + NSA_VPAD, s), BF16), pltpu.VMEM((d + NSA_VPAD, s), BF16),
                        pltpu.VMEM((d, NSA_GROUP * tq), F32), pltpu.VMEM((d, NSA_GROUP * tq), F32)],
        compiler_params=pltpu.CompilerParams(
            dimension_semantics=("parallel", "parallel", "arbitrary"),
            vmem_limit_bytes=V7X_VMEM_LIMIT_BYTES),
        name="nsa_attention",
    )(h_mm, h_f32, h_f32, h_mm, pe, w1, w2, overlap)


def _prep_w_in(w):
    w = w.astype(BF16)
    lead = w.shape[:-1]
    sizes = (3 * CONV_WIDTH, 3 * GDN_WIDTH, GDN_WIDTH, GDN_HEADS, GDN_HEADS,
             NSA_WIDTH, 6 * NSA_KV_HEADS * HEAD_DIM, 3 * NSA_Q_HEADS)
    parts, start = [], 0
    for n in sizes:
        parts.append(w[..., start:start + n])
        start += n
    wconv, gqkv, gz, ga, gb, nq, nkv, ngate = parts
    pad = lambda n: jnp.zeros(lead + (n,), w.dtype)
    wgdn = jnp.concatenate([gqkv, gz, ga, gb, pad(LANES - 2 * GDN_HEADS)], axis=-1)
    nkv = nkv.reshape(lead + (6, NSA_KV_HEADS, HEAD_DIM))
    ngate = ngate.reshape(lead + (NSA_KV_HEADS, NSA_GROUP, 3))
    mm_cols, f32_cols = [], []
    for h in range(NSA_KV_HEADS):
        qw = NSA_GROUP * HEAD_DIM
        gates = jnp.swapaxes(ngate[..., h, :, :], -1, -2).reshape(lead + (3 * NSA_GROUP,))
        kv = nkv[..., h, :].reshape(lead + (6 * HEAD_DIM,))
        mm_cols += [nq[..., h * qw:(h + 1) * qw], kv[..., 2 * HEAD_DIM:]]
        f32_cols += [gates, pad(LANES - 3 * NSA_GROUP), kv[..., :2 * HEAD_DIM]]
    return wconv, wgdn, jnp.concatenate(mm_cols, axis=-1), jnp.concatenate(f32_cols, axis=-1)


def _prep_cmp_weights(pe_k, pe_v, k_w1, k_w2, v_w1, v_w2):
    d, hid = HEAD_DIM, k_w1.shape[1]
    pe = jnp.concatenate([pe_k, pe_v], axis=1)
    k1 = k_w1.reshape(CMP_LEN, d, hid)
    v1 = v_w1.reshape(CMP_LEN, d, hid)
    z1 = jnp.zeros_like(k1)
    w1 = jnp.concatenate([jnp.concatenate([k1, z1], axis=2),
                          jnp.concatenate([z1, v1], axis=2)], axis=1)
    z2 = jnp.zeros_like(k_w2)
    w2 = jnp.concatenate([jnp.concatenate([k_w2, z2], axis=1),
                          jnp.concatenate([z2, v_w2], axis=1)], axis=0)
    return pe, w1.astype(BF16), w2.astype(BF16)


def _overlap_matrix(s):
    n_cmp = (s - CMP_LEN) // CMP_STRIDE + 1
    t = jnp.arange(s)
    starts = jnp.arange(NSA_NCMP_PAD) * CMP_STRIDE
    cmp_tok = (t[None, :] >= starts[:, None]) & (t[None, :] < starts[:, None] + CMP_LEN)
    cmp_tok = cmp_tok & (jnp.arange(NSA_NCMP_PAD) < n_cmp)[:, None]
    sel_tok = (t[None, :] // SEL_BLOCK) == jnp.arange(s // SEL_BLOCK)[:, None]
    return ((sel_tok.astype(F32) @ cmp_tok.astype(F32).T) / CMP_LEN).astype(BF16)


def kernel(x, ffn1_w_gate, ffn1_w_up, ffn1_w_down, ln1_g, ln1_b, w_in, conv_w, gdn_conv_w, gdn_a_log, gdn_dt_bias, gdn_norm_w, cmp_pe_k, cmp_pe_v, cmp_k_w1, cmp_k_w2, cmp_v_w1, cmp_v_w2, w_out, ln2_g, ln2_b, ffn2_w_gate, ffn2_w_up, ffn2_w_down, ln3_g, ln3_b):
    bsz, s, dm = x.shape
    depth = w_in.shape[0]
    alpha = (2 * depth) ** 0.25
    m = bsz * s
    overlap = _overlap_matrix(s)
    row = lambda v: v.reshape(1, -1)
    rows = lambda v: v[:, None, :]
    bf = lambda v: v.astype(BF16)
    ffn1, ffn2 = (bf(ffn1_w_gate), bf(ffn1_w_up), bf(ffn1_w_down)), (bf(ffn2_w_gate), bf(ffn2_w_up), bf(ffn2_w_down))
    w_in_groups = _prep_w_in(w_in)
    wo = bf(w_out)
    conv_taps = jnp.swapaxes(conv_w, 1, 2)
    h = x.reshape(m, dm)
    for l in range(depth):
        h = _ffn_ln(h, *ffn1, rows(ln1_g), rows(ln1_b), layer=l, alpha=alpha)

        hc, hg, hn_mm, hn_f32 = _in_proj(h, w_in_groups, (F32, F32, BF16, F32), layer=l)
        y_b = _gated_deltanet(hg.reshape(bsz, s, -1), gdn_conv_w[l].T,
                              row(jnp.pad(gdn_a_log[l], (0, LANES - GDN_HEADS))),
                              row(jnp.pad(gdn_dt_bias[l], (0, LANES - GDN_HEADS))),
                              row(jnp.tile(gdn_norm_w[l], GDN_HEADS)))
        pe, w1, w2 = _prep_cmp_weights(cmp_pe_k[l], cmp_pe_v[l], cmp_k_w1[l], cmp_k_w2[l], cmp_v_w1[l], cmp_v_w2[l])
        y_c = _nsa_attention(hn_mm.reshape(bsz, s, -1), hn_f32.reshape(bsz, s, -1), pe, w1, w2, overlap)
        h = _out_proj_ln(h, hc, y_b.reshape(m, -1), y_c.reshape(m, -1), conv_taps, wo,
                         rows(ln2_g), rows(ln2_b), layer=l, alpha=alpha, seq_len=s)

        h = _ffn_ln(h, *ffn2, rows(ln3_g), rows(ln3_b), layer=l, alpha=alpha)
    return h.reshape(bsz, s, dm)
```

```python
import functools

import jax
import jax.numpy as jnp
from jax import lax
from jax.experimental import pallas as pl
from jax.experimental.pallas import tpu as pltpu

F32 = jnp.float32
BF16 = jnp.bfloat16

HEAD_DIM = 64
CONV_WIDTH = 256
CONV_K = 3
GDN_HEADS = 4
GDN_WIDTH = GDN_HEADS * HEAD_DIM
GDN_CONV_K = 4
GDN_CHUNK = 64
NSA_Q_HEADS = 8
NSA_KV_HEADS = 2
NSA_GROUP = NSA_Q_HEADS // NSA_KV_HEADS
NSA_WIDTH = NSA_Q_HEADS * HEAD_DIM
CMP_LEN = 32
CMP_STRIDE = 16
SEL_BLOCK = 64
SEL_TOPN = 8
WINDOW = 512
FORCE_BONUS = 1e3
LN_EPS = 1e-5
NORM_EPS = 1e-6
NEG = -1e30

V7X_VMEM_LIMIT_BYTES = 56 * 1024 * 1024
LANES = 128
HEAD_SHIFT = HEAD_DIM.bit_length() - 1
SEL_SHIFT = SEL_BLOCK.bit_length() - 1

GDN_COLS = 4 * GDN_WIDTH + LANES


def _layer_norm(r, g, b):
    mu = jnp.mean(r, axis=-1, keepdims=True)
    c = r - mu
    var = jnp.mean(c * c, axis=-1, keepdims=True)
    return c * lax.rsqrt(var + LN_EPS) * g + b


def _const_spec(shape):
    return pl.BlockSpec(shape, lambda *_: (0,) * len(shape), pipeline_mode=pl.Buffered(1))


def _layer_spec(shape, layer, block=None):
    index = (layer,) + tuple(block or (0,) * len(shape))
    return pl.BlockSpec((None,) + tuple(shape), lambda *_: index, pipeline_mode=pl.Buffered(1))


def _ffn_ln_body(x_ref, wg_ref, wu_ref, wd_ref, g_ref, b_ref, o_ref, *, alpha):
    x = x_ref[...]
    xb = x.astype(BF16)
    hg = jnp.dot(xb, wg_ref[...], preferred_element_type=F32)
    hu = jnp.dot(xb, wu_ref[...], preferred_element_type=F32)
    a = (hg * jax.nn.sigmoid(hg) * hu).astype(BF16)
    y = jnp.dot(a, wd_ref[...], preferred_element_type=F32)
    o_ref[...] = _layer_norm(alpha * x + 0.5 * y, g_ref[...], b_ref[...])


def _ffn_ln(x, wg, wu, wd, g, b, *, layer, alpha, tm=512):
    m, d = x.shape
    f = wg.shape[2]
    return pl.pallas_call(
        functools.partial(_ffn_ln_body, alpha=alpha),
        grid=(m // tm,),
        in_specs=[
            pl.BlockSpec((tm, d), lambda i: (i, 0)),
            _layer_spec((d, f), layer), _layer_spec((d, f), layer), _layer_spec((f, d), layer),
            _layer_spec((1, d), layer), _layer_spec((1, d), layer),
        ],
        out_specs=pl.BlockSpec((tm, d), lambda i: (i, 0)),
        out_shape=jax.ShapeDtypeStruct((m, d), F32),
        compiler_params=pltpu.CompilerParams(
            dimension_semantics=("parallel",), vmem_limit_bytes=V7X_VMEM_LIMIT_BYTES),
        name="ffn_ln",
    )(x, wg, wu, wd, g, b)


def _in_proj_body(x_ref, *refs):
    n = len(refs) // 2
    xb = x_ref[...].astype(BF16)
    for w_ref, o_ref in zip(refs[:n], refs[n:]):
        o_ref[...] = jnp.dot(xb, w_ref[...], preferred_element_type=F32).astype(o_ref.dtype)


def _in_proj(x, weights, out_dtypes, *, layer, tm=512):
    m, d = x.shape
    widths = [w.shape[2] for w in weights]
    return pl.pallas_call(
        _in_proj_body,
        grid=(m // tm,),
        in_specs=[pl.BlockSpec((tm, d), lambda i: (i, 0))] + [_layer_spec((d, w), layer) for w in widths],
        out_specs=[pl.BlockSpec((tm, w), lambda i: (i, 0)) for w in widths],
        out_shape=[jax.ShapeDtypeStruct((m, w), dt) for w, dt in zip(widths, out_dtypes)],
        compiler_params=pltpu.CompilerParams(
            dimension_semantics=("parallel",), vmem_limit_bytes=V7X_VMEM_LIMIT_BYTES),
        name="in_proj",
    )(x, *weights)


SUBLANES = 8


def _out_proj_ln_body(x_ref, hc_ref, halo_ref, yb_ref, yc_ref, taps_ref, wa_ref, wb_ref, wc_ref, g_ref, b_ref,
                      o_ref, *, alpha, tiles_per_seq):
    c = CONV_WIDTH
    h = hc_ref[...]
    u = h[:, c:2 * c] * h[:, 2 * c:3 * c]
    hh = halo_ref[...]
    at_seq_start = pl.program_id(0) % tiles_per_seq == 0
    u_prev = jnp.where(at_seq_start, 0.0, hh[:, c:2 * c] * hh[:, 2 * c:3 * c])
    y_a = h[:, 0:c] * _causal_dwconv(u_prev, u, taps_ref[...])
    y = jnp.dot(y_a.astype(BF16), wa_ref[...], preferred_element_type=F32)
    y += jnp.dot(yb_ref[...].astype(BF16), wb_ref[...], preferred_element_type=F32)
    y += jnp.dot(yc_ref[...].astype(BF16), wc_ref[...], preferred_element_type=F32)
    o_ref[...] = _layer_norm(alpha * x_ref[...] + y, g_ref[...], b_ref[...])


def _out_proj_ln(x, hc, yb, yc, conv_taps, w_out, g, b, *, layer, alpha, seq_len, tm=512):
    m, d = x.shape
    wa, wb, wc = CONV_WIDTH, yb.shape[1], yc.shape[1]
    assert seq_len % tm == 0 and CONV_K - 1 <= SUBLANES and wa == wb and wc == wa + wb
    row = lambda w: pl.BlockSpec((tm, w), lambda i: (i, 0))
    halo = pl.BlockSpec((SUBLANES, hc.shape[1]), lambda i: (jnp.maximum(i * (tm // SUBLANES) - 1, 0), 0))
    return pl.pallas_call(
        functools.partial(_out_proj_ln_body, alpha=alpha, tiles_per_seq=seq_len // tm),
        grid=(m // tm,),
        in_specs=[row(d), row(hc.shape[1]), halo, row(wb), row(wc),
                  _layer_spec(conv_taps.shape[1:], layer),
                  _layer_spec((wa, d), layer, (0, 0)), _layer_spec((wb, d), layer, (1, 0)),
                  _layer_spec((wc, d), layer, (1, 0)),
                  _layer_spec((1, d), layer), _layer_spec((1, d), layer)],
        out_specs=row(d),
        out_shape=jax.ShapeDtypeStruct((m, d), F32),
        compiler_params=pltpu.CompilerParams(
            dimension_semantics=("parallel",), vmem_limit_bytes=V7X_VMEM_LIMIT_BYTES),
        name="out_proj_ln",
    )(x, hc, hc, yb, yc, conv_taps, w_out, w_out, w_out, g, b)


def _causal_dwconv(prev, body, taps):
    ext = jnp.concatenate([prev, body], axis=0)
    k = taps.shape[0]
    y = body * taps[k - 1:k, :]
    for j in range(k - 1):
        y = y + pltpu.roll(ext, k - 1 - j, 0)[SUBLANES:, :] * taps[j:j + 1, :]
    return y


def _dot(a, b):
    return jnp.dot(a, b, preferred_element_type=F32)


def _dot_nt(a, b):
    return lax.dot_general(a, b, (((1,), (1,)), ((), ())), preferred_element_type=F32)


def _dot_tn(a, b):
    return lax.dot_general(a, b, (((0,), (0,)), ((), ())), preferred_element_type=F32)


def _split_bf16(a, terms):
    parts, rest = [], a
    for _ in range(terms):
        p = rest.astype(BF16)
        parts.append(p)
        rest = rest - p.astype(F32)
    return parts


def _dot_exact_lhs(m_bf16, a, terms):
    return sum(_dot(m_bf16, p) for p in _split_bf16(a, terms))


def _dot_exact_rhs(a, m_bf16, terms):
    return sum(_dot(p, m_bf16) for p in _split_bf16(a, terms))


def _iota2(shape, dim):
    return lax.broadcasted_iota(jnp.int32, shape, dim)


GDN_INV_BASE = 8
GDN_CHUNKS_PER_STEP = 4


def _block_diag(x, bd_mask):
    return jnp.concatenate([x] * (x.shape[1] // x.shape[0]), axis=0) * bd_mask


def _heads_dot_hl(a, b, bd_mask):
    a_hi, a_lo = _split_bf16(a, 2)
    b_hi, b_lo = _split_bf16(b, 2)
    r = a.shape[0]
    hi = _dot(jnp.concatenate([a_hi, a_lo], axis=0), _block_diag(b_hi, bd_mask))
    return hi[0:r] + hi[r:2 * r] + _dot(a_hi, _block_diag(b_lo, bd_mask))


def _inverse_masks(c, width):
    ri, ci = _iota2((c, width), 0), _iota2((c, width), 1) & (c - 1)
    base = GDN_INV_BASE.bit_length() - 1
    eye = (ci == ri).astype(F32)
    diag = (ri >> base) == (ci >> base)
    levels = [((ri >> (s + 1)) == (ci >> (s + 1))) & ((ri >> s) != (ci >> s))
              for s in range(base, c.bit_length() - 1)]
    return eye, diag, levels


def _unit_lower_inverse(lmats, masks, bd_mask):
    eye, diag, levels = masks
    c = lmats[0].shape[0]
    hdot = lambda a, b: _heads_dot_hl(a, b, bd_mask)
    l0 = [jnp.where(diag, l, 0.0) for l in lmats]
    p = [eye - x for x in l0]
    m = [hdot(x, x) for x in l0]
    yield
    pm = [hdot(jnp.concatenate([pi, mi], axis=0), mi) for pi, mi in zip(p, m)]
    yield
    p = [pi + x[0:c] for pi, x in zip(p, pm)]
    inv = [pi + hdot(pi, x[c:2 * c]) for pi, x in zip(p, pm)]
    yield
    for level in levels:
        t = [hdot(i, jnp.where(level, l, 0.0)) for i, l in zip(inv, lmats)]
        yield
        inv = [i - hdot(ti, i) for i, ti in zip(inv, t)]
        yield
    return inv


def _interleave(*stage_generators):
    live = list(stage_generators)
    while live:
        for gen in list(live):
            try:
                next(gen)
            except StopIteration:
                live.remove(gen)


def _softplus(x):
    return jnp.maximum(x, 0.0) + jnp.log1p(jnp.exp(-jnp.abs(x)))


def _gdn_body(h_ref, taps_ref, alog_ref, dtb_ref, nw_ref, o_ref,
              qn_s, kn_s, kb_s, vb_s, g_s, u_s, w_s, a_s, qd_s, kd_s, egl_s, o_s):
    s = h_ref.shape[1]
    c = GDN_CHUNK
    d = HEAD_DIM
    w = GDN_WIDTH
    nc = s // c

    bd_mask = (_iota2((w, w), 0) >> HEAD_SHIFT == _iota2((w, w), 1) >> HEAD_SHIFT).astype(BF16)

    def head_sum(t):
        return _dot_exact_rhs(t, bd_mask, 2)

    def l2norm(t):
        return t * lax.rsqrt(head_sum(t * t) + NORM_EPS)

    def conv_silu(slab):
        cols = slice(slab * w, (slab + 1) * w)
        y = _causal_dwconv(jnp.zeros((SUBLANES, w), F32), h_ref[0, :, cols], taps_ref[:, cols])
        return y * jax.nn.sigmoid(y)

    qn_s[...] = l2norm(conv_silu(0)) * (d ** -0.5)
    kn = l2norm(conv_silu(1))
    kn_s[...] = kn
    ab = h_ref[0, :, 4 * w:4 * w + LANES]
    src = _iota2((LANES, w), 0)
    head = _iota2((LANES, w), 1) >> HEAD_SHIFT
    g_small = -jnp.exp(alog_ref[...]) * _softplus(ab + dtb_ref[...])
    g_s[...] = _dot_exact_rhs(g_small, (src == head).astype(BF16), 3)
    beta = _dot_exact_rhs(jax.nn.sigmoid(ab), (src == head + GDN_HEADS).astype(BF16), 3)
    kb_s[...] = kn * beta
    vb_s[...] = conv_silu(2) * beta

    ri = _iota2((c, w), 0)
    ci = _iota2((c, w), 1) & (c - 1)
    tril = ci <= ri
    strict = ci < ri
    upper = (ri <= ci).astype(F32)
    inv_masks = _inverse_masks(c, w)
    tril_b = (_iota2((c, c), 1) <= _iota2((c, c), 0)).astype(BF16)
    ones_b = jnp.ones((c, c), BF16)

    grp = GDN_CHUNKS_PER_STEP
    n_groups = nc // grp

    def chunk_rows(n):
        return pl.ds(n * c if isinstance(n, int) else pl.multiple_of(n * c, c), c)

    def prep_stages(group):
        ns = [group * grp + i for i in range(grp)]
        rows = [chunk_rows(n) for n in ns]
        g = [g_s[r, :] for r in rows]
        gc = [_dot_exact_lhs(tril_b, x, 3) for x in g]
        gct = [_dot_exact_lhs(ones_b, x * upper, 3) for x in g]
        yield
        kn_c = [kn_s[r, :] for r in rows]
        kb_c = [kb_s[r, :] for r in rows]
        qn_c = [qn_s[r, :] for r in rows]
        kq = [_dot_nt(jnp.concatenate([kb, q], axis=0).astype(BF16), _block_diag(k.astype(BF16), bd_mask))
              for kb, q, k in zip(kb_c, qn_c, kn_c)]
        yield
        decay = [jnp.where(tril, jnp.exp(jnp.where(tril, a - b, 0.0)), 0.0) for a, b in zip(gc, gct)]
        lmat = [jnp.where(strict, x[0:c] * dk, 0.0) for x, dk in zip(kq, decay)]
        tinv = yield from _unit_lower_inverse(lmat, inv_masks, bd_mask)
        egc = [jnp.exp(x) for x in gc]
        rhs = [jnp.concatenate([_block_diag(vb_s[r, :].astype(BF16), bd_mask),
                                _block_diag((kb * e).astype(BF16), bd_mask)], axis=1)
               for r, kb, e in zip(rows, kb_c, egc)]
        uw = [_dot(t.astype(BF16), x) for t, x in zip(tinv, rhs)]
        yield
        for i, (n, r) in enumerate(zip(ns, rows)):
            glast = gc[i][c - 1:c, :]
            egl_s[n] = jnp.exp(glast)
            u_s[r, :] = uw[i][:, 0:w]
            w_s[r, :] = uw[i][:, w:2 * w].astype(BF16)
            a_s[r, :] = jnp.where(tril, kq[i][c:2 * c] * decay[i], 0.0).astype(BF16)
            qd_s[r, :] = (qn_c[i] * egc[i]).astype(BF16)
            kd_s[r, :] = (kn_c[i] * jnp.exp(glast - gc[i])).astype(BF16)

    lane_head = _iota2((d, w), 1) >> HEAD_SHIFT

    def scan_stages(group, state):
        for i in range(grp):
            n = group * grp + i
            rows = chunk_rows(n)
            st = state[0]
            st_bd = _block_diag(st.astype(BF16), bd_mask)
            ws_qs = _dot(jnp.concatenate([w_s[rows, :], qd_s[rows, :]], axis=0), st_bd)
            yield
            v_new = u_s[rows, :] - ws_qs[0:c]
            v_b = v_new.astype(BF16)
            o_s[rows, :] = ws_qs[c:2 * c] + _dot(a_s[rows, :], _block_diag(v_b, bd_mask))
            kv = _dot_tn(kd_s[rows, :], v_b)
            upd = jnp.zeros((d, w), F32)
            for h in range(GDN_HEADS):
                upd = upd + jnp.where(lane_head == h, kv[h * d:(h + 1) * d, :], 0.0)
            state[0] = st * egl_s[n] + upd
            yield

    _interleave(prep_stages(0))

    def group_body(group, st):
        state = [st]
        _interleave(prep_stages(group), scan_stages(group - 1, state))
        return state[0]

    state = [lax.fori_loop(1, n_groups, group_body, jnp.zeros((d, w), F32))]
    _interleave(scan_stages(n_groups - 1, state))

    o = o_s[...]
    o = o * lax.rsqrt(head_sum(o * o) * (1.0 / d) + NORM_EPS) * nw_ref[...]
    z = h_ref[0, :, 3 * w:4 * w]
    o_ref[0] = (o * (z * jax.nn.sigmoid(z))).astype(o_ref.dtype)


def _gated_deltanet(hg, conv_taps, alog_rep, dtb_rep, nw_rep):
    bsz, s, cols = hg.shape
    w = GDN_WIDTH
    assert GDN_CHUNK == HEAD_DIM and cols == GDN_COLS and s % (GDN_CHUNK * GDN_CHUNKS_PER_STEP) == 0
    assert conv_taps.shape[0] == GDN_CONV_K and GDN_CONV_K - 1 <= SUBLANES
    return pl.pallas_call(
        _gdn_body,
        grid=(bsz,),
        in_specs=[pl.BlockSpec((1, s, cols), lambda b: (b, 0, 0)),
                  _const_spec(conv_taps.shape),
                  _const_spec((1, LANES)), _const_spec((1, LANES)), _const_spec((1, w))],
        out_specs=pl.BlockSpec((1, s, w), lambda b: (b, 0, 0)),
        out_shape=jax.ShapeDtypeStruct((bsz, s, w), BF16),
        scratch_shapes=[pltpu.VMEM((s, w), F32)] * 6 + [pltpu.VMEM((s, w), BF16)] * 4 + [
            pltpu.VMEM((s // GDN_CHUNK, 1, w), F32), pltpu.VMEM((s, w), F32)],
        compiler_params=pltpu.CompilerParams(
            dimension_semantics=("parallel",), vmem_limit_bytes=V7X_VMEM_LIMIT_BYTES),
        name="gated_deltanet",
    )(hg, conv_taps, alog_rep, dtb_rep, nw_rep)


NSA_TQ = 256
NSA_SLC_STEP = 256
NSA_NCMP_PAD = 128
NSA_VPAD = 16


def _softmax_cols(s):
    m = jnp.max(s, axis=0, keepdims=True)
    e = jnp.exp(s - m)
    return e, jnp.sum(e, axis=0, keepdims=True)


def _nsa_body(q_ref, gate_ref, cmp_ref, kv_ref, pe_ref, w1_ref, w2_ref, ovlt_ref, wbias_ref, dbias_ref, o_ref,
              kc_s, vct_s, ks_s, kw_s, vst_s, vwt_s, oslc_s, owin_s):
    qi = pl.program_id(2)
    tq, d, grp = NSA_TQ, HEAD_DIM, NSA_GROUP
    s_len = kv_ref.shape[1]
    n_sel = s_len // SEL_BLOCK
    seg = CMP_LEN // CMP_STRIDE
    band = WINDOW + tq
    wide = grp * tq

    @pl.when(qi == 0)
    def _():
        def relayout(i, carry):
            r0 = pl.multiple_of(i * LANES, LANES)
            blk = kv_ref[0, pl.ds(r0, LANES), :]
            block_of_row = (r0 + _iota2((LANES, LANES - d), 0)) >> SEL_SHIFT
            one_hot = (block_of_row == _iota2((LANES, LANES - d), 1)).astype(BF16)
            ks_s[pl.ds(r0, LANES), :] = jnp.concatenate([blk[:, 0:d].astype(BF16), one_hot], axis=1)
            kw_s[pl.ds(r0, LANES), :] = blk[:, 2 * d:3 * d].astype(BF16)
            blk_t = blk.astype(F32).T
            ones_row = (_iota2((NSA_VPAD, LANES), 0) == 0).astype(BF16)
            vst_s[:, pl.ds(r0, LANES)] = jnp.concatenate([blk_t[d:2 * d, :].astype(BF16), ones_row], axis=0)
            vwt_s[:, pl.ds(r0, LANES)] = jnp.concatenate([blk_t[3 * d:4 * d, :].astype(BF16), ones_row], axis=0)
            return carry

        lax.fori_loop(0, s_len // LANES, relayout, 0)
        pre = [jnp.zeros((NSA_NCMP_PAD, 2 * LANES), F32) for _ in range(seg)]
        for l in range(CMP_STRIDE):
            t_l = cmp_ref[0, pl.ds(l, NSA_NCMP_PAD, stride=CMP_STRIDE), :]
            for h in range(seg):
                lh = l + h * CMP_STRIDE
                pre[h] = pre[h] + _dot((t_l + pe_ref[lh:lh + 1, :]).astype(BF16), w1_ref[lh])
        hid = pre[0] + pltpu.roll(pre[1], NSA_NCMP_PAD - 1, 0)
        hid = hid * jax.nn.sigmoid(hid)
        kcv = _dot(hid.astype(BF16), w2_ref[...])
        kc_s[...] = kcv[:, 0:d].astype(BF16)
        vct_s[...] = kcv.T[d:2 * d, :].astype(BF16)

    ts = qi * tq
    q_t = (q_ref[0].astype(F32) * (d ** -0.5)).T
    q_t = jnp.concatenate([q_t[g * d:(g + 1) * d, :] for g in range(grp)], axis=1).astype(BF16)
    t_row = ts + (_iota2((1, wide), 1) & (tq - 1))
    t_row1 = t_row[:, 0:tq]

    w0 = pl.multiple_of(jnp.maximum(ts - WINDOW, 0), tq)
    sc_win = _dot(kw_s[pl.ds(w0, band), :], q_t) + wbias_ref[jnp.minimum(qi, WINDOW // tq)]
    m_win = jnp.max(sc_win, axis=0, keepdims=True)

    n_col =_iota2((NSA_NCMP_PAD, 1), 0)
    cmp_ok = (n_col * CMP_STRIDE + (CMP_LEN - 1) <= t_row) & (n_col < NSA_NCMP_PAD - 1)
    e, den = _softmax_cols(jnp.where(cmp_ok, _dot(kc_s[...], q_t), NEG))
    p_cmp = jnp.where(t_row >= CMP_LEN - 1, e / den, 0.0)
    o_cmp = _dot(vct_s[...], p_cmp.astype(BF16))

    p_sum = p_cmp[:, 0:tq]
    for g in range(1, grp):
        p_sum = p_sum + p_cmp[:, g * tq:(g + 1) * tq]
    imp = _dot_exact_lhs(ovlt_ref[...], p_sum, 3)
    j_idx = _iota2((n_sel, tq), 0)
    q_blk = t_row1 >> SEL_SHIFT
    forced = (j_idx == 0) | (j_idx == q_blk) | (j_idx == q_blk - 1)
    imp = jnp.where(forced, imp + FORCE_BONUS, imp)
    causal_blk = j_idx <= q_blk
    imp = jnp.where(causal_blk, imp, NEG)
    rank = jnp.zeros((n_sel, tq), F32)
    for jp in range(n_sel):
        row = imp[jp:jp + 1, :]
        beats = (row > imp) | ((row == imp) & (j_idx > jp))
        rank = rank + beats.astype(F32)
    selected = (rank < SEL_TOPN) & causal_blk

    blk0 = ts >> SEL_SHIFT
    sel_bias = jnp.where(selected, 0.0, NEG)
    past_bias = jnp.where(j_idx < blk0, sel_bias, NEG)
    zero_rows = jnp.zeros((LANES - d - n_sel, wide), BF16)

    def with_bias(bias):
        return jnp.concatenate([q_t, jnp.concatenate([bias.astype(BF16)] * grp, axis=1), zero_rows], axis=0)

    sc_diag = _dot(ks_s[pl.ds(ts, tq), :], with_bias(sel_bias)) + dbias_ref[...]
    q_past = with_bias(past_bias)

    ch = NSA_SLC_STEP

    def weighted_values(value_t, sc, m):
        return _dot(value_t, jnp.exp(sc - m).astype(BF16))

    def normalised(acc):
        return acc[0:d] / acc[d:d + 1]

    def slc_and_win(nk):
        n_past, n_win = nk // ch, band // ch
        past_sc, win_acc = [], None
        for i in range(max(n_past, n_win)):
            if i < n_past:
                past_sc.append(_dot(ks_s[i * ch:(i + 1) * ch, :], q_past))
            if i < n_win:
                term = weighted_values(vwt_s[:, pl.ds(w0 + i * ch, ch)], sc_win[i * ch:(i + 1) * ch], m_win)
                win_acc = term if win_acc is None else win_acc + term
        owin_s[...] = normalised(win_acc)
        m = functools.reduce(jnp.maximum, [jnp.max(p, axis=0, keepdims=True) for p in past_sc + [sc_diag]])
        acc = None
        for i in range(n_past):
            term = weighted_values(vst_s[:, i * ch:(i + 1) * ch], past_sc[i], m)
            acc = term if acc is None else acc + term
        term = weighted_values(vst_s[:, pl.ds(ts, tq)], sc_diag, m)
        oslc_s[...] = normalised(term if acc is None else acc + term)

    n_var = (s_len - tq + NSA_SLC_STEP - 1) // NSA_SLC_STEP + 1
    variant = (ts + NSA_SLC_STEP - 1) // NSA_SLC_STEP
    for v in range(n_var):
        @pl.when(variant == v)
        def _(v=v):
            slc_and_win(min(v * NSA_SLC_STEP, s_len))
    o_slc, o_win = oslc_s[...], owin_s[...]

    gt = jax.nn.sigmoid(gate_ref[0]).T
    outs = []
    for g in range(grp):
        cols = slice(g * tq, (g + 1) * tq)
        outs.append(gt[g:g + 1, :] * o_cmp[:, cols] + gt[grp + g:grp + g + 1, :] * o_slc[:, cols]
                    + gt[2 * grp + g:2 * grp + g + 1, :] * o_win[:, cols])
    o_ref[0] = jnp.concatenate(outs, axis=0).T.astype(o_ref.dtype)


def _attention_biases(tq):
    q = jnp.arange(NSA_GROUP * tq) & (tq - 1)
    r = jnp.arange(WINDOW + tq)
    lead = jnp.arange(WINDOW // tq + 1) * tq
    dist = lead[:, None, None] + q[None, None, :] - r[None, :, None]
    win_bias = jnp.where((dist >= 0) & (dist < WINDOW), 0.0, NEG).astype(F32)
    diag_bias = jnp.where(jnp.arange(tq)[:, None] <= q[None, :], 0.0, NEG).astype(F32)
    return win_bias, diag_bias


def _nsa_attention(h_mm, h_f32, pe, w1, w2, overlap):
    bsz, s, _ = h_mm.shape
    tq, d = NSA_TQ, HEAD_DIM
    qw = NSA_GROUP * d
    assert h_mm.shape[2] == NSA_KV_HEADS * 2 * qw and h_f32.shape[2] == NSA_KV_HEADS * 2 * LANES
    assert s % tq == 0 and s >= WINDOW + tq and (WINDOW + tq) % NSA_SLC_STEP == 0 and d + s // SEL_BLOCK <= LANES
    assert WINDOW % tq == 0
    win_bias, diag_bias = _attention_biases(tq)
    return pl.pallas_call(
        _nsa_body,
        grid=(bsz, NSA_KV_HEADS, s // tq),
        in_specs=[
            pl.BlockSpec((1, tq, qw), lambda b, h, i: (b, i, 2 * h)),
            pl.BlockSpec((1, tq, LANES), lambda b, h, i: (b, i, 2 * h)),
            pl.BlockSpec((1, s, LANES), lambda b, h, i: (b, 0, 2 * h + 1)),
            pl.BlockSpec((1, s, qw), lambda b, h, i: (b, 0, 2 * h + 1)),
            _const_spec(pe.shape), _const_spec(w1.shape), _const_spec(w2.shape), _const_spec(overlap.shape),
            _const_spec(win_bias.shape), _const_spec(diag_bias.shape),
        ],
        out_specs=pl.BlockSpec((1, tq, qw), lambda b, h, i: (b, i, h)),
        out_shape=jax.ShapeDtypeStruct((bsz, s, NSA_WIDTH), BF16),
        scratch_shapes=[pltpu.VMEM((NSA_NCMP_PAD, d), BF16), pltpu.VMEM((d, NSA_NCMP_PAD), BF16),
                        pltpu.VMEM((s, LANES), BF16), pltpu.VMEM((s, d), BF16),
                        pltpu.VMEM((d + NSA_VPAD, s), BF16), pltpu.VMEM((d + NSA_VPAD, s), BF16),
                        pltpu.VMEM((d, NSA_GROUP * tq), F32), pltpu.VMEM((d, NSA_GROUP * tq), F32)],
        compiler_params=pltpu.CompilerParams(
            dimension_semantics=("parallel", "parallel", "arbitrary"),
            vmem_limit_bytes=V7X_VMEM_LIMIT_BYTES),
        name="nsa_attention",
    )(h_mm, h_f32, h_f32, h_mm, pe, w1, w2, overlap, win_bias, diag_bias)


def _prep_w_in(w):
    w = w.astype(BF16)
    lead = w.shape[:-1]
    sizes = (3 * CONV_WIDTH, 3 * GDN_WIDTH, GDN_WIDTH, GDN_HEADS, GDN_HEADS,
             NSA_WIDTH, 6 * NSA_KV_HEADS * HEAD_DIM, 3 * NSA_Q_HEADS)
    parts, start = [], 0
    for n in sizes:
        parts.append(w[..., start:start + n])
        start += n
    wconv, gqkv, gz, ga, gb, nq, nkv, ngate = parts
    pad = lambda n: jnp.zeros(lead + (n,), w.dtype)
    wgdn = jnp.concatenate([gqkv, gz, ga, gb, pad(LANES - 2 * GDN_HEADS)], axis=-1)
    nkv = nkv.reshape(lead + (6, NSA_KV_HEADS, HEAD_DIM))
    ngate = ngate.reshape(lead + (NSA_KV_HEADS, NSA_GROUP, 3))
    mm_cols, f32_cols = [], []
    for h in range(NSA_KV_HEADS):
        qw = NSA_GROUP * HEAD_DIM
        gates = jnp.swapaxes(ngate[..., h, :, :], -1, -2).reshape(lead + (3 * NSA_GROUP,))
        kv = nkv[..., h, :].reshape(lead + (6 * HEAD_DIM,))
        mm_cols += [nq[..., h * qw:(h + 1) * qw], kv[..., 2 * HEAD_DIM:]]
        f32_cols += [gates, pad(LANES - 3 * NSA_GROUP), kv[..., :2 * HEAD_DIM]]
    return wconv, wgdn, jnp.concatenate(mm_cols, axis=-1), jnp.concatenate(f32_cols, axis=-1)


def _prep_cmp_weights(pe_k, pe_v, k_w1, k_w2, v_w1, v_w2):
    d, hid = HEAD_DIM, k_w1.shape[1]
    pe = jnp.concatenate([pe_k, pe_v], axis=1)
    k1 = k_w1.reshape(CMP_LEN, d, hid)
    v1 = v_w1.reshape(CMP_LEN, d, hid)
    z1 = jnp.zeros_like(k1)
    w1 = jnp.concatenate([jnp.concatenate([k1, z1], axis=2),
                          jnp.concatenate([z1, v1], axis=2)], axis=1)
    z2 = jnp.zeros_like(k_w2)
    w2 = jnp.concatenate([jnp.concatenate([k_w2, z2], axis=1),
                          jnp.concatenate([z2, v_w2], axis=1)], axis=0)
    return pe, w1.astype(BF16), w2.astype(BF16)


def _overlap_matrix(s):
    n_cmp = (s - CMP_LEN) // CMP_STRIDE + 1
    t = jnp.arange(s)
    starts = jnp.arange(NSA_NCMP_PAD) * CMP_STRIDE
    cmp_tok = (t[None, :] >= starts[:, None]) & (t[None, :] < starts[:, None] + CMP_LEN)
    cmp_tok = cmp_tok & (jnp.arange(NSA_NCMP_PAD) < n_cmp)[:, None]
    sel_tok = (t[None, :] // SEL_BLOCK) == jnp.arange(s // SEL_BLOCK)[:, None]
    return ((sel_tok.astype(F32) @ cmp_tok.astype(F32).T) / CMP_LEN).astype(BF16)


def kernel(x, ffn1_w_gate, ffn1_w_up, ffn1_w_down, ln1_g, ln1_b, w_in, conv_w, gdn_conv_w, gdn_a_log, gdn_dt_bias, gdn_norm_w, cmp_pe_k, cmp_pe_v, cmp_k_w1, cmp_k_w2, cmp_v_w1, cmp_v_w2, w_out, ln2_g, ln2_b, ffn2_w_gate, ffn2_w_up, ffn2_w_down, ln3_g, ln3_b):
    bsz, s, dm = x.shape
    depth = w_in.shape[0]
    alpha = (2 * depth) ** 0.25
    m = bsz * s
    overlap = _overlap_matrix(s)
    row = lambda v: v.reshape(1, -1)
    rows = lambda v: v[:, None, :]
    bf = lambda v: v.astype(BF16)
    ffn1, ffn2 = (bf(ffn1_w_gate), bf(ffn1_w_up), bf(ffn1_w_down)), (bf(ffn2_w_gate), bf(ffn2_w_up), bf(ffn2_w_down))
    w_in_groups = _prep_w_in(w_in)
    wo = bf(w_out)
    conv_taps = jnp.swapaxes(conv_w, 1, 2)
    h = x.reshape(m, dm)
    for l in range(depth):
        h = _ffn_ln(h, *ffn1, rows(ln1_g), rows(ln1_b), layer=l, alpha=alpha)

        hc, hg, hn_mm, hn_f32 = _in_proj(h, w_in_groups, (F32, F32, BF16, F32), layer=l)
        y_b = _gated_deltanet(hg.reshape(bsz, s, -1), gdn_conv_w[l].T,
                              row(jnp.pad(gdn_a_log[l], (0, LANES - GDN_HEADS))),
                              row(jnp.pad(gdn_dt_bias[l], (0, LANES - GDN_HEADS))),
                              row(jnp.tile(gdn_norm_w[l], GDN_HEADS)))
        pe, w1, w2 = _prep_cmp_weights(cmp_pe_k[l], cmp_pe_v[l], cmp_k_w1[l], cmp_k_w2[l], cmp_v_w1[l], cmp_v_w2[l])
        y_c = _nsa_attention(hn_mm.reshape(bsz, s, -1), hn_f32.reshape(bsz, s, -1), pe, w1, w2, overlap)
        h = _out_proj_ln(h, hc, y_b.reshape(m, -1), y_c.reshape(m, -1), conv_taps, wo,
                         rows(ln2_g), rows(ln2_b), layer=l, alpha=alpha, seq_len=s)

        h = _ffn_ln(h, *ffn2, rows(ln3_g), rows(ln3_b), layer=l, alpha=alpha)
    return h.reshape(bsz, s, dm)
```

```python
import functools

import jax
import jax.numpy as jnp
from jax import lax
from jax.experimental import pallas as pl
from jax.experimental.pallas import tpu as pltpu

F32 = jnp.float32
BF16 = jnp.bfloat16

HEAD_DIM = 64
CONV_WIDTH = 256
CONV_K = 3
GDN_HEADS = 4
GDN_WIDTH = GDN_HEADS * HEAD_DIM
GDN_CONV_K = 4
GDN_CHUNK = 64
NSA_Q_HEADS = 8
NSA_KV_HEADS = 2
NSA_GROUP = NSA_Q_HEADS // NSA_KV_HEADS
NSA_WIDTH = NSA_Q_HEADS * HEAD_DIM
CMP_LEN = 32
CMP_STRIDE = 16
SEL_BLOCK = 64
SEL_TOPN = 8
WINDOW = 512
FORCE_BONUS = 1e3
LN_EPS = 1e-5
NORM_EPS = 1e-6
NEG = -1e30

V7X_VMEM_LIMIT_BYTES = 56 * 1024 * 1024
LANES = 128
HEAD_SHIFT = HEAD_DIM.bit_length() - 1
SEL_SHIFT = SEL_BLOCK.bit_length() - 1

GDN_COLS = 4 * GDN_WIDTH + LANES


def _layer_norm(r, g, b):
    mu = jnp.mean(r, axis=-1, keepdims=True)
    c = r - mu
    var = jnp.mean(c * c, axis=-1, keepdims=True)
    return c * lax.rsqrt(var + LN_EPS) * g + b


def _const_spec(shape):
    return pl.BlockSpec(shape, lambda *_: (0,) * len(shape), pipeline_mode=pl.Buffered(1))


def _layer_spec(shape, layer, block=None):
    index = (layer,) + tuple(block or (0,) * len(shape))
    return pl.BlockSpec((None,) + tuple(shape), lambda *_: index, pipeline_mode=pl.Buffered(1))


def _ffn_ln_body(x_ref, wg_ref, wu_ref, wd_ref, g_ref, b_ref, o_ref, *, alpha):
    x = x_ref[...]
    xb = x.astype(BF16)
    hg = jnp.dot(xb, wg_ref[...], preferred_element_type=F32)
    hu = jnp.dot(xb, wu_ref[...], preferred_element_type=F32)
    a = (hg * jax.nn.sigmoid(hg) * hu).astype(BF16)
    y = jnp.dot(a, wd_ref[...], preferred_element_type=F32)
    o_ref[...] = _layer_norm(alpha * x + 0.5 * y, g_ref[...], b_ref[...])


def _ffn_ln(x, wg, wu, wd, g, b, *, layer, alpha, tm=512):
    m, d = x.shape
    f = wg.shape[2]
    return pl.pallas_call(
        functools.partial(_ffn_ln_body, alpha=alpha),
        grid=(m // tm,),
        in_specs=[
            pl.BlockSpec((tm, d), lambda i: (i, 0)),
            _layer_spec((d, f), layer), _layer_spec((d, f), layer), _layer_spec((f, d), layer),
            _layer_spec((1, d), layer), _layer_spec((1, d), layer),
        ],
        out_specs=pl.BlockSpec((tm, d), lambda i: (i, 0)),
        out_shape=jax.ShapeDtypeStruct((m, d), F32),
        compiler_params=pltpu.CompilerParams(
            dimension_semantics=("parallel",), vmem_limit_bytes=V7X_VMEM_LIMIT_BYTES),
        name="ffn_ln",
    )(x, wg, wu, wd, g, b)


def _in_proj_body(x_ref, *refs):
    n = len(refs) // 2
    xb = x_ref[...].astype(BF16)
    for w_ref, o_ref in zip(refs[:n], refs[n:]):
        o_ref[...] = jnp.dot(xb, w_ref[...], preferred_element_type=F32).astype(o_ref.dtype)


def _in_proj(x, weights, out_dtypes, *, layer, tm=1024):
    m, d = x.shape
    widths = [w.shape[2] for w in weights]
    return pl.pallas_call(
        _in_proj_body,
        grid=(m // tm,),
        in_specs=[pl.BlockSpec((tm, d), lambda i: (i, 0))] + [_layer_spec((d, w), layer) for w in widths],
        out_specs=[pl.BlockSpec((tm, w), lambda i: (i, 0)) for w in widths],
        out_shape=[jax.ShapeDtypeStruct((m, w), dt) for w, dt in zip(widths, out_dtypes)],
        compiler_params=pltpu.CompilerParams(
            dimension_semantics=("parallel",), vmem_limit_bytes=V7X_VMEM_LIMIT_BYTES),
        name="in_proj",
    )(x, *weights)


SUBLANES = 8


def _out_proj_ln_body(x_ref, hc_ref, halo_ref, yb_ref, yc_ref, taps_ref, wa_ref, wb_ref, wc_ref, g_ref, b_ref,
                      o_ref, *, alpha, tiles_per_seq):
    c = CONV_WIDTH
    h = hc_ref[...]
    u = h[:, c:2 * c] * h[:, 2 * c:3 * c]
    hh = halo_ref[...]
    at_seq_start = pl.program_id(0) % tiles_per_seq == 0
    u_prev = jnp.where(at_seq_start, 0.0, hh[:, c:2 * c] * hh[:, 2 * c:3 * c])
    y_a = h[:, 0:c] * _causal_dwconv(u_prev, u, taps_ref[...])
    y = jnp.dot(y_a.astype(BF16), wa_ref[...], preferred_element_type=F32)
    y += jnp.dot(yb_ref[...].astype(BF16), wb_ref[...], preferred_element_type=F32)
    y += jnp.dot(yc_ref[...].astype(BF16), wc_ref[...], preferred_element_type=F32)
    o_ref[...] = _layer_norm(alpha * x_ref[...] + y, g_ref[...], b_ref[...])


def _out_proj_ln(x, hc, yb, yc, conv_taps, w_out, g, b, *, layer, alpha, seq_len, tm=1024):
    m, d = x.shape
    wa, wb, wc = CONV_WIDTH, yb.shape[1], yc.shape[1]
    assert seq_len % tm == 0 and CONV_K - 1 <= SUBLANES and wa == wb and wc == wa + wb
    row = lambda w: pl.BlockSpec((tm, w), lambda i: (i, 0))
    halo = pl.BlockSpec((SUBLANES, hc.shape[1]), lambda i: (jnp.maximum(i * (tm // SUBLANES) - 1, 0), 0))
    return pl.pallas_call(
        functools.partial(_out_proj_ln_body, alpha=alpha, tiles_per_seq=seq_len // tm),
        grid=(m // tm,),
        in_specs=[row(d), row(hc.shape[1]), halo, row(wb), row(wc),
                  _layer_spec(conv_taps.shape[1:], layer),
                  _layer_spec((wa, d), layer, (0, 0)), _layer_spec((wb, d), layer, (1, 0)),
                  _layer_spec((wc, d), layer, (1, 0)),
                  _layer_spec((1, d), layer), _layer_spec((1, d), layer)],
        out_specs=row(d),
        out_shape=jax.ShapeDtypeStruct((m, d), F32),
        compiler_params=pltpu.CompilerParams(
            dimension_semantics=("parallel",), vmem_limit_bytes=V7X_VMEM_LIMIT_BYTES),
        name="out_proj_ln",
    )(x, hc, hc, yb, yc, conv_taps, w_out, w_out, w_out, g, b)


def _causal_dwconv(prev, body, taps):
    ext = jnp.concatenate([prev, body], axis=0)
    k = taps.shape[0]
    y = body * taps[k - 1:k, :]
    for j in range(k - 1):
        y = y + pltpu.roll(ext, k - 1 - j, 0)[SUBLANES:, :] * taps[j:j + 1, :]
    return y


def _dot(a, b):
    return jnp.dot(a, b, preferred_element_type=F32)


def _dot_nt(a, b):
    return lax.dot_general(a, b, (((1,), (1,)), ((), ())), preferred_element_type=F32)


def _dot_tn(a, b):
    return lax.dot_general(a, b, (((0,), (0,)), ((), ())), preferred_element_type=F32)


def _split_bf16(a, terms):
    parts, rest = [], a
    for _ in range(terms):
        p = rest.astype(BF16)
        parts.append(p)
        rest = rest - p.astype(F32)
    return parts


def _dot_exact_lhs(m_bf16, a, terms):
    return sum(_dot(m_bf16, p) for p in _split_bf16(a, terms))


def _dot_exact_rhs(a, m_bf16, terms):
    return sum(_dot(p, m_bf16) for p in _split_bf16(a, terms))


def _iota2(shape, dim):
    return lax.broadcasted_iota(jnp.int32, shape, dim)


GDN_INV_BASE = 8
GDN_CHUNKS_PER_STEP = 4


def _block_diag(x, bd_mask):
    return jnp.concatenate([x] * (x.shape[1] // x.shape[0]), axis=0) * bd_mask


def _heads_dot_hl(a, b, bd_mask):
    a_hi, a_lo = _split_bf16(a, 2)
    b_hi, b_lo = _split_bf16(b, 2)
    r = a.shape[0]
    hi = _dot(jnp.concatenate([a_hi, a_lo], axis=0), _block_diag(b_hi, bd_mask))
    return hi[0:r] + hi[r:2 * r] + _dot(a_hi, _block_diag(b_lo, bd_mask))


def _inverse_masks(c, width):
    ri, ci = _iota2((c, width), 0), _iota2((c, width), 1) & (c - 1)
    base = GDN_INV_BASE.bit_length() - 1
    eye = (ci == ri).astype(F32)
    diag = (ri >> base) == (ci >> base)
    levels = [((ri >> (s + 1)) == (ci >> (s + 1))) & ((ri >> s) != (ci >> s))
              for s in range(base, c.bit_length() - 1)]
    return eye, diag, levels


def _unit_lower_inverse(lmats, masks, bd_mask):
    eye, diag, levels = masks
    c = lmats[0].shape[0]
    hdot = lambda a, b: _heads_dot_hl(a, b, bd_mask)
    l0 = [jnp.where(diag, l, 0.0) for l in lmats]
    p = [eye - x for x in l0]
    m = [hdot(x, x) for x in l0]
    yield
    pm = [hdot(jnp.concatenate([pi, mi], axis=0), mi) for pi, mi in zip(p, m)]
    yield
    p = [pi + x[0:c] for pi, x in zip(p, pm)]
    inv = [pi + hdot(pi, x[c:2 * c]) for pi, x in zip(p, pm)]
    yield
    for level in levels:
        t = [hdot(i, jnp.where(level, l, 0.0)) for i, l in zip(inv, lmats)]
        yield
        inv = [i - hdot(ti, i) for i, ti in zip(inv, t)]
        yield
    return inv


def _interleave(*stage_generators):
    live = list(stage_generators)
    while live:
        for gen in list(live):
            try:
                next(gen)
            except StopIteration:
                live.remove(gen)


def _softplus(x):
    return jnp.maximum(x, 0.0) + jnp.log1p(jnp.exp(-jnp.abs(x)))


def _gdn_body(h_ref, taps_ref, alog_ref, dtb_ref, nw_ref, o_ref,
              qn_s, kn_s, kb_s, vb_s, g_s, u_s, w_s, a_s, qd_s, kd_s, egl_s, o_s):
    s = h_ref.shape[1]
    c = GDN_CHUNK
    d = HEAD_DIM
    w = GDN_WIDTH
    nc = s // c

    bd_mask = (_iota2((w, w), 0) >> HEAD_SHIFT == _iota2((w, w), 1) >> HEAD_SHIFT).astype(BF16)

    def head_sum(t):
        return _dot_exact_rhs(t, bd_mask, 2)

    def l2norm(t):
        return t * lax.rsqrt(head_sum(t * t) + NORM_EPS)

    def conv_silu(slab):
        cols = slice(slab * w, (slab + 1) * w)
        y = _causal_dwconv(jnp.zeros((SUBLANES, w), F32), h_ref[0, :, cols], taps_ref[:, cols])
        return y * jax.nn.sigmoid(y)

    qn_s[...] = l2norm(conv_silu(0)) * (d ** -0.5)
    kn = l2norm(conv_silu(1))
    kn_s[...] = kn
    ab = h_ref[0, :, 4 * w:4 * w + LANES]
    src = _iota2((LANES, w), 0)
    head = _iota2((LANES, w), 1) >> HEAD_SHIFT
    g_small = -jnp.exp(alog_ref[...]) * _softplus(ab + dtb_ref[...])
    g_s[...] = _dot_exact_rhs(g_small, (src == head).astype(BF16), 3)
    beta = _dot_exact_rhs(jax.nn.sigmoid(ab), (src == head + GDN_HEADS).astype(BF16), 3)
    kb_s[...] = kn * beta
    vb_s[...] = conv_silu(2) * beta

    ri = _iota2((c, w), 0)
    ci = _iota2((c, w), 1) & (c - 1)
    tril = ci <= ri
    strict = ci < ri
    upper = (ri <= ci).astype(F32)
    inv_masks = _inverse_masks(c, w)
    tril_b = (_iota2((c, c), 1) <= _iota2((c, c), 0)).astype(BF16)
    ones_b = jnp.ones((c, c), BF16)

    grp = GDN_CHUNKS_PER_STEP
    n_groups = nc // grp

    def chunk_rows(n):
        return pl.ds(n * c if isinstance(n, int) else pl.multiple_of(n * c, c), c)

    def prep_stages(group):
        ns = [group * grp + i for i in range(grp)]
        rows = [chunk_rows(n) for n in ns]
        g = [g_s[r, :] for r in rows]
        gc = [_dot_exact_lhs(tril_b, x, 3) for x in g]
        gct = [_dot_exact_lhs(ones_b, x * upper, 3) for x in g]
        yield
        kn_c = [kn_s[r, :] for r in rows]
        kb_c = [kb_s[r, :] for r in rows]
        qn_c = [qn_s[r, :] for r in rows]
        kq = [_dot_nt(jnp.concatenate([kb, q], axis=0).astype(BF16), _block_diag(k.astype(BF16), bd_mask))
              for kb, q, k in zip(kb_c, qn_c, kn_c)]
        yield
        decay = [jnp.where(tril, jnp.exp(jnp.where(tril, a - b, 0.0)), 0.0) for a, b in zip(gc, gct)]
        lmat = [jnp.where(strict, x[0:c] * dk, 0.0) for x, dk in zip(kq, decay)]
        tinv = yield from _unit_lower_inverse(lmat, inv_masks, bd_mask)
        egc = [jnp.exp(x) for x in gc]
        rhs = [jnp.concatenate([_block_diag(vb_s[r, :].astype(BF16), bd_mask),
                                _block_diag((kb * e).astype(BF16), bd_mask)], axis=1)
               for r, kb, e in zip(rows, kb_c, egc)]
        uw = [_dot(t.astype(BF16), x) for t, x in zip(tinv, rhs)]
        yield
        for i, (n, r) in enumerate(zip(ns, rows)):
            glast = gc[i][c - 1:c, :]
            egl_s[n] = jnp.exp(glast)
            u_s[r, :] = uw[i][:, 0:w]
            w_s[r, :] = uw[i][:, w:2 * w].astype(BF16)
            a_s[r, :] = jnp.where(tril, kq[i][c:2 * c] * decay[i], 0.0).astype(BF16)
            qd_s[r, :] = (qn_c[i] * egc[i]).astype(BF16)
            kd_s[r, :] = (kn_c[i] * jnp.exp(glast - gc[i])).astype(BF16)

    lane_head = _iota2((d, w), 1) >> HEAD_SHIFT

    def scan_stages(group, state):
        for i in range(grp):
            n = group * grp + i
            rows = chunk_rows(n)
            st = state[0]
            st_bd = _block_diag(st.astype(BF16), bd_mask)
            ws_qs = _dot(jnp.concatenate([w_s[rows, :], qd_s[rows, :]], axis=0), st_bd)
            yield
            v_new = u_s[rows, :] - ws_qs[0:c]
            v_b = v_new.astype(BF16)
            o_s[rows, :] = ws_qs[c:2 * c] + _dot(a_s[rows, :], _block_diag(v_b, bd_mask))
            kv = _dot_tn(kd_s[rows, :], v_b)
            upd = jnp.zeros((d, w), F32)
            for h in range(GDN_HEADS):
                upd = upd + jnp.where(lane_head == h, kv[h * d:(h + 1) * d, :], 0.0)
            state[0] = st * egl_s[n] + upd
            yield

    _interleave(prep_stages(0))

    def group_body(group, st):
        state = [st]
        _interleave(prep_stages(group), scan_stages(group - 1, state))
        return state[0]

    state = [lax.fori_loop(1, n_groups, group_body, jnp.zeros((d, w), F32))]
    _interleave(scan_stages(n_groups - 1, state))

    o = o_s[...]
    o = o * lax.rsqrt(head_sum(o * o) * (1.0 / d) + NORM_EPS) * nw_ref[...]
    z = h_ref[0, :, 3 * w:4 * w]
    o_ref[0] = (o * (z * jax.nn.sigmoid(z))).astype(o_ref.dtype)


def _gated_deltanet(hg, conv_taps, alog_rep, dtb_rep, nw_rep):
    bsz, s, cols = hg.shape
    w = GDN_WIDTH
    assert GDN_CHUNK == HEAD_DIM and cols == GDN_COLS and s % (GDN_CHUNK * GDN_CHUNKS_PER_STEP) == 0
    assert conv_taps.shape[0] == GDN_CONV_K and GDN_CONV_K - 1 <= SUBLANES
    return pl.pallas_call(
        _gdn_body,
        grid=(bsz,),
        in_specs=[pl.BlockSpec((1, s, cols), lambda b: (b, 0, 0)),
                  _const_spec(conv_taps.shape),
                  _const_spec((1, LANES)), _const_spec((1, LANES)), _const_spec((1, w))],
        out_specs=pl.BlockSpec((1, s, w), lambda b: (b, 0, 0)),
        out_shape=jax.ShapeDtypeStruct((bsz, s, w), BF16),
        scratch_shapes=[pltpu.VMEM((s, w), F32)] * 6 + [pltpu.VMEM((s, w), BF16)] * 4 + [
            pltpu.VMEM((s // GDN_CHUNK, 1, w), F32), pltpu.VMEM((s, w), F32)],
        compiler_params=pltpu.CompilerParams(
            dimension_semantics=("parallel",), vmem_limit_bytes=V7X_VMEM_LIMIT_BYTES),
        name="gated_deltanet",
    )(hg, conv_taps, alog_rep, dtb_rep, nw_rep)


NSA_TQ = 256
NSA_SLC_STEP = 256
NSA_NCMP_PAD = 128
NSA_VPAD = 16


def _softmax_cols(s):
    m = jnp.max(s, axis=0, keepdims=True)
    e = jnp.exp(s - m)
    return e, jnp.sum(e, axis=0, keepdims=True)


def _nsa_body(q_ref, gate_ref, cmp_ref, kv_ref, pe_ref, w1_ref, w2_ref, ovlt_ref, wbias_ref, dbias_ref, o_ref,
              kc_s, vct_s, ks_s, kw_s, vst_s, vwt_s, oslc_s, owin_s):
    qi = pl.program_id(2)
    tq, d, grp = NSA_TQ, HEAD_DIM, NSA_GROUP
    s_len = kv_ref.shape[1]
    n_sel = s_len // SEL_BLOCK
    seg = CMP_LEN // CMP_STRIDE
    band = WINDOW + tq
    wide = grp * tq

    @pl.when(qi == 0)
    def _():
        def relayout(i, carry):
            r0 = pl.multiple_of(i * LANES, LANES)
            blk = kv_ref[0, pl.ds(r0, LANES), :]
            block_of_row = (r0 + _iota2((LANES, LANES - d), 0)) >> SEL_SHIFT
            one_hot = (block_of_row == _iota2((LANES, LANES - d), 1)).astype(BF16)
            ks_s[pl.ds(r0, LANES), :] = jnp.concatenate([blk[:, 0:d].astype(BF16), one_hot], axis=1)
            kw_s[pl.ds(r0, LANES), :] = blk[:, 2 * d:3 * d].astype(BF16)
            blk_t = blk.astype(F32).T
            ones_row = (_iota2((NSA_VPAD, LANES), 0) == 0).astype(BF16)
            vst_s[:, pl.ds(r0, LANES)] = jnp.concatenate([blk_t[d:2 * d, :].astype(BF16), ones_row], axis=0)
            vwt_s[:, pl.ds(r0, LANES)] = jnp.concatenate([blk_t[3 * d:4 * d, :].astype(BF16), ones_row], axis=0)
            return carry

        lax.fori_loop(0, s_len // LANES, relayout, 0)
        pre = [jnp.zeros((NSA_NCMP_PAD, 2 * LANES), F32) for _ in range(seg)]
        for l in range(CMP_STRIDE):
            t_l = cmp_ref[0, pl.ds(l, NSA_NCMP_PAD, stride=CMP_STRIDE), :]
            for h in range(seg):
                lh = l + h * CMP_STRIDE
                pre[h] = pre[h] + _dot((t_l + pe_ref[lh:lh + 1, :]).astype(BF16), w1_ref[lh])
        hid = pre[0] + pltpu.roll(pre[1], NSA_NCMP_PAD - 1, 0)
        hid = hid * jax.nn.sigmoid(hid)
        kcv = _dot(hid.astype(BF16), w2_ref[...])
        kc_s[...] = kcv[:, 0:d].astype(BF16)
        vct_s[...] = kcv.T[d:2 * d, :].astype(BF16)

    ts = qi * tq
    q_t = (q_ref[0].astype(F32) * (d ** -0.5)).T
    q_t = jnp.concatenate([q_t[g * d:(g + 1) * d, :] for g in range(grp)], axis=1).astype(BF16)
    t_row = ts + (_iota2((1, wide), 1) & (tq - 1))
    t_row1 = t_row[:, 0:tq]

    w0 = pl.multiple_of(jnp.maximum(ts - WINDOW, 0), tq)
    sc_win = _dot(kw_s[pl.ds(w0, band), :], q_t) + wbias_ref[jnp.minimum(qi, WINDOW // tq)]
    m_win = jnp.max(sc_win, axis=0, keepdims=True)

    n_col =_iota2((NSA_NCMP_PAD, 1), 0)
    cmp_ok = (n_col * CMP_STRIDE + (CMP_LEN - 1) <= t_row) & (n_col < NSA_NCMP_PAD - 1)
    e, den = _softmax_cols(jnp.where(cmp_ok, _dot(kc_s[...], q_t), NEG))
    p_cmp = jnp.where(t_row >= CMP_LEN - 1, e / den, 0.0)
    o_cmp = _dot(vct_s[...], p_cmp.astype(BF16))

    p_sum = p_cmp[:, 0:tq]
    for g in range(1, grp):
        p_sum = p_sum + p_cmp[:, g * tq:(g + 1) * tq]
    imp = _dot_exact_lhs(ovlt_ref[...], p_sum, 3)
    j_idx = _iota2((n_sel, tq), 0)
    q_blk = t_row1 >> SEL_SHIFT
    forced = (j_idx == 0) | (j_idx == q_blk) | (j_idx == q_blk - 1)
    imp = jnp.where(forced, imp + FORCE_BONUS, imp)
    causal_blk = j_idx <= q_blk
    imp = jnp.where(causal_blk, imp, NEG)
    rank = jnp.zeros((n_sel, tq), F32)
    for jp in range(n_sel):
        row = imp[jp:jp + 1, :]
        beats = (row > imp) | ((row == imp) & (j_idx > jp))
        rank = rank + beats.astype(F32)
    selected = (rank < SEL_TOPN) & causal_blk

    blk0 = ts >> SEL_SHIFT
    sel_bias = jnp.where(selected, 0.0, NEG)
    past_bias = jnp.where(j_idx < blk0, sel_bias, NEG)
    zero_rows = jnp.zeros((LANES - d - n_sel, wide), BF16)

    def with_bias(bias):
        return jnp.concatenate([q_t, jnp.concatenate([bias.astype(BF16)] * grp, axis=1), zero_rows], axis=0)

    sc_diag = _dot(ks_s[pl.ds(ts, tq), :], with_bias(sel_bias)) + dbias_ref[...]
    q_past = with_bias(past_bias)

    ch = NSA_SLC_STEP

    def weighted_values(value_t, sc, m):
        return _dot(value_t, jnp.exp(sc - m).astype(BF16))

    def normalised(acc):
        return acc[0:d] / acc[d:d + 1]

    def slc_and_win(nk):
        n_past, n_win = nk // ch, band // ch
        past_sc, win_acc = [], None
        for i in range(max(n_past, n_win)):
            if i < n_past:
                past_sc.append(_dot(ks_s[i * ch:(i + 1) * ch, :], q_past))
            if i < n_win:
                term = weighted_values(vwt_s[:, pl.ds(w0 + i * ch, ch)], sc_win[i * ch:(i + 1) * ch], m_win)
                win_acc = term if win_acc is None else win_acc + term
        owin_s[...] = normalised(win_acc)
        m = functools.reduce(jnp.maximum, [jnp.max(p, axis=0, keepdims=True) for p in past_sc + [sc_diag]])
        acc = None
        for i in range(n_past):
            term = weighted_values(vst_s[:, i * ch:(i + 1) * ch], past_sc[i], m)
            acc = term if acc is None else acc + term
        term = weighted_values(vst_s[:, pl.ds(ts, tq)], sc_diag, m)
        oslc_s[...] = normalised(term if acc is None else acc + term)

    n_var = (s_len - tq + NSA_SLC_STEP - 1) // NSA_SLC_STEP + 1
    variant = (ts + NSA_SLC_STEP - 1) // NSA_SLC_STEP
    for v in range(n_var):
        @pl.when(variant == v)
        def _(v=v):
            slc_and_win(min(v * NSA_SLC_STEP, s_len))
    o_slc, o_win = oslc_s[...], owin_s[...]

    gt = jax.nn.sigmoid(gate_ref[0]).T
    outs = []
    for g in range(grp):
        cols = slice(g * tq, (g + 1) * tq)
        outs.append(gt[g:g + 1, :] * o_cmp[:, cols] + gt[grp + g:grp + g + 1, :] * o_slc[:, cols]
                    + gt[2 * grp + g:2 * grp + g + 1, :] * o_win[:, cols])
    o_ref[0] = jnp.concatenate(outs, axis=0).T.astype(o_ref.dtype)


def _attention_biases(tq):
    q = jnp.arange(NSA_GROUP * tq) & (tq - 1)
    r = jnp.arange(WINDOW + tq)
    lead = jnp.arange(WINDOW // tq + 1) * tq
    dist = lead[:, None, None] + q[None, None, :] - r[None, :, None]
    win_bias = jnp.where((dist >= 0) & (dist < WINDOW), 0.0, NEG).astype(F32)
    diag_bias = jnp.where(jnp.arange(tq)[:, None] <= q[None, :], 0.0, NEG).astype(F32)
    return win_bias, diag_bias


def _nsa_attention(h_mm, h_f32, pe, w1, w2, overlap):
    bsz, s, _ = h_mm.shape
    tq, d = NSA_TQ, HEAD_DIM
    qw = NSA_GROUP * d
    assert h_mm.shape[2] == NSA_KV_HEADS * 2 * qw and h_f32.shape[2] == NSA_KV_HEADS * 2 * LANES
    assert s % tq == 0 and s >= WINDOW + tq and (WINDOW + tq) % NSA_SLC_STEP == 0 and d + s // SEL_BLOCK <= LANES
    assert WINDOW % tq == 0
    win_bias, diag_bias = _attention_biases(tq)
    return pl.pallas_call(
        _nsa_body,
        grid=(bsz, NSA_KV_HEADS, s // tq),
        in_specs=[
            pl.BlockSpec((1, tq, qw), lambda b, h, i: (b, i, 2 * h)),
            pl.BlockSpec((1, tq, LANES), lambda b, h, i: (b, i, 2 * h)),
            pl.BlockSpec((1, s, LANES), lambda b, h, i: (b, 0, 2 * h + 1)),
            pl.BlockSpec((1, s, qw), lambda b, h, i: (b, 0, 2 * h + 1)),
            _const_spec(pe.shape), _const_spec(w1.shape), _const_spec(w2.shape), _const_spec(overlap.shape),
            _const_spec(win_bias.shape), _const_spec(diag_bias.shape),
        ],
        out_specs=pl.BlockSpec((1, tq, qw), lambda b, h, i: (b, i, h)),
        out_shape=jax.ShapeDtypeStruct((bsz, s, NSA_WIDTH), BF16),
        scratch_shapes=[pltpu.VMEM((NSA_NCMP_PAD, d), BF16), pltpu.VMEM((d, NSA_NCMP_PAD), BF16),
                        pltpu.VMEM((s, LANES), BF16), pltpu.VMEM((s, d), BF16),
                        pltpu.VMEM((d + NSA_VPAD, s), BF16), pltpu.VMEM((d + NSA_VPAD, s), BF16),
                        pltpu.VMEM((d, NSA_GROUP * tq), F32), pltpu.VMEM((d, NSA_GROUP * tq), F32)],
        compiler_params=pltpu.CompilerParams(
            dimension_semantics=("parallel", "parallel", "arbitrary"),
            vmem_limit_bytes=V7X_VMEM_LIMIT_BYTES),
        name="nsa_attention",
    )(h_mm, h_f32, h_f32, h_mm, pe, w1, w2, overlap, win_bias, diag_bias)


def _prep_w_in(w):
    w = w.astype(BF16)
    lead = w.shape[:-1]
    sizes = (3 * CONV_WIDTH, 3 * GDN_WIDTH, GDN_WIDTH, GDN_HEADS, GDN_HEADS,
             NSA_WIDTH, 6 * NSA_KV_HEADS * HEAD_DIM, 3 * NSA_Q_HEADS)
    parts, start = [], 0
    for n in sizes:
        parts.append(w[..., start:start + n])
        start += n
    wconv, gqkv, gz, ga, gb, nq, nkv, ngate = parts
    pad = lambda n: jnp.zeros(lead + (n,), w.dtype)
    wgdn = jnp.concatenate([gqkv, gz, ga, gb, pad(LANES - 2 * GDN_HEADS)], axis=-1)
    nkv = nkv.reshape(lead + (6, NSA_KV_HEADS, HEAD_DIM))
    ngate = ngate.reshape(lead + (NSA_KV_HEADS, NSA_GROUP, 3))
    mm_cols, f32_cols = [], []
    for h in range(NSA_KV_HEADS):
        qw = NSA_GROUP * HEAD_DIM
        gates = jnp.swapaxes(ngate[..., h, :, :], -1, -2).reshape(lead + (3 * NSA_GROUP,))
        kv = nkv[..., h, :].reshape(lead + (6 * HEAD_DIM,))
        mm_cols += [nq[..., h * qw:(h + 1) * qw], kv[..., 2 * HEAD_DIM:]]
        f32_cols += [gates, pad(LANES - 3 * NSA_GROUP), kv[..., :2 * HEAD_DIM]]
    return wconv, wgdn, jnp.concatenate(mm_cols, axis=-1), jnp.concatenate(f32_cols, axis=-1)


def _prep_cmp_weights(pe_k, pe_v, k_w1, k_w2, v_w1, v_w2):
    d, hid = HEAD_DIM, k_w1.shape[1]
    pe = jnp.concatenate([pe_k, pe_v], axis=1)
    k1 = k_w1.reshape(CMP_LEN, d, hid)
    v1 = v_w1.reshape(CMP_LEN, d, hid)
    z1 = jnp.zeros_like(k1)
    w1 = jnp.concatenate([jnp.concatenate([k1, z1], axis=2),
                          jnp.concatenate([z1, v1], axis=2)], axis=1)
    z2 = jnp.zeros_like(k_w2)
    w2 = jnp.concatenate([jnp.concatenate([k_w2, z2], axis=1),
                          jnp.concatenate([z2, v_w2], axis=1)], axis=0)
    return pe, w1.astype(BF16), w2.astype(BF16)


def _overlap_matrix(s):
    n_cmp = (s - CMP_LEN) // CMP_STRIDE + 1
    t = jnp.arange(s)
    starts = jnp.arange(NSA_NCMP_PAD) * CMP_STRIDE
    cmp_tok = (t[None, :] >= starts[:, None]) & (t[None, :] < starts[:, None] + CMP_LEN)
    cmp_tok = cmp_tok & (jnp.arange(NSA_NCMP_PAD) < n_cmp)[:, None]
    sel_tok = (t[None, :] // SEL_BLOCK) == jnp.arange(s // SEL_BLOCK)[:, None]
    return ((sel_tok.astype(F32) @ cmp_tok.astype(F32).T) / CMP_LEN).astype(BF16)


def kernel(x, ffn1_w_gate, ffn1_w_up, ffn1_w_down, ln1_g, ln1_b, w_in, conv_w, gdn_conv_w, gdn_a_log, gdn_dt_bias, gdn_norm_w, cmp_pe_k, cmp_pe_v, cmp_k_w1, cmp_k_w2, cmp_v_w1, cmp_v_w2, w_out, ln2_g, ln2_b, ffn2_w_gate, ffn2_w_up, ffn2_w_down, ln3_g, ln3_b):
    bsz, s, dm = x.shape
    depth = w_in.shape[0]
    alpha = (2 * depth) ** 0.25
    m = bsz * s
    overlap = _overlap_matrix(s)
    row = lambda v: v.reshape(1, -1)
    rows = lambda v: v[:, None, :]
    bf = lambda v: v.astype(BF16)
    ffn1, ffn2 = (bf(ffn1_w_gate), bf(ffn1_w_up), bf(ffn1_w_down)), (bf(ffn2_w_gate), bf(ffn2_w_up), bf(ffn2_w_down))
    w_in_groups = _prep_w_in(w_in)
    wo = bf(w_out)
    conv_taps = jnp.swapaxes(conv_w, 1, 2)
    h = x.reshape(m, dm)
    for l in range(depth):
        h = _ffn_ln(h, *ffn1, rows(ln1_g), rows(ln1_b), layer=l, alpha=alpha)

        hc, hg, hn_mm, hn_f32 = _in_proj(h, w_in_groups, (F32, F32, BF16, F32), layer=l)
        y_b = _gated_deltanet(hg.reshape(bsz, s, -1), gdn_conv_w[l].T,
                              row(jnp.pad(gdn_a_log[l], (0, LANES - GDN_HEADS))),
                              row(jnp.pad(gdn_dt_bias[l], (0, LANES - GDN_HEADS))),
                              row(jnp.tile(gdn_norm_w[l], GDN_HEADS)))
        pe, w1, w2 = _prep_cmp_weights(cmp_pe_k[l], cmp_pe_v[l], cmp_k_w1[l], cmp_k_w2[l], cmp_v_w1[l], cmp_v_w2[l])
        y_c = _nsa_attention(hn_mm.reshape(bsz, s, -1), hn_f32.reshape(bsz, s, -1), pe, w1, w2, overlap)
        h = _out_proj_ln(h, hc, y_b.reshape(m, -1), y_c.reshape(m, -1), conv_taps, wo,
                         rows(ln2_g), rows(ln2_b), layer=l, alpha=alpha, seq_len=s)

        h = _ffn_ln(h, *ffn2, rows(ln3_g), rows(ln3_b), layer=l, alpha=alpha)
    return h.reshape(bsz, s, dm)
```

```python
import functools

import jax
import jax.numpy as jnp
from jax import lax
from jax.experimental import pallas as pl
from jax.experimental.pallas import tpu as pltpu

F32 = jnp.float32
BF16 = jnp.bfloat16

HEAD_DIM = 64
CONV_WIDTH = 256
CONV_K = 3
GDN_HEADS = 4
GDN_WIDTH = GDN_HEADS * HEAD_DIM
GDN_CONV_K = 4
GDN_CHUNK = 64
NSA_Q_HEADS = 8
NSA_KV_HEADS = 2
NSA_GROUP = NSA_Q_HEADS // NSA_KV_HEADS
NSA_WIDTH = NSA_Q_HEADS * HEAD_DIM
CMP_LEN = 32
CMP_STRIDE = 16
SEL_BLOCK = 64
SEL_TOPN = 8
WINDOW = 512
FORCE_BONUS = 1e3
LN_EPS = 1e-5
NORM_EPS = 1e-6
NEG = -1e30

V7X_VMEM_LIMIT_BYTES = 56 * 1024 * 1024
LANES = 128
HEAD_SHIFT = HEAD_DIM.bit_length() - 1
SEL_SHIFT = SEL_BLOCK.bit_length() - 1

GDN_COLS = 4 * GDN_WIDTH + LANES


def _layer_norm(r, g, b):
    mu = jnp.mean(r, axis=-1, keepdims=True)
    c = r - mu
    var = jnp.mean(c * c, axis=-1, keepdims=True)
    return c * lax.rsqrt(var + LN_EPS) * g + b


def _const_spec(shape):
    return pl.BlockSpec(shape, lambda *_: (0,) * len(shape), pipeline_mode=pl.Buffered(1))


def _layer_spec(shape, layer, block=None):
    index = (layer,) + tuple(block or (0,) * len(shape))
    return pl.BlockSpec((None,) + tuple(shape), lambda *_: index, pipeline_mode=pl.Buffered(1))


def _ffn_ln_value(x, wg_ref, wu_ref, wd_ref, g_ref, b_ref, alpha):
    xb = x.astype(BF16)
    hg = jnp.dot(xb, wg_ref[...], preferred_element_type=F32)
    hu = jnp.dot(xb, wu_ref[...], preferred_element_type=F32)
    a = (hg * jax.nn.sigmoid(hg) * hu).astype(BF16)
    y = jnp.dot(a, wd_ref[...], preferred_element_type=F32)
    return _layer_norm(alpha * x + 0.5 * y, g_ref[...], b_ref[...])


def _ffn_ln_body(x_ref, wg_ref, wu_ref, wd_ref, g_ref, b_ref, o_ref, *, alpha):
    o_ref[...] = _ffn_ln_value(x_ref[...], wg_ref, wu_ref, wd_ref, g_ref, b_ref, alpha)


def _ffn_ln(x, wg, wu, wd, g, b, *, layer, alpha, tm=512):
    m, d = x.shape
    f = wg.shape[2]
    return pl.pallas_call(
        functools.partial(_ffn_ln_body, alpha=alpha),
        grid=(m // tm,),
        in_specs=[
            pl.BlockSpec((tm, d), lambda i: (i, 0)),
            _layer_spec((d, f), layer), _layer_spec((d, f), layer), _layer_spec((f, d), layer),
            _layer_spec((1, d), layer), _layer_spec((1, d), layer),
        ],
        out_specs=pl.BlockSpec((tm, d), lambda i: (i, 0)),
        out_shape=jax.ShapeDtypeStruct((m, d), F32),
        compiler_params=pltpu.CompilerParams(
            dimension_semantics=("parallel",), vmem_limit_bytes=V7X_VMEM_LIMIT_BYTES),
        name="ffn_ln",
    )(x, wg, wu, wd, g, b)


def _in_proj_body(x_ref, *refs):
    n = len(refs) // 2
    xb = x_ref[...].astype(BF16)
    for w_ref, o_ref in zip(refs[:n], refs[n:]):
        o_ref[...] = jnp.dot(xb, w_ref[...], preferred_element_type=F32).astype(o_ref.dtype)


def _in_proj(x, weights, out_dtypes, *, layer, tm=1024):
    m, d = x.shape
    widths = [w.shape[2] for w in weights]
    return pl.pallas_call(
        _in_proj_body,
        grid=(m // tm,),
        in_specs=[pl.BlockSpec((tm, d), lambda i: (i, 0))] + [_layer_spec((d, w), layer) for w in widths],
        out_specs=[pl.BlockSpec((tm, w), lambda i: (i, 0)) for w in widths],
        out_shape=[jax.ShapeDtypeStruct((m, w), dt) for w, dt in zip(widths, out_dtypes)],
        compiler_params=pltpu.CompilerParams(
            dimension_semantics=("parallel",), vmem_limit_bytes=V7X_VMEM_LIMIT_BYTES),
        name="in_proj",
    )(x, *weights)


SUBLANES = 8


def _out_proj_ln_body(x_ref, hc_ref, halo_ref, yb_ref, yc_ref, taps_ref, wa_ref, wb_ref, wc_ref, g_ref, b_ref,
                      wg_ref, wu_ref, wd_ref, g3_ref, b3_ref, o_ref, *, alpha, tiles_per_seq):
    c = CONV_WIDTH
    h = hc_ref[...]
    u = h[:, c:2 * c] * h[:, 2 * c:3 * c]
    hh = halo_ref[...]
    at_seq_start = pl.program_id(0) % tiles_per_seq == 0
    u_prev = jnp.where(at_seq_start, 0.0, hh[:, c:2 * c] * hh[:, 2 * c:3 * c])
    y_a = h[:, 0:c] * _causal_dwconv(u_prev, u, taps_ref[...])
    y = jnp.dot(y_a.astype(BF16), wa_ref[...], preferred_element_type=F32)
    y += jnp.dot(yb_ref[...].astype(BF16), wb_ref[...], preferred_element_type=F32)
    y += jnp.dot(yc_ref[...].astype(BF16), wc_ref[...], preferred_element_type=F32)
    x2 = _layer_norm(alpha * x_ref[...] + y, g_ref[...], b_ref[...])
    o_ref[...] = _ffn_ln_value(x2, wg_ref, wu_ref, wd_ref, g3_ref, b3_ref, alpha)


def _out_proj_ln(x, hc, yb, yc, conv_taps, w_out, g, b, ffn_w, g3, b3, *, layer, alpha, seq_len, tm=512):
    m, d = x.shape
    f = ffn_w[0].shape[2]
    wa, wb, wc = CONV_WIDTH, yb.shape[1], yc.shape[1]
    assert seq_len % tm == 0 and CONV_K - 1 <= SUBLANES and wa == wb and wc == wa + wb
    row = lambda w: pl.BlockSpec((tm, w), lambda i: (i, 0))
    halo = pl.BlockSpec((SUBLANES, hc.shape[1]), lambda i: (jnp.maximum(i * (tm // SUBLANES) - 1, 0), 0))
    return pl.pallas_call(
        functools.partial(_out_proj_ln_body, alpha=alpha, tiles_per_seq=seq_len // tm),
        grid=(m // tm,),
        in_specs=[row(d), row(hc.shape[1]), halo, row(wb), row(wc),
                  _layer_spec(conv_taps.shape[1:], layer),
                  _layer_spec((wa, d), layer, (0, 0)), _layer_spec((wb, d), layer, (1, 0)),
                  _layer_spec((wc, d), layer, (1, 0)),
                  _layer_spec((1, d), layer), _layer_spec((1, d), layer),
                  _layer_spec((d, f), layer), _layer_spec((d, f), layer), _layer_spec((f, d), layer),
                  _layer_spec((1, d), layer), _layer_spec((1, d), layer)],
        out_specs=row(d),
        out_shape=jax.ShapeDtypeStruct((m, d), F32),
        compiler_params=pltpu.CompilerParams(
            dimension_semantics=("parallel",), vmem_limit_bytes=V7X_VMEM_LIMIT_BYTES),
        name="out_proj_ffn_ln",
    )(x, hc, hc, yb, yc, conv_taps, w_out, w_out, w_out, g, b, *ffn_w, g3, b3)


def _causal_dwconv(prev, body, taps):
    ext = jnp.concatenate([prev, body], axis=0)
    k = taps.shape[0]
    y = body * taps[k - 1:k, :]
    for j in range(k - 1):
        y = y + pltpu.roll(ext, k - 1 - j, 0)[SUBLANES:, :] * taps[j:j + 1, :]
    return y


def _dot(a, b):
    return jnp.dot(a, b, preferred_element_type=F32)


def _dot_nt(a, b):
    return lax.dot_general(a, b, (((1,), (1,)), ((), ())), preferred_element_type=F32)


def _dot_tn(a, b):
    return lax.dot_general(a, b, (((0,), (0,)), ((), ())), preferred_element_type=F32)


def _split_bf16(a, terms):
    parts, rest = [], a
    for _ in range(terms):
        p = rest.astype(BF16)
        parts.append(p)
        rest = rest - p.astype(F32)
    return parts


def _dot_exact_lhs(m_bf16, a, terms):
    return sum(_dot(m_bf16, p) for p in _split_bf16(a, terms))


def _dot_exact_rhs(a, m_bf16, terms):
    return sum(_dot(p, m_bf16) for p in _split_bf16(a, terms))


def _iota2(shape, dim):
    return lax.broadcasted_iota(jnp.int32, shape, dim)


GDN_INV_BASE = 8
GDN_CHUNKS_PER_STEP = 4


def _block_diag(x, bd_mask):
    return jnp.concatenate([x] * (x.shape[1] // x.shape[0]), axis=0) * bd_mask


def _heads_dot_hl(a, b, bd_mask):
    a_hi, a_lo = _split_bf16(a, 2)
    b_hi, b_lo = _split_bf16(b, 2)
    r = a.shape[0]
    hi = _dot(jnp.concatenate([a_hi, a_lo], axis=0), _block_diag(b_hi, bd_mask))
    return hi[0:r] + hi[r:2 * r] + _dot(a_hi, _block_diag(b_lo, bd_mask))


def _inverse_masks(c, width):
    ri, ci = _iota2((c, width), 0), _iota2((c, width), 1) & (c - 1)
    base = GDN_INV_BASE.bit_length() - 1
    eye = (ci == ri).astype(F32)
    diag = (ri >> base) == (ci >> base)
    levels = [((ri >> (s + 1)) == (ci >> (s + 1))) & ((ri >> s) != (ci >> s))
              for s in range(base, c.bit_length() - 1)]
    return eye, diag, levels


def _unit_lower_inverse(lmats, masks, bd_mask):
    eye, diag, levels = masks
    c = lmats[0].shape[0]
    hdot = lambda a, b: _heads_dot_hl(a, b, bd_mask)
    l0 = [jnp.where(diag, l, 0.0) for l in lmats]
    p = [eye - x for x in l0]
    m = [hdot(x, x) for x in l0]
    yield
    pm = [hdot(jnp.concatenate([pi, mi], axis=0), mi) for pi, mi in zip(p, m)]
    yield
    p = [pi + x[0:c] for pi, x in zip(p, pm)]
    inv = [pi + hdot(pi, x[c:2 * c]) for pi, x in zip(p, pm)]
    yield
    for level in levels:
        t = [hdot(i, jnp.where(level, l, 0.0)) for i, l in zip(inv, lmats)]
        yield
        inv = [i - hdot(ti, i) for i, ti in zip(inv, t)]
        yield
    return inv


def _interleave(*stage_generators):
    live = list(stage_generators)
    while live:
        for gen in list(live):
            try:
                next(gen)
            except StopIteration:
                live.remove(gen)


def _softplus(x):
    return jnp.maximum(x, 0.0) + jnp.log1p(jnp.exp(-jnp.abs(x)))


def _gdn_body(h_ref, taps_ref, alog_ref, dtb_ref, nw_ref, o_ref,
              qn_s, kn_s, kb_s, vb_s, g_s, u_s, w_s, a_s, qd_s, kd_s, egl_s, o_s):
    s = h_ref.shape[1]
    c = GDN_CHUNK
    d = HEAD_DIM
    w = GDN_WIDTH
    nc = s // c

    bd_mask = (_iota2((w, w), 0) >> HEAD_SHIFT == _iota2((w, w), 1) >> HEAD_SHIFT).astype(BF16)

    def head_sum(t):
        return _dot_exact_rhs(t, bd_mask, 2)

    def l2norm(t):
        return t * lax.rsqrt(head_sum(t * t) + NORM_EPS)

    def conv_silu(slab):
        cols = slice(slab * w, (slab + 1) * w)
        y = _causal_dwconv(jnp.zeros((SUBLANES, w), F32), h_ref[0, :, cols], taps_ref[:, cols])
        return y * jax.nn.sigmoid(y)

    qn_s[...] = l2norm(conv_silu(0)) * (d ** -0.5)
    kn = l2norm(conv_silu(1))
    kn_s[...] = kn
    ab = h_ref[0, :, 4 * w:4 * w + LANES]
    src = _iota2((LANES, w), 0)
    head = _iota2((LANES, w), 1) >> HEAD_SHIFT
    g_small = -jnp.exp(alog_ref[...]) * _softplus(ab + dtb_ref[...])
    g_s[...] = _dot_exact_rhs(g_small, (src == head).astype(BF16), 3)
    beta = _dot_exact_rhs(jax.nn.sigmoid(ab), (src == head + GDN_HEADS).astype(BF16), 3)
    kb_s[...] = kn * beta
    vb_s[...] = conv_silu(2) * beta

    ri = _iota2((c, w), 0)
    ci = _iota2((c, w), 1) & (c - 1)
    tril = ci <= ri
    strict = ci < ri
    upper = (ri <= ci).astype(F32)
    inv_masks = _inverse_masks(c, w)
    tril_b = (_iota2((c, c), 1) <= _iota2((c, c), 0)).astype(BF16)
    ones_b = jnp.ones((c, c), BF16)

    grp = GDN_CHUNKS_PER_STEP
    n_groups = nc // grp

    def chunk_rows(n):
        return pl.ds(n * c if isinstance(n, int) else pl.multiple_of(n * c, c), c)

    def prep_stages(group):
        ns = [group * grp + i for i in range(grp)]
        rows = [chunk_rows(n) for n in ns]
        g = [g_s[r, :] for r in rows]
        gc = [_dot_exact_lhs(tril_b, x, 3) for x in g]
        gct = [_dot_exact_lhs(ones_b, x * upper, 3) for x in g]
        yield
        kn_c = [kn_s[r, :] for r in rows]
        kb_c = [kb_s[r, :] for r in rows]
        qn_c = [qn_s[r, :] for r in rows]
        kq = [_dot_nt(jnp.concatenate([kb, q], axis=0).astype(BF16), _block_diag(k.astype(BF16), bd_mask))
              for kb, q, k in zip(kb_c, qn_c, kn_c)]
        yield
        decay = [jnp.where(tril, jnp.exp(jnp.where(tril, a - b, 0.0)), 0.0) for a, b in zip(gc, gct)]
        lmat = [jnp.where(strict, x[0:c] * dk, 0.0) for x, dk in zip(kq, decay)]
        tinv = yield from _unit_lower_inverse(lmat, inv_masks, bd_mask)
        egc = [jnp.exp(x) for x in gc]
        rhs = [jnp.concatenate([_block_diag(vb_s[r, :].astype(BF16), bd_mask),
                                _block_diag((kb * e).astype(BF16), bd_mask)], axis=1)
               for r, kb, e in zip(rows, kb_c, egc)]
        uw = [_dot(t.astype(BF16), x) for t, x in zip(tinv, rhs)]
        yield
        for i, (n, r) in enumerate(zip(ns, rows)):
            glast = gc[i][c - 1:c, :]
            egl_s[n] = jnp.exp(glast)
            u_s[r, :] = uw[i][:, 0:w]
            w_s[r, :] = uw[i][:, w:2 * w].astype(BF16)
            a_s[r, :] = jnp.where(tril, kq[i][c:2 * c] * decay[i], 0.0).astype(BF16)
            qd_s[r, :] = (qn_c[i] * egc[i]).astype(BF16)
            kd_s[r, :] = (kn_c[i] * jnp.exp(glast - gc[i])).astype(BF16)

    lane_head = _iota2((d, w), 1) >> HEAD_SHIFT

    def scan_stages(group, state):
        for i in range(grp):
            n = group * grp + i
            rows = chunk_rows(n)
            st = state[0]
            st_bd = _block_diag(st.astype(BF16), bd_mask)
            ws_qs = _dot(jnp.concatenate([w_s[rows, :], qd_s[rows, :]], axis=0), st_bd)
            yield
            v_new = u_s[rows, :] - ws_qs[0:c]
            v_b = v_new.astype(BF16)
            o_s[rows, :] = ws_qs[c:2 * c] + _dot(a_s[rows, :], _block_diag(v_b, bd_mask))
            kv = _dot_tn(kd_s[rows, :], v_b)
            upd = jnp.zeros((d, w), F32)
            for h in range(GDN_HEADS):
                upd = upd + jnp.where(lane_head == h, kv[h * d:(h + 1) * d, :], 0.0)
            state[0] = st * egl_s[n] + upd
            yield

    _interleave(prep_stages(0))

    def group_body(group, st):
        state = [st]
        _interleave(prep_stages(group), scan_stages(group - 1, state))
        return state[0]

    state = [lax.fori_loop(1, n_groups, group_body, jnp.zeros((d, w), F32))]
    _interleave(scan_stages(n_groups - 1, state))

    o = o_s[...]
    o = o * lax.rsqrt(head_sum(o * o) * (1.0 / d) + NORM_EPS) * nw_ref[...]
    z = h_ref[0, :, 3 * w:4 * w]
    o_ref[0] = (o * (z * jax.nn.sigmoid(z))).astype(o_ref.dtype)


def _gated_deltanet(hg, conv_taps, alog_rep, dtb_rep, nw_rep):
    bsz, s, cols = hg.shape
    w = GDN_WIDTH
    assert GDN_CHUNK == HEAD_DIM and cols == GDN_COLS and s % (GDN_CHUNK * GDN_CHUNKS_PER_STEP) == 0
    assert conv_taps.shape[0] == GDN_CONV_K and GDN_CONV_K - 1 <= SUBLANES
    return pl.pallas_call(
        _gdn_body,
        grid=(bsz,),
        in_specs=[pl.BlockSpec((1, s, cols), lambda b: (b, 0, 0)),
                  _const_spec(conv_taps.shape),
                  _const_spec((1, LANES)), _const_spec((1, LANES)), _const_spec((1, w))],
        out_specs=pl.BlockSpec((1, s, w), lambda b: (b, 0, 0)),
        out_shape=jax.ShapeDtypeStruct((bsz, s, w), BF16),
        scratch_shapes=[pltpu.VMEM((s, w), F32)] * 6 + [pltpu.VMEM((s, w), BF16)] * 4 + [
            pltpu.VMEM((s // GDN_CHUNK, 1, w), F32), pltpu.VMEM((s, w), F32)],
        compiler_params=pltpu.CompilerParams(
            dimension_semantics=("parallel",), vmem_limit_bytes=V7X_VMEM_LIMIT_BYTES),
        name="gated_deltanet",
    )(hg, conv_taps, alog_rep, dtb_rep, nw_rep)


NSA_TQ = 256
NSA_SLC_STEP = 256
NSA_NCMP_PAD = 128
NSA_VPAD = 16


def _softmax_cols(s):
    m = jnp.max(s, axis=0, keepdims=True)
    e = jnp.exp(s - m)
    return e, jnp.sum(e, axis=0, keepdims=True)


def _nsa_body(q_ref, gate_ref, cmp_ref, kv_ref, pe_ref, w1_ref, w2_ref, ovlt_ref, wbias_ref, dbias_ref, o_ref,
              kc_s, vct_s, ks_s, kw_s, vst_s, vwt_s, oslc_s, owin_s):
    qi = pl.program_id(2)
    tq, d, grp = NSA_TQ, HEAD_DIM, NSA_GROUP
    s_len = kv_ref.shape[1]
    n_sel = s_len // SEL_BLOCK
    seg = CMP_LEN // CMP_STRIDE
    band = WINDOW + tq
    wide = grp * tq

    @pl.when(qi == 0)
    def _():
        def relayout(i, carry):
            r0 = pl.multiple_of(i * LANES, LANES)
            blk = kv_ref[0, pl.ds(r0, LANES), :]
            block_of_row = (r0 + _iota2((LANES, LANES - d), 0)) >> SEL_SHIFT
            one_hot = (block_of_row == _iota2((LANES, LANES - d), 1)).astype(BF16)
            ks_s[pl.ds(r0, LANES), :] = jnp.concatenate([blk[:, 0:d].astype(BF16), one_hot], axis=1)
            kw_s[pl.ds(r0, LANES), :] = blk[:, 2 * d:3 * d].astype(BF16)
            blk_t = blk.astype(F32).T
            ones_row = (_iota2((NSA_VPAD, LANES), 0) == 0).astype(BF16)
            vst_s[:, pl.ds(r0, LANES)] = jnp.concatenate([blk_t[d:2 * d, :].astype(BF16), ones_row], axis=0)
            vwt_s[:, pl.ds(r0, LANES)] = jnp.concatenate([blk_t[3 * d:4 * d, :].astype(BF16), ones_row], axis=0)
            return carry

        lax.fori_loop(0, s_len // LANES, relayout, 0)
        pre = [jnp.zeros((NSA_NCMP_PAD, 2 * LANES), F32) for _ in range(seg)]
        for l in range(CMP_STRIDE):
            t_l = cmp_ref[0, pl.ds(l, NSA_NCMP_PAD, stride=CMP_STRIDE), :]
            for h in range(seg):
                lh = l + h * CMP_STRIDE
                pre[h] = pre[h] + _dot((t_l + pe_ref[lh:lh + 1, :]).astype(BF16), w1_ref[lh])
        hid = pre[0] + pltpu.roll(pre[1], NSA_NCMP_PAD - 1, 0)
        hid = hid * jax.nn.sigmoid(hid)
        kcv = _dot(hid.astype(BF16), w2_ref[...])
        kc_s[...] = kcv[:, 0:d].astype(BF16)
        vct_s[...] = kcv.T[d:2 * d, :].astype(BF16)

    ts = qi * tq
    q_t = (q_ref[0].astype(F32) * (d ** -0.5)).T
    q_t = jnp.concatenate([q_t[g * d:(g + 1) * d, :] for g in range(grp)], axis=1).astype(BF16)
    t_row = ts + (_iota2((1, wide), 1) & (tq - 1))
    t_row1 = t_row[:, 0:tq]

    w0 = pl.multiple_of(jnp.maximum(ts - WINDOW, 0), tq)
    sc_win = _dot(kw_s[pl.ds(w0, band), :], q_t) + wbias_ref[jnp.minimum(qi, WINDOW // tq)]
    m_win = jnp.max(sc_win, axis=0, keepdims=True)

    n_col =_iota2((NSA_NCMP_PAD, 1), 0)
    cmp_ok = (n_col * CMP_STRIDE + (CMP_LEN - 1) <= t_row) & (n_col < NSA_NCMP_PAD - 1)
    e, den = _softmax_cols(jnp.where(cmp_ok, _dot(kc_s[...], q_t), NEG))
    p_cmp = jnp.where(t_row >= CMP_LEN - 1, e / den, 0.0)
    o_cmp = _dot(vct_s[...], p_cmp.astype(BF16))

    p_sum = p_cmp[:, 0:tq]
    for g in range(1, grp):
        p_sum = p_sum + p_cmp[:, g * tq:(g + 1) * tq]
    imp = _dot_exact_lhs(ovlt_ref[...], p_sum, 3)
    j_idx = _iota2((n_sel, tq), 0)
    q_blk = t_row1 >> SEL_SHIFT
    forced = (j_idx == 0) | (j_idx == q_blk) | (j_idx == q_blk - 1)
    imp = jnp.where(forced, imp + FORCE_BONUS, imp)
    causal_blk = j_idx <= q_blk
    imp = jnp.where(causal_blk, imp, NEG)
    rank = jnp.zeros((n_sel, tq), F32)
    for jp in range(n_sel):
        row = imp[jp:jp + 1, :]
        beats = (row > imp) | ((row == imp) & (j_idx > jp))
        rank = rank + beats.astype(F32)
    selected = (rank < SEL_TOPN) & causal_blk

    blk0 = ts >> SEL_SHIFT
    sel_bias = jnp.where(selected, 0.0, NEG)
    past_bias = jnp.where(j_idx < blk0, sel_bias, NEG)
    zero_rows = jnp.zeros((LANES - d - n_sel, wide), BF16)

    def with_bias(bias):
        return jnp.concatenate([q_t, jnp.concatenate([bias.astype(BF16)] * grp, axis=1), zero_rows], axis=0)

    sc_diag = _dot(ks_s[pl.ds(ts, tq), :], with_bias(sel_bias)) + dbias_ref[...]
    q_past = with_bias(past_bias)

    ch = NSA_SLC_STEP

    def weighted_values(value_t, sc, m):
        return _dot(value_t, jnp.exp(sc - m).astype(BF16))

    def normalised(acc):
        return acc[0:d] / acc[d:d + 1]

    def slc_and_win(nk):
        n_past, n_win = nk // ch, band // ch
        past_sc, win_acc = [], None
        for i in range(max(n_past, n_win)):
            if i < n_past:
                past_sc.append(_dot(ks_s[i * ch:(i + 1) * ch, :], q_past))
            if i < n_win:
                term = weighted_values(vwt_s[:, pl.ds(w0 + i * ch, ch)], sc_win[i * ch:(i + 1) * ch], m_win)
                win_acc = term if win_acc is None else win_acc + term
        owin_s[...] = normalised(win_acc)
        m = functools.reduce(jnp.maximum, [jnp.max(p, axis=0, keepdims=True) for p in past_sc + [sc_diag]])
        acc = None
        for i in range(n_past):
            term = weighted_values(vst_s[:, i * ch:(i + 1) * ch], past_sc[i], m)
            acc = term if acc is None else acc + term
        term = weighted_values(vst_s[:, pl.ds(ts, tq)], sc_diag, m)
        oslc_s[...] = normalised(term if acc is None else acc + term)

    n_var = (s_len - tq + NSA_SLC_STEP - 1) // NSA_SLC_STEP + 1
    variant = (ts + NSA_SLC_STEP - 1) // NSA_SLC_STEP
    for v in range(n_var):
        @pl.when(variant == v)
        def _(v=v):
            slc_and_win(min(v * NSA_SLC_STEP, s_len))
    o_slc, o_win = oslc_s[...], owin_s[...]

    gt = jax.nn.sigmoid(gate_ref[0]).T
    outs = []
    for g in range(grp):
        cols = slice(g * tq, (g + 1) * tq)
        outs.append(gt[g:g + 1, :] * o_cmp[:, cols] + gt[grp + g:grp + g + 1, :] * o_slc[:, cols]
                    + gt[2 * grp + g:2 * grp + g + 1, :] * o_win[:, cols])
    o_ref[0] = jnp.concatenate(outs, axis=0).T.astype(o_ref.dtype)


def _attention_biases(tq):
    q = jnp.arange(NSA_GROUP * tq) & (tq - 1)
    r = jnp.arange(WINDOW + tq)
    lead = jnp.arange(WINDOW // tq + 1) * tq
    dist = lead[:, None, None] + q[None, None, :] - r[None, :, None]
    win_bias = jnp.where((dist >= 0) & (dist < WINDOW), 0.0, NEG).astype(F32)
    diag_bias = jnp.where(jnp.arange(tq)[:, None] <= q[None, :], 0.0, NEG).astype(F32)
    return win_bias, diag_bias


def _nsa_attention(h_mm, h_f32, pe, w1, w2, overlap):
    bsz, s, _ = h_mm.shape
    tq, d = NSA_TQ, HEAD_DIM
    qw = NSA_GROUP * d
    assert h_mm.shape[2] == NSA_KV_HEADS * 2 * qw and h_f32.shape[2] == NSA_KV_HEADS * 2 * LANES
    assert s % tq == 0 and s >= WINDOW + tq and (WINDOW + tq) % NSA_SLC_STEP == 0 and d + s // SEL_BLOCK <= LANES
    assert WINDOW % tq == 0
    win_bias, diag_bias = _attention_biases(tq)
    return pl.pallas_call(
        _nsa_body,
        grid=(bsz, NSA_KV_HEADS, s // tq),
        in_specs=[
            pl.BlockSpec((1, tq, qw), lambda b, h, i: (b, i, 2 * h)),
            pl.BlockSpec((1, tq, LANES), lambda b, h, i: (b, i, 2 * h)),
            pl.BlockSpec((1, s, LANES), lambda b, h, i: (b, 0, 2 * h + 1)),
            pl.BlockSpec((1, s, qw), lambda b, h, i: (b, 0, 2 * h + 1)),
            _const_spec(pe.shape), _const_spec(w1.shape), _const_spec(w2.shape), _const_spec(overlap.shape),
            _const_spec(win_bias.shape), _const_spec(diag_bias.shape),
        ],
        out_specs=pl.BlockSpec((1, tq, qw), lambda b, h, i: (b, i, h)),
        out_shape=jax.ShapeDtypeStruct((bsz, s, NSA_WIDTH), BF16),
        scratch_shapes=[pltpu.VMEM((NSA_NCMP_PAD, d), BF16), pltpu.VMEM((d, NSA_NCMP_PAD), BF16),
                        pltpu.VMEM((s, LANES), BF16), pltpu.VMEM((s, d), BF16),
                        pltpu.VMEM((d + NSA_VPAD, s), BF16), pltpu.VMEM((d + NSA_VPAD, s), BF16),
                        pltpu.VMEM((d, NSA_GROUP * tq), F32), pltpu.VMEM((d, NSA_GROUP * tq), F32)],
        compiler_params=pltpu.CompilerParams(
            dimension_semantics=("parallel", "parallel", "arbitrary"),
            vmem_limit_bytes=V7X_VMEM_LIMIT_BYTES),
        name="nsa_attention",
    )(h_mm, h_f32, h_f32, h_mm, pe, w1, w2, overlap, win_bias, diag_bias)


def _prep_w_in(w):
    w = w.astype(BF16)
    lead = w.shape[:-1]
    sizes = (3 * CONV_WIDTH, 3 * GDN_WIDTH, GDN_WIDTH, GDN_HEADS, GDN_HEADS,
             NSA_WIDTH, 6 * NSA_KV_HEADS * HEAD_DIM, 3 * NSA_Q_HEADS)
    parts, start = [], 0
    for n in sizes:
        parts.append(w[..., start:start + n])
        start += n
    wconv, gqkv, gz, ga, gb, nq, nkv, ngate = parts
    pad = lambda n: jnp.zeros(lead + (n,), w.dtype)
    wgdn = jnp.concatenate([gqkv, gz, ga, gb, pad(LANES - 2 * GDN_HEADS)], axis=-1)
    nkv = nkv.reshape(lead + (6, NSA_KV_HEADS, HEAD_DIM))
    ngate = ngate.reshape(lead + (NSA_KV_HEADS, NSA_GROUP, 3))
    mm_cols, f32_cols = [], []
    for h in range(NSA_KV_HEADS):
        qw = NSA_GROUP * HEAD_DIM
        gates = jnp.swapaxes(ngate[..., h, :, :], -1, -2).reshape(lead + (3 * NSA_GROUP,))
        kv = nkv[..., h, :].reshape(lead + (6 * HEAD_DIM,))
        mm_cols += [nq[..., h * qw:(h + 1) * qw], kv[..., 2 * HEAD_DIM:]]
        f32_cols += [gates, pad(LANES - 3 * NSA_GROUP), kv[..., :2 * HEAD_DIM]]
    return wconv, wgdn, jnp.concatenate(mm_cols, axis=-1), jnp.concatenate(f32_cols, axis=-1)


def _prep_cmp_weights(pe_k, pe_v, k_w1, k_w2, v_w1, v_w2):
    d, hid = HEAD_DIM, k_w1.shape[1]
    pe = jnp.concatenate([pe_k, pe_v], axis=1)
    k1 = k_w1.reshape(CMP_LEN, d, hid)
    v1 = v_w1.reshape(CMP_LEN, d, hid)
    z1 = jnp.zeros_like(k1)
    w1 = jnp.concatenate([jnp.concatenate([k1, z1], axis=2),
                          jnp.concatenate([z1, v1], axis=2)], axis=1)
    z2 = jnp.zeros_like(k_w2)
    w2 = jnp.concatenate([jnp.concatenate([k_w2, z2], axis=1),
                          jnp.concatenate([z2, v_w2], axis=1)], axis=0)
    return pe, w1.astype(BF16), w2.astype(BF16)


def _overlap_matrix(s):
    n_cmp = (s - CMP_LEN) // CMP_STRIDE + 1
    t = jnp.arange(s)
    starts = jnp.arange(NSA_NCMP_PAD) * CMP_STRIDE
    cmp_tok = (t[None, :] >= starts[:, None]) & (t[None, :] < starts[:, None] + CMP_LEN)
    cmp_tok = cmp_tok & (jnp.arange(NSA_NCMP_PAD) < n_cmp)[:, None]
    sel_tok = (t[None, :] // SEL_BLOCK) == jnp.arange(s // SEL_BLOCK)[:, None]
    return ((sel_tok.astype(F32) @ cmp_tok.astype(F32).T) / CMP_LEN).astype(BF16)


def kernel(x, ffn1_w_gate, ffn1_w_up, ffn1_w_down, ln1_g, ln1_b, w_in, conv_w, gdn_conv_w, gdn_a_log, gdn_dt_bias, gdn_norm_w, cmp_pe_k, cmp_pe_v, cmp_k_w1, cmp_k_w2, cmp_v_w1, cmp_v_w2, w_out, ln2_g, ln2_b, ffn2_w_gate, ffn2_w_up, ffn2_w_down, ln3_g, ln3_b):
    bsz, s, dm = x.shape
    depth = w_in.shape[0]
    alpha = (2 * depth) ** 0.25
    m = bsz * s
    overlap = _overlap_matrix(s)
    row = lambda v: v.reshape(1, -1)
    rows = lambda v: v[:, None, :]
    bf = lambda v: v.astype(BF16)
    ffn1, ffn2 = (bf(ffn1_w_gate), bf(ffn1_w_up), bf(ffn1_w_down)), (bf(ffn2_w_gate), bf(ffn2_w_up), bf(ffn2_w_down))
    w_in_groups = _prep_w_in(w_in)
    wo = bf(w_out)
    conv_taps = jnp.swapaxes(conv_w, 1, 2)
    h = x.reshape(m, dm)
    for l in range(depth):
        h = _ffn_ln(h, *ffn1, rows(ln1_g), rows(ln1_b), layer=l, alpha=alpha)

        hc, hg, hn_mm, hn_f32 = _in_proj(h, w_in_groups, (F32, F32, BF16, F32), layer=l)
        y_b = _gated_deltanet(hg.reshape(bsz, s, -1), gdn_conv_w[l].T,
                              row(jnp.pad(gdn_a_log[l], (0, LANES - GDN_HEADS))),
                              row(jnp.pad(gdn_dt_bias[l], (0, LANES - GDN_HEADS))),
                              row(jnp.tile(gdn_norm_w[l], GDN_HEADS)))
        pe, w1, w2 = _prep_cmp_weights(cmp_pe_k[l], cmp_pe_v[l], cmp_k_w1[l], cmp_k_w2[l], cmp_v_w1[l], cmp_v_w2[l])
        y_c = _nsa_attention(hn_mm.reshape(bsz, s, -1), hn_f32.reshape(bsz, s, -1), pe, w1, w2, overlap)
        h = _out_proj_ln(h, hc, y_b.reshape(m, -1), y_c.reshape(m, -1), conv_taps, wo,
                         rows(ln2_g), rows(ln2_b), ffn2, rows(ln3_g), rows(ln3_b),
                         layer=l, alpha=alpha, seq_len=s)
    return h.reshape(bsz, s, dm)
```

```python
import functools

import jax
import jax.numpy as jnp
from jax import lax
from jax.experimental import pallas as pl
from jax.experimental.pallas import tpu as pltpu

F32 = jnp.float32
BF16 = jnp.bfloat16

HEAD_DIM = 64
CONV_WIDTH = 256
CONV_K = 3
GDN_HEADS = 4
GDN_WIDTH = GDN_HEADS * HEAD_DIM
GDN_CONV_K = 4
GDN_CHUNK = 64
NSA_Q_HEADS = 8
NSA_KV_HEADS = 2
NSA_GROUP = NSA_Q_HEADS // NSA_KV_HEADS
NSA_WIDTH = NSA_Q_HEADS * HEAD_DIM
CMP_LEN = 32
CMP_STRIDE = 16
SEL_BLOCK = 64
SEL_TOPN = 8
WINDOW = 512
FORCE_BONUS = 1e3
LN_EPS = 1e-5
NORM_EPS = 1e-6
NEG = -1e30

V7X_VMEM_LIMIT_BYTES = 56 * 1024 * 1024
LANES = 128
HEAD_SHIFT = HEAD_DIM.bit_length() - 1
SEL_SHIFT = SEL_BLOCK.bit_length() - 1

GDN_COLS = 4 * GDN_WIDTH + LANES


def _layer_norm(r, g, b):
    mu = jnp.mean(r, axis=-1, keepdims=True)
    c = r - mu
    var = jnp.mean(c * c, axis=-1, keepdims=True)
    return c * lax.rsqrt(var + LN_EPS) * g + b


def _const_spec(shape):
    return pl.BlockSpec(shape, lambda *_: (0,) * len(shape), pipeline_mode=pl.Buffered(1))


def _layer_spec(shape, layer, block=None):
    index = (layer,) + tuple(block or (0,) * len(shape))
    return pl.BlockSpec((None,) + tuple(shape), lambda *_: index, pipeline_mode=pl.Buffered(1))


def _ffn_ln_value(x, wg_ref, wu_ref, wd_ref, g_ref, b_ref, alpha):
    xb = x.astype(BF16)
    hg = jnp.dot(xb, wg_ref[...], preferred_element_type=F32)
    hu = jnp.dot(xb, wu_ref[...], preferred_element_type=F32)
    a = (hg * jax.nn.sigmoid(hg) * hu).astype(BF16)
    y = jnp.dot(a, wd_ref[...], preferred_element_type=F32)
    return _layer_norm(alpha * x + 0.5 * y, g_ref[...], b_ref[...])


def _ffn_ln_body(x_ref, wg_ref, wu_ref, wd_ref, g_ref, b_ref, o_ref, *, alpha):
    o_ref[...] = _ffn_ln_value(x_ref[...], wg_ref, wu_ref, wd_ref, g_ref, b_ref, alpha)


def _ffn_ln(x, wg, wu, wd, g, b, *, layer, alpha, tm=512):
    m, d = x.shape
    f = wg.shape[2]
    return pl.pallas_call(
        functools.partial(_ffn_ln_body, alpha=alpha),
        grid=(m // tm,),
        in_specs=[
            pl.BlockSpec((tm, d), lambda i: (i, 0)),
            _layer_spec((d, f), layer), _layer_spec((d, f), layer), _layer_spec((f, d), layer),
            _layer_spec((1, d), layer), _layer_spec((1, d), layer),
        ],
        out_specs=pl.BlockSpec((tm, d), lambda i: (i, 0)),
        out_shape=jax.ShapeDtypeStruct((m, d), F32),
        compiler_params=pltpu.CompilerParams(
            dimension_semantics=("parallel",), vmem_limit_bytes=V7X_VMEM_LIMIT_BYTES),
        name="ffn_ln",
    )(x, wg, wu, wd, g, b)


def _in_proj_body(x_ref, *refs):
    n = len(refs) // 2
    xb = x_ref[...].astype(BF16)
    for w_ref, o_ref in zip(refs[:n], refs[n:]):
        o_ref[...] = jnp.dot(xb, w_ref[...], preferred_element_type=F32).astype(o_ref.dtype)


def _in_proj(x, weights, out_dtypes, *, layer, tm=1024):
    m, d = x.shape
    widths = [w.shape[2] for w in weights]
    return pl.pallas_call(
        _in_proj_body,
        grid=(m // tm,),
        in_specs=[pl.BlockSpec((tm, d), lambda i: (i, 0))] + [_layer_spec((d, w), layer) for w in widths],
        out_specs=[pl.BlockSpec((tm, w), lambda i: (i, 0)) for w in widths],
        out_shape=[jax.ShapeDtypeStruct((m, w), dt) for w, dt in zip(widths, out_dtypes)],
        compiler_params=pltpu.CompilerParams(
            dimension_semantics=("parallel",), vmem_limit_bytes=V7X_VMEM_LIMIT_BYTES),
        name="in_proj",
    )(x, *weights)


SUBLANES = 8


def _out_proj_ln_body(x_ref, hc_ref, halo_ref, yb_ref, yc_ref, taps_ref, wa_ref, wb_ref, wc_ref, g_ref, b_ref,
                      wg_ref, wu_ref, wd_ref, g3_ref, b3_ref, o_ref, *, alpha, tiles_per_seq):
    c = CONV_WIDTH
    h = hc_ref[...]
    u = h[:, c:2 * c] * h[:, 2 * c:3 * c]
    hh = halo_ref[...]
    at_seq_start = pl.program_id(0) % tiles_per_seq == 0
    u_prev = jnp.where(at_seq_start, 0.0, hh[:, c:2 * c] * hh[:, 2 * c:3 * c])
    y_a = h[:, 0:c] * _causal_dwconv(u_prev, u, taps_ref[...])
    y = jnp.dot(y_a.astype(BF16), wa_ref[...], preferred_element_type=F32)
    y += jnp.dot(yb_ref[...].astype(BF16), wb_ref[...], preferred_element_type=F32)
    y += jnp.dot(yc_ref[...].astype(BF16), wc_ref[...], preferred_element_type=F32)
    x2 = _layer_norm(alpha * x_ref[...] + y, g_ref[...], b_ref[...])
    o_ref[...] = _ffn_ln_value(x2, wg_ref, wu_ref, wd_ref, g3_ref, b3_ref, alpha)


def _out_proj_ln(x, hc, yb, yc, conv_taps, w_out, g, b, ffn_w, g3, b3, *, layer, alpha, seq_len, tm=512):
    m, d = x.shape
    f = ffn_w[0].shape[2]
    wa, wb, wc = CONV_WIDTH, yb.shape[1], yc.shape[1]
    assert seq_len % tm == 0 and CONV_K - 1 <= SUBLANES and wa == wb and wc == wa + wb
    row = lambda w: pl.BlockSpec((tm, w), lambda i: (i, 0))
    halo = pl.BlockSpec((SUBLANES, hc.shape[1]), lambda i: (jnp.maximum(i * (tm // SUBLANES) - 1, 0), 0))
    return pl.pallas_call(
        functools.partial(_out_proj_ln_body, alpha=alpha, tiles_per_seq=seq_len // tm),
        grid=(m // tm,),
        in_specs=[row(d), row(hc.shape[1]), halo, row(wb), row(wc),
                  _layer_spec(conv_taps.shape[1:], layer),
                  _layer_spec((wa, d), layer, (0, 0)), _layer_spec((wb, d), layer, (1, 0)),
                  _layer_spec((wc, d), layer, (1, 0)),
                  _layer_spec((1, d), layer), _layer_spec((1, d), layer),
                  _layer_spec((d, f), layer), _layer_spec((d, f), layer), _layer_spec((f, d), layer),
                  _layer_spec((1, d), layer), _layer_spec((1, d), layer)],
        out_specs=row(d),
        out_shape=jax.ShapeDtypeStruct((m, d), F32),
        compiler_params=pltpu.CompilerParams(
            dimension_semantics=("parallel",), vmem_limit_bytes=V7X_VMEM_LIMIT_BYTES),
        name="out_proj_ffn_ln",
    )(x, hc, hc, yb, yc, conv_taps, w_out, w_out, w_out, g, b, *ffn_w, g3, b3)


def _causal_dwconv(prev, body, taps):
    ext = jnp.concatenate([prev, body], axis=0)
    k = taps.shape[0]
    y = body * taps[k - 1:k, :]
    for j in range(k - 1):
        y = y + pltpu.roll(ext, k - 1 - j, 0)[SUBLANES:, :] * taps[j:j + 1, :]
    return y


def _dot(a, b):
    return jnp.dot(a, b, preferred_element_type=F32)


def _dot_nt(a, b):
    return lax.dot_general(a, b, (((1,), (1,)), ((), ())), preferred_element_type=F32)


def _dot_tn(a, b):
    return lax.dot_general(a, b, (((0,), (0,)), ((), ())), preferred_element_type=F32)


def _split_bf16(a, terms):
    parts, rest = [], a
    for _ in range(terms):
        p = rest.astype(BF16)
        parts.append(p)
        rest = rest - p.astype(F32)
    return parts


def _dot_exact_lhs(m_bf16, a, terms):
    return sum(_dot(m_bf16, p) for p in _split_bf16(a, terms))


def _dot_exact_rhs(a, m_bf16, terms):
    return sum(_dot(p, m_bf16) for p in _split_bf16(a, terms))


def _iota2(shape, dim):
    return lax.broadcasted_iota(jnp.int32, shape, dim)


GDN_INV_BASE = 8
GDN_CHUNKS_PER_STEP = 4


def _block_diag(x, bd_mask):
    return jnp.concatenate([x] * (x.shape[1] // x.shape[0]), axis=0) * bd_mask


def _heads_dot_hl(a, b, bd_mask):
    a_hi, a_lo = _split_bf16(a, 2)
    b_hi, b_lo = _split_bf16(b, 2)
    r = a.shape[0]
    hi = _dot(jnp.concatenate([a_hi, a_lo], axis=0), _block_diag(b_hi, bd_mask))
    return hi[0:r] + hi[r:2 * r] + _dot(a_hi, _block_diag(b_lo, bd_mask))


def _inverse_masks(c, width):
    ri, ci = _iota2((c, width), 0), _iota2((c, width), 1) & (c - 1)
    base = GDN_INV_BASE.bit_length() - 1
    eye = (ci == ri).astype(F32)
    diag = (ri >> base) == (ci >> base)
    levels = [((ri >> (s + 1)) == (ci >> (s + 1))) & ((ri >> s) != (ci >> s))
              for s in range(base, c.bit_length() - 1)]
    return eye, diag, levels


def _unit_lower_inverse(lmats, masks, bd_mask):
    eye, diag, levels = masks
    c = lmats[0].shape[0]
    hdot = lambda a, b: _heads_dot_hl(a, b, bd_mask)
    l0 = [jnp.where(diag, l, 0.0) for l in lmats]
    p = [eye - x for x in l0]
    m = [hdot(x, x) for x in l0]
    yield
    pm = [hdot(jnp.concatenate([pi, mi], axis=0), mi) for pi, mi in zip(p, m)]
    yield
    p = [pi + x[0:c] for pi, x in zip(p, pm)]
    inv = [pi + hdot(pi, x[c:2 * c]) for pi, x in zip(p, pm)]
    yield
    for level in levels:
        t = [hdot(i, jnp.where(level, l, 0.0)) for i, l in zip(inv, lmats)]
        yield
        inv = [i - hdot(ti, i) for i, ti in zip(inv, t)]
        yield
    return inv


def _interleave(*stage_generators):
    live = list(stage_generators)
    while live:
        for gen in list(live):
            try:
                next(gen)
            except StopIteration:
                live.remove(gen)


def _softplus(x):
    return jnp.maximum(x, 0.0) + jnp.log1p(jnp.exp(-jnp.abs(x)))


def _gdn_body(h_ref, taps_ref, alog_ref, dtb_ref, nw_ref, o_ref,
              qn_s, kn_s, kb_s, vb_s, g_s, u_s, w_s, a_s, qd_s, kd_s, egl_s, o_s):
    s = h_ref.shape[1]
    c = GDN_CHUNK
    d = HEAD_DIM
    w = GDN_WIDTH
    nc = s // c

    bd_mask = (_iota2((w, w), 0) >> HEAD_SHIFT == _iota2((w, w), 1) >> HEAD_SHIFT).astype(BF16)

    def head_sum(t):
        return _dot_exact_rhs(t, bd_mask, 2)

    def l2norm(t):
        return t * lax.rsqrt(head_sum(t * t) + NORM_EPS)

    def conv_silu(slab):
        cols = slice(slab * w, (slab + 1) * w)
        y = _causal_dwconv(jnp.zeros((SUBLANES, w), F32), h_ref[0, :, cols], taps_ref[:, cols])
        return y * jax.nn.sigmoid(y)

    qn_s[...] = l2norm(conv_silu(0)) * (d ** -0.5)
    kn = l2norm(conv_silu(1))
    kn_s[...] = kn
    ab = h_ref[0, :, 4 * w:4 * w + LANES]
    src = _iota2((LANES, w), 0)
    head = _iota2((LANES, w), 1) >> HEAD_SHIFT
    g_small = -jnp.exp(alog_ref[...]) * _softplus(ab + dtb_ref[...])
    g_s[...] = _dot_exact_rhs(g_small, (src == head).astype(BF16), 3)
    beta = _dot_exact_rhs(jax.nn.sigmoid(ab), (src == head + GDN_HEADS).astype(BF16), 3)
    kb_s[...] = kn * beta
    vb_s[...] = conv_silu(2) * beta

    ri = _iota2((c, w), 0)
    ci = _iota2((c, w), 1) & (c - 1)
    tril = ci <= ri
    strict = ci < ri
    upper = (ri <= ci).astype(F32)
    inv_masks = _inverse_masks(c, w)
    tril_b = (_iota2((c, c), 1) <= _iota2((c, c), 0)).astype(BF16)
    ones_b = jnp.ones((c, c), BF16)

    grp = GDN_CHUNKS_PER_STEP
    n_groups = nc // grp

    def chunk_rows(n):
        return pl.ds(n * c if isinstance(n, int) else pl.multiple_of(n * c, c), c)

    def prep_stages(group):
        ns = [group * grp + i for i in range(grp)]
        rows = [chunk_rows(n) for n in ns]
        g = [g_s[r, :] for r in rows]
        gc = [_dot_exact_lhs(tril_b, x, 3) for x in g]
        gct = [_dot_exact_lhs(ones_b, x * upper, 3) for x in g]
        yield
        kn_c = [kn_s[r, :] for r in rows]
        kb_c = [kb_s[r, :] for r in rows]
        qn_c = [qn_s[r, :] for r in rows]
        kq = [_dot_nt(jnp.concatenate([kb, q], axis=0).astype(BF16), _block_diag(k.astype(BF16), bd_mask))
              for kb, q, k in zip(kb_c, qn_c, kn_c)]
        yield
        decay = [jnp.where(tril, jnp.exp(jnp.where(tril, a - b, 0.0)), 0.0) for a, b in zip(gc, gct)]
        lmat = [jnp.where(strict, x[0:c] * dk, 0.0) for x, dk in zip(kq, decay)]
        tinv = yield from _unit_lower_inverse(lmat, inv_masks, bd_mask)
        egc = [jnp.exp(x) for x in gc]
        rhs = [jnp.concatenate([_block_diag(vb_s[r, :].astype(BF16), bd_mask),
                                _block_diag((kb * e).astype(BF16), bd_mask)], axis=1)
               for r, kb, e in zip(rows, kb_c, egc)]
        uw = [_dot(t.astype(BF16), x) for t, x in zip(tinv, rhs)]
        yield
        for i, (n, r) in enumerate(zip(ns, rows)):
            glast = gc[i][c - 1:c, :]
            egl_s[n] = jnp.exp(glast)
            u_s[r, :] = uw[i][:, 0:w]
            w_s[r, :] = uw[i][:, w:2 * w].astype(BF16)
            a_s[r, :] = jnp.where(tril, kq[i][c:2 * c] * decay[i], 0.0).astype(BF16)
            qd_s[r, :] = (qn_c[i] * egc[i]).astype(BF16)
            kd_s[r, :] = (kn_c[i] * jnp.exp(glast - gc[i])).astype(BF16)

    lane_head = _iota2((d, w), 1) >> HEAD_SHIFT

    def scan_stages(group, state):
        for i in range(grp):
            n = group * grp + i
            rows = chunk_rows(n)
            st = state[0]
            st_bd = _block_diag(st.astype(BF16), bd_mask)
            ws_qs = _dot(jnp.concatenate([w_s[rows, :], qd_s[rows, :]], axis=0), st_bd)
            yield
            v_new = u_s[rows, :] - ws_qs[0:c]
            v_b = v_new.astype(BF16)
            o_s[rows, :] = ws_qs[c:2 * c] + _dot(a_s[rows, :], _block_diag(v_b, bd_mask))
            kv = _dot_tn(kd_s[rows, :], v_b)
            upd = jnp.zeros((d, w), F32)
            for h in range(GDN_HEADS):
                upd = upd + jnp.where(lane_head == h, kv[h * d:(h + 1) * d, :], 0.0)
            state[0] = st * egl_s[n] + upd
            yield

    _interleave(prep_stages(0))

    def group_body(group, st):
        state = [st]
        _interleave(prep_stages(group), scan_stages(group - 1, state))
        return state[0]

    state = [lax.fori_loop(1, n_groups, group_body, jnp.zeros((d, w), F32))]
    _interleave(scan_stages(n_groups - 1, state))

    o = o_s[...]
    o = o * lax.rsqrt(head_sum(o * o) * (1.0 / d) + NORM_EPS) * nw_ref[...]
    z = h_ref[0, :, 3 * w:4 * w]
    o_ref[0] = (o * (z * jax.nn.sigmoid(z))).astype(o_ref.dtype)


def _gated_deltanet(hg, conv_taps, alog_rep, dtb_rep, nw_rep):
    bsz, s, cols = hg.shape
    w = GDN_WIDTH
    assert GDN_CHUNK == HEAD_DIM and cols == GDN_COLS and s % (GDN_CHUNK * GDN_CHUNKS_PER_STEP) == 0
    assert conv_taps.shape[0] == GDN_CONV_K and GDN_CONV_K - 1 <= SUBLANES
    return pl.pallas_call(
        _gdn_body,
        grid=(bsz,),
        in_specs=[pl.BlockSpec((1, s, cols), lambda b: (b, 0, 0)),
                  _const_spec(conv_taps.shape),
                  _const_spec((1, LANES)), _const_spec((1, LANES)), _const_spec((1, w))],
        out_specs=pl.BlockSpec((1, s, w), lambda b: (b, 0, 0)),
        out_shape=jax.ShapeDtypeStruct((bsz, s, w), BF16),
        scratch_shapes=[pltpu.VMEM((s, w), F32)] * 6 + [pltpu.VMEM((s, w), BF16)] * 4 + [
            pltpu.VMEM((s // GDN_CHUNK, 1, w), F32), pltpu.VMEM((s, w), F32)],
        compiler_params=pltpu.CompilerParams(
            dimension_semantics=("parallel",), vmem_limit_bytes=V7X_VMEM_LIMIT_BYTES),
        name="gated_deltanet",
    )(hg, conv_taps, alog_rep, dtb_rep, nw_rep)


NSA_TQ = 256
NSA_SLC_STEP = 256
NSA_NCMP_PAD = 128
NSA_VPAD = 16


def _softmax_cols(s):
    m = jnp.max(s, axis=0, keepdims=True)
    e = jnp.exp(s - m)
    return e, jnp.sum(e, axis=0, keepdims=True)


def _nsa_body(q_ref, gate_ref, cmp_ref, kv_ref, pe_ref, w1_ref, w2_ref, ovlt_ref, wbias_ref, dbias_ref, o_ref,
              kc_s, vct_s, ks_s, kw_s, vst_s, vwt_s, oslc_s, owin_s):
    qi = pl.program_id(2)
    tq, d, grp = NSA_TQ, HEAD_DIM, NSA_GROUP
    s_len = kv_ref.shape[1]
    n_sel = s_len // SEL_BLOCK
    seg = CMP_LEN // CMP_STRIDE
    band = WINDOW + tq
    wide = grp * tq

    @pl.when(qi == 0)
    def _():
        def relayout(i, carry):
            r0 = pl.multiple_of(i * LANES, LANES)
            blk = kv_ref[0, pl.ds(r0, LANES), :]
            block_of_row = (r0 + _iota2((LANES, LANES - d), 0)) >> SEL_SHIFT
            one_hot = (block_of_row == _iota2((LANES, LANES - d), 1)).astype(BF16)
            ks_s[pl.ds(r0, LANES), :] = jnp.concatenate([blk[:, 0:d].astype(BF16), one_hot], axis=1)
            kw_s[pl.ds(r0, LANES), :] = blk[:, 2 * d:3 * d].astype(BF16)
            blk_t = blk.astype(F32).T
            ones_row = (_iota2((NSA_VPAD, LANES), 0) == 0).astype(BF16)
            vst_s[:, pl.ds(r0, LANES)] = jnp.concatenate([blk_t[d:2 * d, :].astype(BF16), ones_row], axis=0)
            vwt_s[:, pl.ds(r0, LANES)] = jnp.concatenate([blk_t[3 * d:4 * d, :].astype(BF16), ones_row], axis=0)
            return carry

        lax.fori_loop(0, s_len // LANES, relayout, 0)
        pre = [jnp.zeros((NSA_NCMP_PAD, 2 * LANES), F32) for _ in range(seg)]
        for l in range(CMP_STRIDE):
            t_l = cmp_ref[0, pl.ds(l, NSA_NCMP_PAD, stride=CMP_STRIDE), :]
            for h in range(seg):
                lh = l + h * CMP_STRIDE
                pre[h] = pre[h] + _dot((t_l + pe_ref[lh:lh + 1, :]).astype(BF16), w1_ref[lh])
        hid = pre[0] + pltpu.roll(pre[1], NSA_NCMP_PAD - 1, 0)
        hid = hid * jax.nn.sigmoid(hid)
        kcv = _dot(hid.astype(BF16), w2_ref[...])
        kc_s[...] = kcv[:, 0:d].astype(BF16)
        vct_s[...] = kcv.T[d:2 * d, :].astype(BF16)

    ts = qi * tq
    q_t = (q_ref[0].astype(F32) * (d ** -0.5)).T
    q_t = jnp.concatenate([q_t[g * d:(g + 1) * d, :] for g in range(grp)], axis=1).astype(BF16)
    t_row = ts + (_iota2((1, wide), 1) & (tq - 1))
    t_row1 = t_row[:, 0:tq]

    w0 = pl.multiple_of(jnp.maximum(ts - WINDOW, 0), tq)
    sc_win = _dot(kw_s[pl.ds(w0, band), :], q_t) + wbias_ref[jnp.minimum(qi, WINDOW // tq)]
    m_win = jnp.max(sc_win, axis=0, keepdims=True)

    n_col =_iota2((NSA_NCMP_PAD, 1), 0)
    cmp_ok = (n_col * CMP_STRIDE + (CMP_LEN - 1) <= t_row) & (n_col < NSA_NCMP_PAD - 1)
    e, den = _softmax_cols(jnp.where(cmp_ok, _dot(kc_s[...], q_t), NEG))
    p_cmp = jnp.where(t_row >= CMP_LEN - 1, e / den, 0.0)
    o_cmp = _dot(vct_s[...], p_cmp.astype(BF16))

    p_sum = p_cmp[:, 0:tq]
    for g in range(1, grp):
        p_sum = p_sum + p_cmp[:, g * tq:(g + 1) * tq]
    imp = _dot_exact_lhs(ovlt_ref[...], p_sum, 3)
    j_idx = _iota2((n_sel, tq), 0)
    q_blk = t_row1 >> SEL_SHIFT
    forced = (j_idx == 0) | (j_idx == q_blk) | (j_idx == q_blk - 1)
    imp = jnp.where(forced, imp + FORCE_BONUS, imp)
    causal_blk = j_idx <= q_blk
    imp = jnp.where(causal_blk, imp, NEG)
    rank = jnp.zeros((n_sel, tq), F32)
    for jp in range(n_sel):
        row = imp[jp:jp + 1, :]
        beats = (row > imp) | ((row == imp) & (j_idx > jp))
        rank = rank + beats.astype(F32)
    selected = (rank < SEL_TOPN) & causal_blk

    blk0 = ts >> SEL_SHIFT
    sel_bias = jnp.where(selected, 0.0, NEG)
    past_bias = jnp.where(j_idx < blk0, sel_bias, NEG)
    zero_rows = jnp.zeros((LANES - d - n_sel, wide), BF16)

    def with_bias(bias):
        return jnp.concatenate([q_t, jnp.concatenate([bias.astype(BF16)] * grp, axis=1), zero_rows], axis=0)

    sc_diag = _dot(ks_s[pl.ds(ts, tq), :], with_bias(sel_bias)) + dbias_ref[...]
    q_past = with_bias(past_bias)

    ch = NSA_SLC_STEP

    def weighted_values(value_t, sc, m):
        return _dot(value_t, jnp.exp(sc - m).astype(BF16))

    def normalised(acc):
        return acc[0:d] / acc[d:d + 1]

    def slc_and_win(nk):
        n_past, n_win = nk // ch, band // ch
        past_sc, win_acc = [], None
        m = jnp.max(sc_diag, axis=0, keepdims=True)
        for i in range(max(n_past, n_win)):
            if i < n_past:
                past_sc.append(_dot(ks_s[i * ch:(i + 1) * ch, :], q_past))
                m = jnp.maximum(m, jnp.max(past_sc[-1], axis=0, keepdims=True))
            if i < n_win:
                term = weighted_values(vwt_s[:, pl.ds(w0 + i * ch, ch)], sc_win[i * ch:(i + 1) * ch], m_win)
                win_acc = term if win_acc is None else win_acc + term
        owin_s[...] = normalised(win_acc)
        acc = None
        for i in range(n_past):
            term = weighted_values(vst_s[:, i * ch:(i + 1) * ch], past_sc[i], m)
            acc = term if acc is None else acc + term
        term = weighted_values(vst_s[:, pl.ds(ts, tq)], sc_diag, m)
        oslc_s[...] = normalised(term if acc is None else acc + term)

    n_var = (s_len - tq + NSA_SLC_STEP - 1) // NSA_SLC_STEP + 1
    variant = (ts + NSA_SLC_STEP - 1) // NSA_SLC_STEP
    for v in range(n_var):
        @pl.when(variant == v)
        def _(v=v):
            slc_and_win(min(v * NSA_SLC_STEP, s_len))
    o_slc, o_win = oslc_s[...], owin_s[...]

    gt = jax.nn.sigmoid(gate_ref[0]).T
    outs = []
    for g in range(grp):
        cols = slice(g * tq, (g + 1) * tq)
        outs.append(gt[g:g + 1, :] * o_cmp[:, cols] + gt[grp + g:grp + g + 1, :] * o_slc[:, cols]
                    + gt[2 * grp + g:2 * grp + g + 1, :] * o_win[:, cols])
    o_ref[0] = jnp.concatenate(outs, axis=0).T.astype(o_ref.dtype)


def _attention_biases(tq):
    q = jnp.arange(NSA_GROUP * tq) & (tq - 1)
    r = jnp.arange(WINDOW + tq)
    lead = jnp.arange(WINDOW // tq + 1) * tq
    dist = lead[:, None, None] + q[None, None, :] - r[None, :, None]
    win_bias = jnp.where((dist >= 0) & (dist < WINDOW), 0.0, NEG).astype(F32)
    diag_bias = jnp.where(jnp.arange(tq)[:, None] <= q[None, :], 0.0, NEG).astype(F32)
    return win_bias, diag_bias


def _nsa_attention(h_mm, h_f32, pe, w1, w2, overlap):
    bsz, s, _ = h_mm.shape
    tq, d = NSA_TQ, HEAD_DIM
    qw = NSA_GROUP * d
    assert h_mm.shape[2] == NSA_KV_HEADS * 2 * qw and h_f32.shape[2] == NSA_KV_HEADS * 2 * LANES
    assert s % tq == 0 and s >= WINDOW + tq and (WINDOW + tq) % NSA_SLC_STEP == 0 and d + s // SEL_BLOCK <= LANES
    assert WINDOW % tq == 0
    win_bias, diag_bias = _attention_biases(tq)
    return pl.pallas_call(
        _nsa_body,
        grid=(bsz, NSA_KV_HEADS, s // tq),
        in_specs=[
            pl.BlockSpec((1, tq, qw), lambda b, h, i: (b, i, 2 * h)),
            pl.BlockSpec((1, tq, LANES), lambda b, h, i: (b, i, 2 * h)),
            pl.BlockSpec((1, s, LANES), lambda b, h, i: (b, 0, 2 * h + 1)),
            pl.BlockSpec((1, s, qw), lambda b, h, i: (b, 0, 2 * h + 1)),
            _const_spec(pe.shape), _const_spec(w1.shape), _const_spec(w2.shape), _const_spec(overlap.shape),
            _const_spec(win_bias.shape), _const_spec(diag_bias.shape),
        ],
        out_specs=pl.BlockSpec((1, tq, qw), lambda b, h, i: (b, i, h)),
        out_shape=jax.ShapeDtypeStruct((bsz, s, NSA_WIDTH), BF16),
        scratch_shapes=[pltpu.VMEM((NSA_NCMP_PAD, d), BF16), pltpu.VMEM((d, NSA_NCMP_PAD), BF16),
                        pltpu.VMEM((s, LANES), BF16), pltpu.VMEM((s, d), BF16),
                        pltpu.VMEM((d + NSA_VPAD, s), BF16), pltpu.VMEM((d + NSA_VPAD, s), BF16),
                        pltpu.VMEM((d, NSA_GROUP * tq), F32), pltpu.VMEM((d, NSA_GROUP * tq), F32)],
        compiler_params=pltpu.CompilerParams(
            dimension_semantics=("parallel", "parallel", "arbitrary"),
            vmem_limit_bytes=V7X_VMEM_LIMIT_BYTES),
        name="nsa_attention",
    )(h_mm, h_f32, h_f32, h_mm, pe, w1, w2, overlap, win_bias, diag_bias)


def _prep_w_in(w):
    w = w.astype(BF16)
    lead = w.shape[:-1]
    sizes = (3 * CONV_WIDTH, 3 * GDN_WIDTH, GDN_WIDTH, GDN_HEADS, GDN_HEADS,
             NSA_WIDTH, 6 * NSA_KV_HEADS * HEAD_DIM, 3 * NSA_Q_HEADS)
    parts, start = [], 0
    for n in sizes:
        parts.append(w[..., start:start + n])
        start += n
    wconv, gqkv, gz, ga, gb, nq, nkv, ngate = parts
    pad = lambda n: jnp.zeros(lead + (n,), w.dtype)
    wgdn = jnp.concatenate([gqkv, gz, ga, gb, pad(LANES - 2 * GDN_HEADS)], axis=-1)
    nkv = nkv.reshape(lead + (6, NSA_KV_HEADS, HEAD_DIM))
    ngate = ngate.reshape(lead + (NSA_KV_HEADS, NSA_GROUP, 3))
    mm_cols, f32_cols = [], []
    for h in range(NSA_KV_HEADS):
        qw = NSA_GROUP * HEAD_DIM
        gates = jnp.swapaxes(ngate[..., h, :, :], -1, -2).reshape(lead + (3 * NSA_GROUP,))
        kv = nkv[..., h, :].reshape(lead + (6 * HEAD_DIM,))
        mm_cols += [nq[..., h * qw:(h + 1) * qw], kv[..., 2 * HEAD_DIM:]]
        f32_cols += [gates, pad(LANES - 3 * NSA_GROUP), kv[..., :2 * HEAD_DIM]]
    return wconv, wgdn, jnp.concatenate(mm_cols, axis=-1), jnp.concatenate(f32_cols, axis=-1)


def _prep_cmp_weights(pe_k, pe_v, k_w1, k_w2, v_w1, v_w2):
    d, hid = HEAD_DIM, k_w1.shape[1]
    pe = jnp.concatenate([pe_k, pe_v], axis=1)
    k1 = k_w1.reshape(CMP_LEN, d, hid)
    v1 = v_w1.reshape(CMP_LEN, d, hid)
    z1 = jnp.zeros_like(k1)
    w1 = jnp.concatenate([jnp.concatenate([k1, z1], axis=2),
                          jnp.concatenate([z1, v1], axis=2)], axis=1)
    z2 = jnp.zeros_like(k_w2)
    w2 = jnp.concatenate([jnp.concatenate([k_w2, z2], axis=1),
                          jnp.concatenate([z2, v_w2], axis=1)], axis=0)
    return pe, w1.astype(BF16), w2.astype(BF16)


def _overlap_matrix(s):
    n_cmp = (s - CMP_LEN) // CMP_STRIDE + 1
    t = jnp.arange(s)
    starts = jnp.arange(NSA_NCMP_PAD) * CMP_STRIDE
    cmp_tok = (t[None, :] >= starts[:, None]) & (t[None, :] < starts[:, None] + CMP_LEN)
    cmp_tok = cmp_tok & (jnp.arange(NSA_NCMP_PAD) < n_cmp)[:, None]
    sel_tok = (t[None, :] // SEL_BLOCK) == jnp.arange(s // SEL_BLOCK)[:, None]
    return ((sel_tok.astype(F32) @ cmp_tok.astype(F32).T) / CMP_LEN).astype(BF16)


def kernel(x, ffn1_w_gate, ffn1_w_up, ffn1_w_down, ln1_g, ln1_b, w_in, conv_w, gdn_conv_w, gdn_a_log, gdn_dt_bias, gdn_norm_w, cmp_pe_k, cmp_pe_v, cmp_k_w1, cmp_k_w2, cmp_v_w1, cmp_v_w2, w_out, ln2_g, ln2_b, ffn2_w_gate, ffn2_w_up, ffn2_w_down, ln3_g, ln3_b):
    bsz, s, dm = x.shape
    depth = w_in.shape[0]
    alpha = (2 * depth) ** 0.25
    m = bsz * s
    overlap = _overlap_matrix(s)
    row = lambda v: v.reshape(1, -1)
    rows = lambda v: v[:, None, :]
    bf = lambda v: v.astype(BF16)
    ffn1, ffn2 = (bf(ffn1_w_gate), bf(ffn1_w_up), bf(ffn1_w_down)), (bf(ffn2_w_gate), bf(ffn2_w_up), bf(ffn2_w_down))
    w_in_groups = _prep_w_in(w_in)
    wo = bf(w_out)
    conv_taps = jnp.swapaxes(conv_w, 1, 2)
    h = x.reshape(m, dm)
    for l in range(depth):
        h = _ffn_ln(h, *ffn1, rows(ln1_g), rows(ln1_b), layer=l, alpha=alpha)

        hc, hg, hn_mm, hn_f32 = _in_proj(h, w_in_groups, (F32, F32, BF16, F32), layer=l)
        y_b = _gated_deltanet(hg.reshape(bsz, s, -1), gdn_conv_w[l].T,
                              row(jnp.pad(gdn_a_log[l], (0, LANES - GDN_HEADS))),
                              row(jnp.pad(gdn_dt_bias[l], (0, LANES - GDN_HEADS))),
                              row(jnp.tile(gdn_norm_w[l], GDN_HEADS)))
        pe, w1, w2 = _prep_cmp_weights(cmp_pe_k[l], cmp_pe_v[l], cmp_k_w1[l], cmp_k_w2[l], cmp_v_w1[l], cmp_v_w2[l])
        y_c = _nsa_attention(hn_mm.reshape(bsz, s, -1), hn_f32.reshape(bsz, s, -1), pe, w1, w2, overlap)
        h = _out_proj_ln(h, hc, y_b.reshape(m, -1), y_c.reshape(m, -1), conv_taps, wo,
                         rows(ln2_g), rows(ln2_b), ffn2, rows(ln3_g), rows(ln3_b),
                         layer=l, alpha=alpha, seq_len=s)
    return h.reshape(bsz, s, dm)
```
